```python
import math
import jax
import jax.numpy as jnp
from jax import lax
import numpy as np

D_MODEL = 1024
BATCH = 8
SEQ = 2048
DEPTH = 4

HEAD_DIM = 64
MIX_HALF = D_MODEL // 2
SGU_GROUPS = 4
SGU_WIDTH = MIX_HALF
SGU_GROUP_DIM = SGU_WIDTH // SGU_GROUPS
CHUNK = 128
DIFF_HEADS = MIX_HALF // (2 * HEAD_DIM)
DIFF_QK = DIFF_HEADS * 2 * HEAD_DIM
DIFF_V = DIFF_HEADS * 2 * HEAD_DIM
Q_BLOCK = 128
DIL_PAIRS = ((128, 1), (512, 4), (2048, 16))
DIL_GROUPS = len(DIL_PAIRS)
DIL_HEADS = MIX_HALF // HEAD_DIM
DIL_WIDTH = DIL_HEADS * HEAD_DIM
WIN_BLOCK = 128
CONV_WIDTH = MIX_HALF
CONV_K = 31
EVEN_IN = 2 * SGU_WIDTH + 2 * DIFF_QK + DIFF_V
ODD_IN = 2 * DIL_GROUPS * DIL_WIDTH + DIL_WIDTH + 2 * CONV_WIDTH
MIX_OUT = SGU_WIDTH + DIFF_V
N_EVEN = (DEPTH + 1) // 2
N_ODD = DEPTH // 2
N_EXPERTS = 16
N_EXPERT_GROUPS = 4
EXPERTS_PER_GROUP = N_EXPERTS // N_EXPERT_GROUPS
TOP_K = 2
EXPERT_FF = D_MODEL // 2
RMS_EPS = 1e-6
LN_EPS = 1e-5

kernel_name = 'hybrid_sgu_diffattn_dilated_conformer_moe'


def rms_norm(x, g):
    xf = x.astype(jnp.float32)
    y = xf * lax.rsqrt(jnp.mean(xf * xf, axis=-1, keepdims=True) + RMS_EPS)
    return (y * g.astype(jnp.float32)).astype(x.dtype)


def layer_norm(x, g, b):
    xf = x.astype(jnp.float32)
    mu = jnp.mean(xf, axis=-1, keepdims=True)
    var = jnp.mean(jnp.square(xf - mu), axis=-1, keepdims=True)
    y = (xf - mu) * lax.rsqrt(var + LN_EPS)
    return (y * g.astype(jnp.float32) + b.astype(jnp.float32)).astype(x.dtype)


def alibi_slopes(n_heads):
    return 2.0 ** (-8.0 * jnp.arange(1, n_heads + 1, dtype=jnp.float32) / n_heads)


def split_cols(x, sizes):
    outs, start = [], 0
    for s in sizes:
        outs.append(x[..., start:start + s])
        start += s
    return outs


def spatial_gating(u, v, ln_g, ln_b, w_s, b_s):
    bsz, seq, _ = v.shape
    v = layer_norm(v, ln_g, ln_b).reshape(bsz, seq // CHUNK, CHUNK, SGU_GROUPS, SGU_GROUP_DIM)
    causal = jnp.tril(jnp.ones((CHUNK, CHUNK), dtype=bool))
    w = jnp.where(causal, w_s, jnp.zeros_like(w_s))
    mixed = jnp.einsum('gts,bcsgd->bctgd', w, v) + b_s.T[None, None, :, :, None]
    return u * mixed.reshape(bsz, seq, SGU_WIDTH)


def differential_attention(q, k, v, lam, subln_g, lam_init):
    bsz, n_heads, _, seq, dh = q.shape
    n_blk = seq // Q_BLOCK
    scale = dh ** -0.5
    slopes = alibi_slopes(n_heads)
    k_pos = jnp.arange(seq)
    q_blocks = jnp.moveaxis(q.reshape(bsz, n_heads, 2, n_blk, Q_BLOCK, dh), 3, 0)

    def one_block(args):
        q_blk, blk = args
        q_pos = blk * Q_BLOCK + jnp.arange(Q_BLOCK)
        dist = q_pos[:, None] - k_pos[None, :]
        bias = -slopes[:, None, None] * dist.astype(jnp.float32)
        s = jnp.einsum('bhmqd,bhmkd->bhmqk', q_blk, k, preferred_element_type=jnp.float32)
        s = jnp.where(dist >= 0, s * scale + bias[None, :, None], -jnp.inf)
        p = jax.nn.softmax(s, axis=-1)
        a = p[:, :, 0] - lam * p[:, :, 1]
        return jnp.einsum('bhqk,bhkd->bhqd', a.astype(v.dtype), v)

    out = lax.map(one_block, (q_blocks, jnp.arange(n_blk)))
    out = jnp.moveaxis(out, 0, 2).reshape(bsz, n_heads, seq, 2 * dh)
    out = rms_norm(out, subln_g) * (1.0 - lam_init)
    return out.transpose(0, 2, 1, 3).reshape(bsz, seq, n_heads * 2 * dh)


def dilated_window_attention(q, k, v, slopes, window, dilation):
    bsz, n_heads, seq, dh = q.shape
    steps = window // dilation
    n_sub = seq // dilation
    n_blk = -(-n_sub // WIN_BLOCK)
    pad = n_blk * WIN_BLOCK - n_sub

    def to_sub(t):
        return t.reshape(bsz, n_heads, n_sub, dilation, dh).transpose(0, 1, 3, 2, 4)

    qs = jnp.pad(to_sub(q), ((0, 0), (0, 0), (0, 0), (0, pad), (0, 0)))
    kv_pad = ((0, 0), (0, 0), (0, 0), (WIN_BLOCK, pad), (0, 0))
    ks = jnp.pad(to_sub(k), kv_pad)
    vs = jnp.pad(to_sub(v), kv_pad)
    blk_shape = (bsz, n_heads, dilation, n_blk, WIN_BLOCK, dh)
    q_b = qs.reshape(blk_shape)

    def band(t):
        prev = t[:, :, :, :n_blk * WIN_BLOCK].reshape(blk_shape)
        cur = t[:, :, :, WIN_BLOCK:].reshape(blk_shape)
        return jnp.concatenate([prev, cur], axis=-2)

    k_b, v_b = band(ks), band(vs)
    i = jnp.arange(WIN_BLOCK)[:, None]
    j = jnp.arange(2 * WIN_BLOCK)[None, :]
    dist = i + WIN_BLOCK - j
    key_pos = (jnp.arange(n_blk)[:, None, None] - 1) * WIN_BLOCK + j[None]
    valid = (dist >= 0) & (dist <= steps) & (key_pos >= 0)
    bias = -slopes[:, None, None] * dist.astype(jnp.float32)
    s = jnp.einsum('bhrnqd,bhrnkd->bhrnqk', q_b, k_b, preferred_element_type=jnp.float32) * dh ** -0.5
    s = jnp.where(valid, s + bias[None, :, None, None], -jnp.inf)
    m = jnp.max(s, axis=-1, keepdims=True)
    p = jnp.exp(s - m)
    den = jnp.sum(p, axis=-1)
    o = jnp.einsum('bhrnqk,bhrnkd->bhrnqd', p.astype(v.dtype), v_b, preferred_element_type=jnp.float32)
    o = (o / den[..., None]).astype(v.dtype)
    lse = m[..., 0] + jnp.log(den)
    o = o.reshape(bsz, n_heads, dilation, n_blk * WIN_BLOCK, dh)[:, :, :, :n_sub]
    o = o.transpose(0, 1, 3, 2, 4).reshape(bsz, n_heads, seq, dh)
    lse = lse.reshape(bsz, n_heads, dilation, n_blk * WIN_BLOCK)[..., :n_sub]
    lse = lse.transpose(0, 1, 3, 2).reshape(bsz, n_heads, seq)
    return o, lse


def even_mixer(h, w_in, w_out, ln_g, ln_b, w_s, b_s, lq1, lk1, lq2, lk2, subln_g, lam_init):
    bsz, seq, _ = h.shape
    proj = h @ w_in
    u, v, q, k, vd = split_cols(proj, (SGU_WIDTH, SGU_WIDTH, DIFF_QK, DIFF_QK, DIFF_V))
    a_out = spatial_gating(jax.nn.gelu(u), jax.nn.gelu(v), ln_g, ln_b, w_s, b_s)
    q = q.reshape(bsz, seq, DIFF_HEADS, 2, HEAD_DIM).transpose(0, 2, 3, 1, 4)
    k = k.reshape(bsz, seq, DIFF_HEADS, 2, HEAD_DIM).transpose(0, 2, 3, 1, 4)
    vd = vd.reshape(bsz, seq, DIFF_HEADS, 2 * HEAD_DIM).transpose(0, 2, 1, 3)
    f32 = jnp.float32
    lam = (jnp.exp(jnp.sum(lq1.astype(f32) * lk1.astype(f32)))
           - jnp.exp(jnp.sum(lq2.astype(f32) * lk2.astype(f32))) + lam_init)
    b_out = differential_attention(q, k, vd, lam, subln_g, lam_init)
    return jnp.concatenate([a_out, b_out], axis=-1) @ w_out


def odd_mixer(h, w_in, w_out, conv_w, conv_b, conv_ln_g, conv_ln_b):
    bsz, seq, _ = h.shape
    proj = h @ w_in
    qc, kc, vc, dc = split_cols(proj, (DIL_GROUPS * DIL_WIDTH, DIL_GROUPS * DIL_WIDTH, DIL_WIDTH, 2 * CONV_WIDTH))

    def group_heads(t):
        return t.reshape(bsz, seq, DIL_GROUPS, DIL_HEADS, HEAD_DIM).transpose(2, 0, 3, 1, 4)

    qg, kg = group_heads(qc), group_heads(kc)
    v = vc.reshape(bsz, seq, DIL_HEADS, HEAD_DIM).transpose(0, 2, 1, 3)
    slopes = alibi_slopes(DIL_HEADS)
    outs, lses = [], []
    for g, (window, dilation) in enumerate(DIL_PAIRS):
        o, lse = dilated_window_attention(qg[g], kg[g], v, slopes, window, dilation)
        outs.append(o)
        lses.append(lse)
    wts = jax.nn.softmax(jnp.stack(lses), axis=0)
    c_out = jnp.sum(wts[..., None].astype(v.dtype) * jnp.stack(outs), axis=0)
    c_out = c_out.transpose(0, 2, 1, 3).reshape(bsz, seq, DIL_WIDTH)
    a, gate = split_cols(dc, (CONV_WIDTH, CONV_WIDTH))
    glu = a * jax.nn.sigmoid(gate)
    conv = lax.conv_general_dilated(glu, conv_w[:, None, :], window_strides=(1,),
                                    padding=((CONV_K - 1, 0),),
                                    dimension_numbers=('NWC', 'WIO', 'NWC'),
                                    feature_group_count=CONV_WIDTH) + conv_b
    d_out = jax.nn.silu(layer_norm(conv, conv_ln_g, conv_ln_b))
    return jnp.concatenate([c_out, d_out], axis=-1) @ w_out


def grouped_moe(h, router_w, router_b, w_gu, w_down):
    bsz, seq, dm = h.shape
    t = h.reshape(bsz * seq, dm)
    probs = jax.nn.softmax((t @ router_w).astype(jnp.float32), axis=-1)
    sel = (probs + router_b.astype(jnp.float32)).reshape(-1, N_EXPERT_GROUPS, EXPERTS_PER_GROUP)
    grp_score = jnp.sum(lax.top_k(sel, TOP_K)[0], axis=-1)
    grp = jnp.argmax(grp_score, axis=-1)
    in_grp = jnp.sum(sel * jax.nn.one_hot(grp, N_EXPERT_GROUPS, dtype=jnp.float32)[:, :, None], axis=1)
    _, local = lax.top_k(in_grp, TOP_K)
    experts = grp[:, None] * EXPERTS_PER_GROUP + local
    p_top = jnp.take_along_axis(probs, experts, axis=-1)
    gates = p_top / jnp.sum(p_top, axis=-1, keepdims=True)
    gate_dense = jnp.sum(jax.nn.one_hot(experts, N_EXPERTS, dtype=jnp.float32) * gates[..., None], axis=1)
    gu = jnp.einsum('td,edf->etf', t, w_gu)
    g, u = split_cols(gu, (EXPERT_FF, EXPERT_FF))
    act = jax.nn.silu(g) * u * gate_dense.T[:, :, None].astype(t.dtype)
    y = jnp.einsum('etf,efd->td', act, w_down)
    return y.reshape(bsz, seq, dm)


def setup_inputs(seed: int = 0) -> dict:
    key = jax.random.key(seed)
    ks = jax.random.split(key, 28)

    def nrm(k, shape, s):
        return jax.random.normal(k, shape, jnp.float32) * s

    return {
        'x': nrm(ks[0], (BATCH, SEQ, D_MODEL), 1.0),
        'c': nrm(ks[1], (BATCH, D_MODEL), 1.0),
        'norm1_g': 1.0 + nrm(ks[2], (DEPTH, D_MODEL), 0.02),
        'norm2_g': 1.0 + nrm(ks[3], (DEPTH, D_MODEL), 0.02),
        'ada_w': nrm(ks[4], (DEPTH, D_MODEL, 6 * D_MODEL), 0.5 * D_MODEL ** -0.5),
        'ada_b': nrm(ks[5], (DEPTH, 6 * D_MODEL), 0.02),
        'ev_w_in': nrm(ks[6], (N_EVEN, D_MODEL, EVEN_IN), D_MODEL ** -0.5),
        'ev_w_out': nrm(ks[7], (N_EVEN, MIX_OUT, D_MODEL), MIX_OUT ** -0.5),
        'sgu_ln_g': 1.0 + nrm(ks[8], (N_EVEN, SGU_WIDTH), 0.02),
        'sgu_ln_b': nrm(ks[9], (N_EVEN, SGU_WIDTH), 0.02),
        'sgu_w': nrm(ks[10], (N_EVEN, SGU_GROUPS, CHUNK, CHUNK), 0.5 * CHUNK ** -0.5),
        'sgu_b': 1.0 + nrm(ks[11], (N_EVEN, SGU_GROUPS, CHUNK), 0.02),
        'lam_q1': nrm(ks[12], (N_EVEN, HEAD_DIM), 0.1),
        'lam_k1': nrm(ks[13], (N_EVEN, HEAD_DIM), 0.1),
        'lam_q2': nrm(ks[14], (N_EVEN, HEAD_DIM), 0.1),
        'lam_k2': nrm(ks[15], (N_EVEN, HEAD_DIM), 0.1),
        'diff_subln_g': 1.0 + nrm(ks[16], (N_EVEN, 2 * HEAD_DIM), 0.02),
        'od_w_in': nrm(ks[17], (N_ODD, D_MODEL, ODD_IN), D_MODEL ** -0.5),
        'od_w_out': nrm(ks[18], (N_ODD, MIX_OUT, D_MODEL), MIX_OUT ** -0.5),
        'conv_w': nrm(ks[19], (N_ODD, CONV_K, CONV_WIDTH), CONV_K ** -0.5),
        'conv_b': nrm(ks[20], (N_ODD, CONV_WIDTH), 0.02),
        'conv_ln_g': 1.0 + nrm(ks[21], (N_ODD, CONV_WIDTH), 0.02),
        'conv_ln_b': nrm(ks[22], (N_ODD, CONV_WIDTH), 0.02),
        'router_w': nrm(ks[23], (D_MODEL, N_EXPERTS), D_MODEL ** -0.5),
        'router_b': nrm(ks[24], (N_EXPERTS,), 0.01),
        'moe_w_gu': nrm(ks[25], (DEPTH, N_EXPERTS, D_MODEL, 2 * EXPERT_FF), D_MODEL ** -0.5),
        'moe_w_down': nrm(ks[26], (DEPTH, N_EXPERTS, EXPERT_FF, D_MODEL), EXPERT_FF ** -0.5),
        'final_g': 1.0 + nrm(ks[27], (D_MODEL,), 0.02),
    }


def reference(x, c, norm1_g, norm2_g, ada_w, ada_b, ev_w_in, ev_w_out, sgu_ln_g, sgu_ln_b,
              sgu_w, sgu_b, lam_q1, lam_k1, lam_q2, lam_k2, diff_subln_g, od_w_in, od_w_out,
              conv_w, conv_b, conv_ln_g, conv_ln_b, router_w, router_b, moe_w_gu, moe_w_down,
              final_g):
    bsz = x.shape[0]
    c_act = jax.nn.silu(c)
    for l in range(DEPTH):
        mod = (c_act @ ada_w[l] + ada_b[l]).reshape(bsz, 6, 1, D_MODEL)
        sh1, sc1, g1, sh2, sc2, g2 = (mod[:, i] for i in range(6))
        h = rms_norm(x, norm1_g[l]) * (1.0 + sc1) + sh1
        i = l // 2
        if l % 2 == 0:
            lam_init = 0.8 - 0.6 * math.exp(-0.3 * l)
            y = even_mixer(h, ev_w_in[i], ev_w_out[i], sgu_ln_g[i], sgu_ln_b[i], sgu_w[i], sgu_b[i],
                           lam_q1[i], lam_k1[i], lam_q2[i], lam_k2[i], diff_subln_g[i], lam_init)
        else:
            y = odd_mixer(h, od_w_in[i], od_w_out[i], conv_w[i], conv_b[i], conv_ln_g[i], conv_ln_b[i])
        x = x + g1 * y
        h = rms_norm(x, norm2_g[l]) * (1.0 + sc2) + sh2
        x = x + g2 * grouped_moe(h, router_w, router_b, moe_w_gu[l], moe_w_down[l])
    return rms_norm(x, final_g)
```

```python
import functools
import math

import jax
import jax.numpy as jnp
from jax import lax
from jax.experimental import pallas as pl
from jax.experimental.pallas import tpu as pltpu

F32 = jnp.float32
BF16 = jnp.bfloat16

HEAD_DIM = 64
LANES = 128
CHUNK = 128
SGU_GROUPS = 4
DIL_PAIRS = ((128, 1), (512, 4), (2048, 16))
WIN_BLOCK = 128
CONV_K = 31
N_EXPERTS = 16
EXPERTS_PER_GROUP = 4
RMS_EPS = 1e-6
LN_EPS = 1e-5
NEG = -1e30
VMEM_LIMIT = 56 * 1024 * 1024


def _params(sem):
    return pltpu.CompilerParams(dimension_semantics=sem, vmem_limit_bytes=VMEM_LIMIT)


def _gelu(x):
    return x * (0.5 * (1.0 + jnp.tanh(0.7978845608028654 * (x + 0.044715 * (x * x * x)))))


def _silu(x):
    return x * jax.nn.sigmoid(x)


def _rms(x, g):
    ms = jnp.mean(x * x, axis=-1, keepdims=True)
    return x * lax.rsqrt(ms + RMS_EPS) * g


def _layer_norm(x, g, b):
    mu = jnp.mean(x, axis=-1, keepdims=True)
    xc = x - mu
    var = jnp.mean(xc * xc, axis=-1, keepdims=True)
    return xc * lax.rsqrt(var + LN_EPS) * g + b


def _mod_kernel(c_ref, w_ref, b_ref, o_ref):
    ca = _silu(c_ref[...]).astype(BF16)
    o_ref[...] = jnp.dot(ca, w_ref[...].astype(BF16), preferred_element_type=F32) + b_ref[...]


def _modulation(c, ada_w, ada_b):
    depth, d, six_d = ada_w.shape
    bsz = c.shape[0]
    n = six_d // d
    out = pl.pallas_call(
        _mod_kernel,
        out_shape=jax.ShapeDtypeStruct((depth, bsz, six_d), F32),
        grid=(depth, n),
        in_specs=[
            pl.BlockSpec((bsz, d), lambda l, j: (0, 0)),
            pl.BlockSpec((None, d, d), lambda l, j: (l, 0, j)),
            pl.BlockSpec((None, 1, d), lambda l, j: (l, 0, j)),
        ],
        out_specs=pl.BlockSpec((None, bsz, d), lambda l, j: (l, 0, j)),
        compiler_params=_params(("parallel", "parallel")),
    )(c, ada_w, ada_b.reshape(depth, 1, six_d))
    return out.reshape(depth, bsz, n, d)


def _inproj_kernel(x_ref, mod_ref, g_ref, w_ref, o_ref, *, n_chunk):
    h = _rms(x_ref[...], g_ref[...]) * (1.0 + mod_ref[1:2, :]) + mod_ref[0:1, :]
    hb = h.astype(BF16)
    for j in range(0, o_ref.shape[-1], n_chunk):
        o_ref[:, j:j + n_chunk] = jnp.dot(
            hb, w_ref[:, j:j + n_chunk], preferred_element_type=F32).astype(o_ref.dtype)


def _in_projection(x2, mod_l, g, w_bf, seq, tm=512):
    t, d = x2.shape
    n = w_bf.shape[1]
    per_b = seq // tm
    return pl.pallas_call(
        functools.partial(_inproj_kernel, n_chunk=512),
        out_shape=jax.ShapeDtypeStruct((t, n), BF16),
        grid=(t // tm,),
        in_specs=[
            pl.BlockSpec((tm, d), lambda i: (i, 0)),
            pl.BlockSpec((None, 6, d), lambda i: (i // per_b, 0, 0)),
            pl.BlockSpec((1, d), lambda i: (0, 0)),
            pl.BlockSpec((d, n), lambda i: (0, 0)),
        ],
        out_specs=pl.BlockSpec((tm, n), lambda i: (i, 0)),
        compiler_params=_params(("parallel",)),
    )(x2, mod_l, g.reshape(1, d), w_bf)


def _sgu_kernel(u_ref, v_ref, lng_ref, lnb_ref, w_ref, bias_ref, o_ref):
    u = _gelu(u_ref[...].astype(F32))
    v = _gelu(v_ref[...].astype(F32))
    vb = _layer_norm(v, lng_ref[...], lnb_ref[...]).astype(BF16)
    row = lax.broadcasted_iota(jnp.int32, (CHUNK, CHUNK), 0)
    col = lax.broadcasted_iota(jnp.int32, (CHUNK, CHUNK), 1)
    causal = col <= row
    gd = u.shape[1] // SGU_GROUPS
    for g in range(SGU_GROUPS):
        w = jnp.where(causal, w_ref[g], 0.0).astype(BF16)
        cs = slice(g * gd, (g + 1) * gd)
        for c in range(u.shape[0] // CHUNK):
            rs = slice(c * CHUNK, (c + 1) * CHUNK)
            mixed = jnp.dot(w, vb[rs, cs], preferred_element_type=F32) + bias_ref[:, cs]
            o_ref[rs, cs] = (u[rs, cs] * mixed).astype(o_ref.dtype)


def _spatial_gating(proj, ln_g, ln_b, w_s, b_s, width, tm=512):
    t = proj.shape[0]
    gd = width // SGU_GROUPS
    bias = jnp.repeat(b_s.T, gd, axis=1)
    return pl.pallas_call(
        _sgu_kernel,
        out_shape=jax.ShapeDtypeStruct((t, width), BF16),
        grid=(t // tm,),
        in_specs=[
            pl.BlockSpec((tm, width), lambda i: (i, 0)),
            pl.BlockSpec((tm, width), lambda i: (i, 1)),
            pl.BlockSpec((1, width), lambda i: (0, 0)),
            pl.BlockSpec((1, width), lambda i: (0, 0)),
            pl.BlockSpec((SGU_GROUPS, CHUNK, CHUNK), lambda i: (0, 0, 0)),
            pl.BlockSpec((CHUNK, width), lambda i: (0, 0)),
        ],
        out_specs=pl.BlockSpec((tm, width), lambda i: (i, 0)),
        compiler_params=_params(("parallel",)),
    )(proj, proj, ln_g.reshape(1, width), ln_b.reshape(1, width), w_s, bias)


def _diff_kernel(slopes_ref, lq1_ref, lk1_ref, lq2_ref, lk2_ref, q_ref, k_ref, v_ref, g_ref,
                 o_ref, *, tq, lam_init):
    h = pl.program_id(1)
    qi = pl.program_id(2)
    slope = slopes_ref[h]
    scale = HEAD_DIM ** -0.5
    lam = (jnp.exp(jnp.sum(lq1_ref[...] * lk1_ref[...], axis=-1, keepdims=True))
           - jnp.exp(jnp.sum(lq2_ref[...] * lk2_ref[...], axis=-1, keepdims=True)) + lam_init)

    q = q_ref[...]
    lane = lax.broadcasted_iota(jnp.int32, q.shape, 1)
    qs = (jnp.where(lane < HEAD_DIM, q, jnp.zeros_like(q)),
          jnp.where(lane >= HEAD_DIM, q, jnp.zeros_like(q)))
    rel_i = (lax.broadcasted_iota(jnp.int32, (tq, tq), 1)
             - lax.broadcasted_iota(jnp.int32, (tq, tq), 0))
    rel = rel_i.astype(F32)

    def tile(j, carry, diag):
        start = pl.multiple_of(j * tq, tq)
        kt = k_ref[pl.ds(start, tq), :]
        vt = v_ref[pl.ds(start, tq), :]
        bias = slope * (rel + ((j - qi) * tq).astype(F32))
        out = []
        for m in range(2):
            m_old, l_old, a_old = carry[m]
            s = lax.dot_general(qs[m], kt, (((1,), (1,)), ((), ())),
                                preferred_element_type=F32) * scale + bias
            if diag:
                s = jnp.where(rel_i <= 0, s, NEG)
            m_new = jnp.maximum(m_old, jnp.max(s, axis=-1, keepdims=True))
            alpha = jnp.exp(m_old - m_new)
            p = jnp.exp(s - m_new)
            l_new = alpha * l_old + jnp.sum(p, axis=-1, keepdims=True)
            a_new = alpha * a_old + jnp.dot(p.astype(BF16), vt, preferred_element_type=F32)
            out.append((m_new, l_new, a_new))
        return tuple(out)

    init = tuple((jnp.full((tq, 1), NEG, F32), jnp.zeros((tq, 1), F32),
                  jnp.zeros((tq, LANES), F32)) for _ in range(2))
    carry = lax.fori_loop(0, qi, lambda j, c: tile(j, c, False), init)
    (_, l1, a1), (_, l2, a2) = tile(qi, carry, True)
    o = a1 / l1 - lam * (a2 / l2)
    o_ref[...] = (_rms(o, g_ref[...]) * (1.0 - lam_init)).astype(o_ref.dtype)


def _diff_attention(proj3, lq1, lk1, lq2, lk2, subln_g, lam_init, n_heads, col0, tq=256):
    bsz, seq, _ = proj3.shape
    slopes = 2.0 ** (-8.0 * jnp.arange(1, n_heads + 1, dtype=F32) / n_heads)
    qb, kb, vb = col0 // LANES, col0 // LANES + n_heads, col0 // LANES + 2 * n_heads
    vec = lambda a: a.reshape(1, -1)
    small = lambda n: pl.BlockSpec((1, n), lambda b, h, i: (0, 0))
    return pl.pallas_call(
        functools.partial(_diff_kernel, tq=tq, lam_init=lam_init),
        out_shape=jax.ShapeDtypeStruct((bsz, seq, n_heads * LANES), BF16),
        grid=(bsz, n_heads, seq // tq),
        in_specs=[
            pl.BlockSpec(memory_space=pltpu.SMEM),
            small(HEAD_DIM), small(HEAD_DIM), small(HEAD_DIM), small(HEAD_DIM),
            pl.BlockSpec((None, tq, LANES), lambda b, h, i: (b, i, qb + h)),
            pl.BlockSpec((None, seq, LANES), lambda b, h, i: (b, 0, kb + h)),
            pl.BlockSpec((None, seq, LANES), lambda b, h, i: (b, 0, vb + h)),
            small(LANES),
        ],
        out_specs=pl.BlockSpec((None, tq, LANES), lambda b, h, i: (b, i, h)),
        compiler_params=_params(("parallel", "parallel", "arbitrary")),
    )(slopes, vec(lq1), vec(lk1), vec(lq2), vec(lk2), proj3, proj3, proj3, vec(subln_g))


def _dil_kernel(slopes_ref, q1_ref, q2_ref, q3_ref, k1_ref, k2_ref, k3_ref, v_ref, o_ref,
                qf, kf, vf, acc_s, m_s, l_s, *, seq):
    hp = pl.program_id(1)
    wb = WIN_BLOCK
    scale = HEAD_DIM ** -0.5
    n_q_blocks = seq // wb

    qf[0] = q2_ref[...].astype(F32)
    qf[1] = q3_ref[...].astype(F32)
    kf[0] = k2_ref[...].astype(F32)
    kf[1] = k3_ref[...].astype(F32)
    vf[...] = v_ref[...].astype(F32)

    lane = lax.broadcasted_iota(jnp.int32, (wb, LANES), 1)
    first = lane < HEAD_DIM
    ii = lax.broadcasted_iota(jnp.int32, (wb, wb), 0)
    jj = lax.broadcasted_iota(jnp.int32, (wb, wb), 1)
    dist_cur = (ii - jj).astype(F32)
    dist_prev = dist_cur + float(wb)
    cur_ok = jj <= ii
    prev_ok = jj >= ii

    for g, (window, dil) in enumerate(DIL_PAIRS):
        assert window // dil == wb
        nb = n_q_blocks // dil
        q_refs = (q1_ref, q2_ref, q3_ref)
        k_refs = (k1_ref, k2_ref, k3_ref)

        def rows(kind, start, g=g, dil=dil, q_refs=q_refs, k_refs=k_refs):
            if dil == 1:
                ref = {"q": q_refs[0], "k": k_refs[0], "v": v_ref}[kind]
                return ref[pl.ds(pl.multiple_of(start, wb), wb), :]
            ref = {"q": qf.at[g - 1], "k": kf.at[g - 1], "v": vf}[kind]
            return ref[pl.ds(start, wb, stride=dil), :].astype(BF16)

        def body(idx, _, g=g, dil=dil, nb=nb, rows=rows):
            r = idx // nb
            n = idx % nb
            cur = n * (wb * dil) + r
            prev = jnp.maximum(n - 1, 0) * (wb * dil) + r
            q = rows("q", cur)
            kc, vc = rows("k", cur), rows("v", cur)
            if nb > 1:
                kp, vp = rows("k", prev), rows("v", prev)
                has_prev = n >= 1
            accs, ms, ls = [], [], []
            for hh in range(2):
                slope = slopes_ref[2 * hp + hh]
                qm = jnp.where(first if hh == 0 else jnp.logical_not(first), q, jnp.zeros_like(q))
                sc = lax.dot_general(qm, kc, (((1,), (1,)), ((), ())),
                                     preferred_element_type=F32) * scale - slope * dist_cur
                sc = jnp.where(cur_ok, sc, NEG)
                m = jnp.max(sc, axis=-1, keepdims=True)
                if nb > 1:
                    sp = lax.dot_general(qm, kp, (((1,), (1,)), ((), ())),
                                         preferred_element_type=F32) * scale - slope * dist_prev
                    sp = jnp.where(jnp.logical_and(prev_ok, has_prev), sp, NEG)
                    m = jnp.maximum(m, jnp.max(sp, axis=-1, keepdims=True))
                pc = jnp.exp(sc - m)
                l = jnp.sum(pc, axis=-1, keepdims=True)
                acc = jnp.dot(pc.astype(BF16), vc, preferred_element_type=F32)
                if nb > 1:
                    pp = jnp.exp(sp - m)
                    l = l + jnp.sum(pp, axis=-1, keepdims=True)
                    acc = acc + jnp.dot(pp.astype(BF16), vp, preferred_element_type=F32)
                accs.append(acc)
                ms.append(jnp.broadcast_to(m, (wb, LANES)))
                ls.append(jnp.broadcast_to(l, (wb, LANES)))
            acc = jnp.where(first, accs[0], accs[1])
            m = jnp.where(first, ms[0], ms[1])
            l = jnp.where(first, ls[0], ls[1])
            if dil == 1:
                dst = pl.ds(pl.multiple_of(cur, wb), wb)
            else:
                dst = pl.ds(cur, wb, stride=dil)
            acc_s[g, dst, :] = acc
            m_s[g, dst, :] = m
            l_s[g, dst, :] = l
            return 0

        lax.fori_loop(0, n_q_blocks, body, 0)

    mr = 256
    for c in range(seq // mr):
        rs = slice(c * mr, (c + 1) * mr)
        m1, m2, m3 = m_s[0, rs, :], m_s[1, rs, :], m_s[2, rs, :]
        m = jnp.maximum(jnp.maximum(m1, m2), m3)
        w1, w2, w3 = jnp.exp(m1 - m), jnp.exp(m2 - m), jnp.exp(m3 - m)
        num = w1 * acc_s[0, rs, :] + w2 * acc_s[1, rs, :] + w3 * acc_s[2, rs, :]
        den = w1 * l_s[0, rs, :] + w2 * l_s[1, rs, :] + w3 * l_s[2, rs, :]
        o_ref[rs, :] = (num / den).astype(o_ref.dtype)


def _dilated_attention(proj3, n_heads):
    bsz, seq, _ = proj3.shape
    n_pairs = n_heads * HEAD_DIM // LANES
    slopes = 2.0 ** (-8.0 * jnp.arange(1, n_heads + 1, dtype=F32) / n_heads)
    groups = len(DIL_PAIRS)

    def spec(blk0):
        return pl.BlockSpec((None, seq, LANES), lambda b, p: (b, 0, blk0 + p))

    q_specs = [spec(g * n_pairs) for g in range(groups)]
    k_specs = [spec((groups + g) * n_pairs) for g in range(groups)]
    v_spec = spec(2 * groups * n_pairs)
    return pl.pallas_call(
        functools.partial(_dil_kernel, seq=seq),
        out_shape=jax.ShapeDtypeStruct((bsz, seq, n_pairs * LANES), BF16),
        grid=(bsz, n_pairs),
        in_specs=[pl.BlockSpec(memory_space=pltpu.SMEM)] + q_specs + k_specs + [v_spec],
        out_specs=pl.BlockSpec((None, seq, LANES), lambda b, p: (b, 0, p)),
        scratch_shapes=[
            pltpu.VMEM((2, seq, LANES), F32),
            pltpu.VMEM((2, seq, LANES), F32),
            pltpu.VMEM((seq, LANES), F32),
            pltpu.VMEM((groups, seq, LANES), F32),
            pltpu.VMEM((groups, seq, LANES), F32),
            pltpu.VMEM((groups, seq, LANES), F32),
        ],
        compiler_params=_params(("parallel", "parallel")),
    )(slopes, *([proj3] * 7))


def _conv_kernel(a_ref, gate_ref, w_ref, cb_ref, lng_ref, lnb_ref, o_ref, buf, *, ts, halo, rc):
    @pl.when(pl.program_id(1) == 0)
    def _():
        buf[0:halo, :] = jnp.zeros((halo, buf.shape[1]), F32)

    buf[halo:halo + ts, :] = a_ref[...].astype(F32) * jax.nn.sigmoid(gate_ref[...].astype(F32))
    off = halo - (CONV_K - 1)
    for c in range(ts // rc):
        acc = jnp.zeros((rc, buf.shape[1]), F32)
        for k in range(CONV_K):
            acc = acc + w_ref[k:k + 1, :] * buf[c * rc + off + k:c * rc + off + k + rc, :]
        y = _layer_norm(acc + cb_ref[...], lng_ref[...], lnb_ref[...])
        o_ref[c * rc:(c + 1) * rc, :] = _silu(y).astype(o_ref.dtype)
    buf[0:halo, :] = buf[ts:ts + halo, :]


def _conv_module(proj3, conv_w, conv_b, ln_g, ln_b, col0, ts=256):
    bsz, seq, _ = proj3.shape
    width = conv_w.shape[1]
    halo = 32
    vec = lambda a: a.reshape(1, width)
    small = pl.BlockSpec((1, width), lambda b, s: (0, 0))
    return pl.pallas_call(
        functools.partial(_conv_kernel, ts=ts, halo=halo, rc=64),
        out_shape=jax.ShapeDtypeStruct((bsz, seq, width), BF16),
        grid=(bsz, seq // ts),
        in_specs=[
            pl.BlockSpec((None, ts, width), lambda b, s: (b, s, col0 // width)),
            pl.BlockSpec((None, ts, width), lambda b, s: (b, s, col0 // width + 1)),
            pl.BlockSpec((CONV_K, width), lambda b, s: (0, 0)),
            small, small, small,
        ],
        out_specs=pl.BlockSpec((None, ts, width), lambda b, s: (b, s, 0)),
        scratch_shapes=[pltpu.VMEM((ts + halo, width), F32)],
        compiler_params=_params(("parallel", "arbitrary")),
    )(proj3, proj3, conv_w, vec(conv_b), vec(ln_g), vec(ln_b))


def _route(logits_t, rb):
    mx = jnp.max(logits_t, axis=0, keepdims=True)
    ex = jnp.exp(logits_t - mx)
    probs = ex / jnp.sum(ex, axis=0, keepdims=True)
    sel = probs + rb
    srow = [sel[i:i + 1, :] for i in range(N_EXPERTS)]
    prow = [probs[i:i + 1, :] for i in range(N_EXPERTS)]
    epg = EXPERTS_PER_GROUP
    n_grp = N_EXPERTS // epg

    scores = []
    for g in range(n_grp):
        a, b, c, d = srow[g * epg:(g + 1) * epg]
        hi1, lo1 = jnp.maximum(a, b), jnp.minimum(a, b)
        hi2, lo2 = jnp.maximum(c, d), jnp.minimum(c, d)
        scores.append(jnp.maximum(hi1, hi2)
                      + jnp.maximum(jnp.minimum(hi1, hi2), jnp.maximum(lo1, lo2)))
    best = scores[0]
    grp = jnp.zeros(best.shape, jnp.int32)
    for g in range(1, n_grp):
        better = scores[g] > best
        grp = jnp.where(better, g, grp)
        best = jnp.where(better, scores[g], best)

    def pick(rows_, j):
        out = rows_[j]
        for g in range(1, n_grp):
            out = jnp.where(grp == g, rows_[g * epg + j], out)
        return out

    ing = [pick(srow, j) for j in range(epg)]
    ping = [pick(prow, j) for j in range(epg)]
    b0, i0, p0 = ing[0], jnp.zeros(best.shape, jnp.int32), ping[0]
    for j in range(1, epg):
        better = ing[j] > b0
        i0 = jnp.where(better, j, i0)
        p0 = jnp.where(better, ping[j], p0)
        b0 = jnp.where(better, ing[j], b0)
    b1 = jnp.full(best.shape, -jnp.inf, F32)
    i1 = jnp.zeros(best.shape, jnp.int32)
    p1 = jnp.zeros(best.shape, F32)
    for j in range(epg):
        better = jnp.logical_and(i0 != j, ing[j] > b1)
        i1 = jnp.where(better, j, i1)
        p1 = jnp.where(better, ping[j], p1)
        b1 = jnp.where(better, ing[j], b1)
    den = p0 + p1
    g0, g1 = p0 / den, p1 / den
    e0, e1 = grp * epg + i0, grp * epg + i1
    eid = lax.broadcasted_iota(jnp.int32, (LANES, e0.shape[1]), 0)
    return jnp.where(eid == e0, g0, 0.0) + jnp.where(eid == e1, g1, 0.0)


def _outproj_kernel(a_ref, b_ref, x_ref, mod_ref, g_ref, w_ref, rwh_ref, rwl_ref, rb_ref,
                    xo_ref, h_ref, gd_ref):
    ka = a_ref.shape[1]
    y = (jnp.dot(a_ref[...], w_ref[0:ka, :], preferred_element_type=F32)
         + jnp.dot(b_ref[...], w_ref[ka:, :], preferred_element_type=F32))
    xn = x_ref[...] + mod_ref[2:3, :] * y
    xo_ref[...] = xn
    h = _rms(xn, g_ref[...]) * (1.0 + mod_ref[4:5, :]) + mod_ref[3:4, :]
    hb = h.astype(BF16)
    h_ref[...] = hb
    hl = (h - hb.astype(F32)).astype(BF16)
    logits = (jnp.dot(hb, rwh_ref[...], preferred_element_type=F32)
              + jnp.dot(hb, rwl_ref[...], preferred_element_type=F32)
              + jnp.dot(hl, rwh_ref[...], preferred_element_type=F32))
    lt = logits.T
    gd_ref[...] = _route(lt[0:N_EXPERTS, :], rb_ref[...]).T


def _out_projection(a, b, x2, mod_l, g, w_bf, rw_hi, rw_lo, rb, seq, tm=512):
    t, d = x2.shape
    ka, kb = a.shape[1], b.shape[1]
    per_b = seq // tm
    return pl.pallas_call(
        _outproj_kernel,
        out_shape=(jax.ShapeDtypeStruct((t, d), F32),
                   jax.ShapeDtypeStruct((t, d), BF16),
                   jax.ShapeDtypeStruct((t, LANES), F32)),
        grid=(t // tm,),
        in_specs=[
            pl.BlockSpec((tm, ka), lambda i: (i, 0)),
            pl.BlockSpec((tm, kb), lambda i: (i, 0)),
            pl.BlockSpec((tm, d), lambda i: (i, 0)),
            pl.BlockSpec((None, 6, d), lambda i: (i // per_b, 0, 0)),
            pl.BlockSpec((1, d), lambda i: (0, 0)),
            pl.BlockSpec((ka + kb, d), lambda i: (0, 0)),
            pl.BlockSpec((d, LANES), lambda i: (0, 0)),
            pl.BlockSpec((d, LANES), lambda i: (0, 0)),
            pl.BlockSpec((N_EXPERTS, 1), lambda i: (0, 0)),
        ],
        out_specs=(pl.BlockSpec((tm, d), lambda i: (i, 0)),
                   pl.BlockSpec((tm, d), lambda i: (i, 0)),
                   pl.BlockSpec((tm, LANES), lambda i: (i, 0))),
        compiler_params=_params(("parallel",)),
    )(a, b, x2, mod_l, g.reshape(1, d), w_bf, rw_hi, rw_lo, rb.reshape(N_EXPERTS, 1))


def _moe_kernel(h_ref, gd_ref, wgu_ref, wd_ref, x_ref, mod_ref, fg_ref, o_ref, acc, *, final):
    e = pl.program_id(1)

    @pl.when(e == 0)
    def _():
        acc[...] = jnp.zeros_like(acc)

    ff = wd_ref.shape[0]
    gu = jnp.dot(h_ref[...], wgu_ref[...].astype(BF16), preferred_element_type=F32)
    lane = lax.broadcasted_iota(jnp.int32, gd_ref.shape, 1)
    gate = jnp.sum(jnp.where(lane == e, gd_ref[...], 0.0), axis=1, keepdims=True)
    act = _silu(gu[:, :ff]) * gu[:, ff:] * gate
    acc[...] += jnp.dot(act.astype(BF16), wd_ref[...].astype(BF16), preferred_element_type=F32)

    @pl.when(e == pl.num_programs(1) - 1)
    def _():
        xn = x_ref[...] + mod_ref[5:6, :] * acc[...]
        if final:
            xn = _rms(xn, fg_ref[...])
        o_ref[...] = xn


def _moe(h, gd, w_gu, w_down, x2, mod_l, final_g, seq, final, tm=1024):
    t, d = x2.shape
    n_exp, _, two_ff = w_gu.shape
    ff = two_ff // 2
    per_b = seq // tm
    return pl.pallas_call(
        functools.partial(_moe_kernel, final=final),
        out_shape=jax.ShapeDtypeStruct((t, d), F32),
        grid=(t // tm, n_exp),
        in_specs=[
            pl.BlockSpec((tm, d), lambda i, e: (i, 0)),
            pl.BlockSpec((tm, LANES), lambda i, e: (i, 0)),
            pl.BlockSpec((None, d, two_ff), lambda i, e: (e, 0, 0)),
            pl.BlockSpec((None, ff, d), lambda i, e: (e, 0, 0)),
            pl.BlockSpec((tm, d), lambda i, e: (i, 0)),
            pl.BlockSpec((None, 6, d), lambda i, e: (i // per_b, 0, 0)),
            pl.BlockSpec((1, d), lambda i, e: (0, 0)),
        ],
        out_specs=pl.BlockSpec((tm, d), lambda i, e: (i, 0)),
        scratch_shapes=[pltpu.VMEM((tm, d), F32)],
        compiler_params=_params(("parallel", "arbitrary")),
    )(h, gd, w_gu, w_down, x2, mod_l, final_g.reshape(1, d))


def kernel(x, c, norm1_g, norm2_g, ada_w, ada_b, ev_w_in, ev_w_out, sgu_ln_g, sgu_ln_b, sgu_w, sgu_b, lam_q1, lam_k1, lam_q2, lam_k2, diff_subln_g, od_w_in, od_w_out, conv_w, conv_b, conv_ln_g, conv_ln_b, router_w, router_b, moe_w_gu, moe_w_down, final_g):
    bsz, seq, d = x.shape
    depth = ada_w.shape[0]
    half = d // 2
    diff_heads = half // (2 * HEAD_DIM)
    dil_heads = half // HEAD_DIM
    t = bsz * seq

    mod = _modulation(c, ada_w, ada_b)
    rw = jnp.pad(router_w, ((0, 0), (0, LANES - N_EXPERTS)))
    rw_hi = rw.astype(BF16)
    rw_lo = (rw - rw_hi.astype(F32)).astype(BF16)

    x2 = x.reshape(t, d)
    for l in range(depth):
        i = l // 2
        if l % 2 == 0:
            lam_init = 0.8 - 0.6 * math.exp(-0.3 * l)
            proj = _in_projection(x2, mod[l], norm1_g[l], ev_w_in[i].astype(BF16), seq)
            part_a = _spatial_gating(proj, sgu_ln_g[i], sgu_ln_b[i], sgu_w[i], sgu_b[i], half)
            part_b = _diff_attention(proj.reshape(bsz, seq, -1), lam_q1[i], lam_k1[i], lam_q2[i],
                                     lam_k2[i], diff_subln_g[i], lam_init, diff_heads, 2 * half)
            w_out = ev_w_out[i]
        else:
            proj = _in_projection(x2, mod[l], norm1_g[l], od_w_in[i].astype(BF16), seq)
            proj3 = proj.reshape(bsz, seq, -1)
            part_a = _dilated_attention(proj3, dil_heads).reshape(t, half)
            part_b = _conv_module(proj3, conv_w[i], conv_b[i], conv_ln_g[i], conv_ln_b[i],
                                  (2 * len(DIL_PAIRS) + 1) * half)
            w_out = od_w_out[i]
        x2, h2, gd = _out_projection(part_a, part_b.reshape(t, half), x2, mod[l], norm2_g[l],
                                     w_out.astype(BF16), rw_hi, rw_lo, router_b, seq)
        x2 = _moe(h2, gd, moe_w_gu[l], moe_w_down[l], x2, mod[l], final_g, seq,
                  final=(l == depth - 1))
    return x2.reshape(bsz, seq, d)
```

```python
import functools
import math

import jax
import jax.numpy as jnp
from jax import lax
from jax.experimental import pallas as pl
from jax.experimental.pallas import tpu as pltpu

F32 = jnp.float32
BF16 = jnp.bfloat16

HEAD_DIM = 64
LANES = 128
CHUNK = 128
SGU_GROUPS = 4
DIL_PAIRS = ((128, 1), (512, 4), (2048, 16))
WIN_BLOCK = 128
CONV_K = 31
N_EXPERTS = 16
EXPERTS_PER_GROUP = 4
RMS_EPS = 1e-6
LN_EPS = 1e-5
NEG = -1e30
VMEM_LIMIT = 56 * 1024 * 1024


def _params(sem):
    return pltpu.CompilerParams(dimension_semantics=sem, vmem_limit_bytes=VMEM_LIMIT)


def _gelu(x):
    return x * (0.5 * (1.0 + jnp.tanh(0.7978845608028654 * (x + 0.044715 * (x * x * x)))))


def _silu(x):
    return x * jax.nn.sigmoid(x)


def _rms(x, g):
    ms = jnp.mean(x * x, axis=-1, keepdims=True)
    return x * lax.rsqrt(ms + RMS_EPS) * g


def _layer_norm(x, g, b):
    mu = jnp.mean(x, axis=-1, keepdims=True)
    xc = x - mu
    var = jnp.mean(xc * xc, axis=-1, keepdims=True)
    return xc * lax.rsqrt(var + LN_EPS) * g + b


def _mod_kernel(c_ref, w_ref, b_ref, o_ref):
    ca = _silu(c_ref[...]).astype(BF16)
    o_ref[...] = jnp.dot(ca, w_ref[...].astype(BF16), preferred_element_type=F32) + b_ref[...]


def _modulation(c, ada_w, ada_b):
    depth, d, six_d = ada_w.shape
    bsz = c.shape[0]
    n = six_d // d
    out = pl.pallas_call(
        _mod_kernel,
        out_shape=jax.ShapeDtypeStruct((depth, bsz, six_d), F32),
        grid=(depth, n),
        in_specs=[
            pl.BlockSpec((bsz, d), lambda l, j: (0, 0)),
            pl.BlockSpec((None, d, d), lambda l, j: (l, 0, j)),
            pl.BlockSpec((None, 1, d), lambda l, j: (l, 0, j)),
        ],
        out_specs=pl.BlockSpec((None, bsz, d), lambda l, j: (l, 0, j)),
        compiler_params=_params(("parallel", "parallel")),
        name="modulation",
    )(c, ada_w, ada_b.reshape(depth, 1, six_d))
    return out.reshape(depth, bsz, n, d)


def _inproj_kernel(x_ref, mod_ref, g_ref, w_ref, o_ref, *, n_chunk):
    h = _rms(x_ref[...], g_ref[...]) * (1.0 + mod_ref[1:2, :]) + mod_ref[0:1, :]
    hb = h.astype(BF16)
    for j in range(0, o_ref.shape[-1], n_chunk):
        o_ref[:, j:j + n_chunk] = jnp.dot(
            hb, w_ref[:, j:j + n_chunk], preferred_element_type=F32).astype(o_ref.dtype)


def _in_projection(x2, mod_l, g, w_bf, seq, tm=512):
    t, d = x2.shape
    n = w_bf.shape[1]
    per_b = seq // tm
    return pl.pallas_call(
        functools.partial(_inproj_kernel, n_chunk=512),
        out_shape=jax.ShapeDtypeStruct((t, n), BF16),
        grid=(t // tm,),
        in_specs=[
            pl.BlockSpec((tm, d), lambda i: (i, 0)),
            pl.BlockSpec((None, 6, d), lambda i: (i // per_b, 0, 0)),
            pl.BlockSpec((1, d), lambda i: (0, 0)),
            pl.BlockSpec((d, n), lambda i: (0, 0)),
        ],
        out_specs=pl.BlockSpec((tm, n), lambda i: (i, 0)),
        compiler_params=_params(("parallel",)),
        name="in_projection",
    )(x2, mod_l, g.reshape(1, d), w_bf)


def _sgu_kernel(u_ref, v_ref, lng_ref, lnb_ref, w_ref, bias_ref, o_ref):
    u = _gelu(u_ref[...].astype(F32))
    v = _gelu(v_ref[...].astype(F32))
    vb = _layer_norm(v, lng_ref[...], lnb_ref[...]).astype(BF16)
    row = lax.broadcasted_iota(jnp.int32, (CHUNK, CHUNK), 0)
    col = lax.broadcasted_iota(jnp.int32, (CHUNK, CHUNK), 1)
    causal = col <= row
    gd = u.shape[1] // SGU_GROUPS
    for g in range(SGU_GROUPS):
        w = jnp.where(causal, w_ref[g], 0.0).astype(BF16)
        cs = slice(g * gd, (g + 1) * gd)
        for c in range(u.shape[0] // CHUNK):
            rs = slice(c * CHUNK, (c + 1) * CHUNK)
            mixed = jnp.dot(w, vb[rs, cs], preferred_element_type=F32) + bias_ref[:, cs]
            o_ref[rs, cs] = (u[rs, cs] * mixed).astype(o_ref.dtype)


def _spatial_gating(proj, ln_g, ln_b, w_s, b_s, width, tm=512):
    t = proj.shape[0]
    gd = width // SGU_GROUPS
    bias = jnp.repeat(b_s.T, gd, axis=1)
    return pl.pallas_call(
        _sgu_kernel,
        out_shape=jax.ShapeDtypeStruct((t, width), BF16),
        grid=(t // tm,),
        in_specs=[
            pl.BlockSpec((tm, width), lambda i: (i, 0)),
            pl.BlockSpec((tm, width), lambda i: (i, 1)),
            pl.BlockSpec((1, width), lambda i: (0, 0)),
            pl.BlockSpec((1, width), lambda i: (0, 0)),
            pl.BlockSpec((SGU_GROUPS, CHUNK, CHUNK), lambda i: (0, 0, 0)),
            pl.BlockSpec((CHUNK, width), lambda i: (0, 0)),
        ],
        out_specs=pl.BlockSpec((tm, width), lambda i: (i, 0)),
        compiler_params=_params(("parallel",)),
        name="spatial_gating",
    )(proj, proj, ln_g.reshape(1, width), ln_b.reshape(1, width), w_s, bias)


ONES_ROWS = 16


def _split3(x):
    hi = x.astype(BF16).astype(F32)
    r1 = x - hi
    mid = r1.astype(BF16).astype(F32)
    lo = (r1 - mid).astype(BF16).astype(F32)
    return hi, mid, lo


def _diff_kernel(slopes_ref, lq1_ref, lk1_ref, lq2_ref, lk2_ref, q_ref, k_ref, v_ref, g_ref,
                 o_ref, kx, vt, acc, *, tq, lam_init):
    h = pl.program_id(1)
    qi = pl.program_id(2)
    seq, vd = v_ref.shape
    hd = HEAD_DIM
    lane = lax.broadcasted_iota(jnp.int32, (tq, LANES), 1)

    def place(pieces, base, fill):
        out = fill
        for n, piece in enumerate(pieces):
            out = jnp.where(lane == base + n, piece, out)
        return out

    @pl.when(qi == 0)
    def _():
        slope = slopes_ref[h]
        for c in range(seq // tq):
            rs = slice(c * tq, (c + 1) * tq)
            k = k_ref[rs, :].astype(F32)
            pos = (lax.broadcasted_iota(jnp.int32, (tq, LANES), 0) + c * tq).astype(F32) * slope
            pieces = _split3(pos)
            zero = jnp.zeros_like(k)
            kx[0, rs, :] = jnp.where(lane < hd, k, place(pieces, hd, zero)).astype(BF16)
            kx[1, rs, :] = jnp.where(lane >= hd, k, place(pieces, 0, zero)).astype(BF16)
            vt[c, 0:vd, :] = v_ref[rs, :].astype(F32).T.astype(BF16)
            vt[c, vd:, :] = jnp.ones((ONES_ROWS, tq), BF16)

    q = q_ref[...].astype(F32) * hd ** -0.5
    zero = jnp.zeros_like(q)
    one = jnp.ones_like(q)
    qx = (jnp.where(lane < hd, q, place((one, one, one), hd, zero)).astype(BF16),
          jnp.where(lane >= hd, q, place((one, one, one), 0, zero)).astype(BF16))
    acc[...] = jnp.zeros_like(acc)
    key_le_query = (lax.broadcasted_iota(jnp.int32, (tq, tq), 0)
                    <= lax.broadcasted_iota(jnp.int32, (tq, tq), 1))

    def tile(j, ms, diag):
        start = pl.multiple_of(j * tq, tq)
        out = []
        for m in range(2):
            s = lax.dot_general(kx[m, pl.ds(start, tq), :], qx[m], (((1,), (1,)), ((), ())),
                                preferred_element_type=F32)
            if diag:
                s = jnp.where(key_le_query, s, NEG)
            m_new = jnp.maximum(ms[m], jnp.max(s, axis=0, keepdims=True))
            alpha = jnp.exp(ms[m] - m_new)
            p = jnp.exp(s - m_new).astype(BF16)
            acc[m] = alpha * acc[m] + jnp.dot(vt[j], p, preferred_element_type=F32)
            out.append(m_new)
        return tuple(out)

    init = (jnp.full((1, tq), NEG, F32), jnp.full((1, tq), NEG, F32))
    ms = lax.fori_loop(0, qi, lambda j, c: tile(j, c, False), init)
    tile(qi, ms, True)

    lam = (jnp.exp(jnp.sum(lq1_ref[...] * lk1_ref[...], axis=-1, keepdims=True))
           - jnp.exp(jnp.sum(lq2_ref[...] * lk2_ref[...], axis=-1, keepdims=True)) + lam_init)
    o_t = (acc[0, 0:vd, :] / acc[0, vd:vd + 1, :]
           - lam * (acc[1, 0:vd, :] / acc[1, vd:vd + 1, :]))
    o_ref[...] = (_rms(o_t.T, g_ref[...]) * (1.0 - lam_init)).astype(o_ref.dtype)


def _diff_attention(proj3, lq1, lk1, lq2, lk2, subln_g, lam_init, n_heads, col0, tq=512):
    bsz, seq, _ = proj3.shape
    slopes = 2.0 ** (-8.0 * jnp.arange(1, n_heads + 1, dtype=F32) / n_heads)
    qb, kb, vb = col0 // LANES, col0 // LANES + n_heads, col0 // LANES + 2 * n_heads
    vec = lambda a: a.reshape(1, -1)
    small = lambda n: pl.BlockSpec((1, n), lambda b, h, i: (0, 0))
    return pl.pallas_call(
        functools.partial(_diff_kernel, tq=tq, lam_init=lam_init),
        out_shape=jax.ShapeDtypeStruct((bsz, seq, n_heads * LANES), BF16),
        grid=(bsz, n_heads, seq // tq),
        in_specs=[
            pl.BlockSpec(memory_space=pltpu.SMEM),
            small(HEAD_DIM), small(HEAD_DIM), small(HEAD_DIM), small(HEAD_DIM),
            pl.BlockSpec((None, tq, LANES), lambda b, h, i: (b, i, qb + h)),
            pl.BlockSpec((None, seq, LANES), lambda b, h, i: (b, 0, kb + h)),
            pl.BlockSpec((None, seq, LANES), lambda b, h, i: (b, 0, vb + h)),
            small(LANES),
        ],
        out_specs=pl.BlockSpec((None, tq, LANES), lambda b, h, i: (b, i, h)),
        scratch_shapes=[
            pltpu.VMEM((2, seq, LANES), BF16),
            pltpu.VMEM((seq // tq, LANES + ONES_ROWS, tq), BF16),
            pltpu.VMEM((2, LANES + ONES_ROWS, tq), F32),
        ],
        compiler_params=_params(("parallel", "parallel", "arbitrary")),
        name="diff_attention",
    )(slopes, vec(lq1), vec(lk1), vec(lq2), vec(lk2), proj3, proj3, proj3, vec(subln_g))


def _dil_kernel(slopes_ref, q1_ref, q2_ref, q3_ref, k1_ref, k2_ref, k3_ref, v_ref, o_ref,
                qf, kf, vf, bias_s, acc_s, m_s, l_s, *, seq):
    hp = pl.program_id(1)
    wb = WIN_BLOCK
    hd = HEAD_DIM
    n_q_blocks = seq // wb

    qf[0] = q2_ref[...].astype(F32)
    qf[1] = q3_ref[...].astype(F32)
    kf[0] = k2_ref[...].astype(F32)
    kf[1] = k3_ref[...].astype(F32)
    vf[...] = v_ref[...].astype(F32)

    row = lax.broadcasted_iota(jnp.int32, (2 * wb, 2 * wb), 0)
    col = lax.broadcasted_iota(jnp.int32, (2 * wb, 2 * wb), 1)
    qi_ = jnp.where(row < wb, row, row - wb)
    dist = qi_ + wb - col
    slope = jnp.where(row < wb, slopes_ref[2 * hp], slopes_ref[2 * hp + 1])
    valid = jnp.logical_and(dist >= 0, dist <= wb)
    table = jnp.where(valid, -slope * dist.astype(F32), NEG)
    bias_s[1] = table
    bias_s[0] = jnp.where(col < wb, NEG, table)

    first = lax.broadcasted_iota(jnp.int32, (wb, LANES), 1) < hd
    q_scale = hd ** -0.5

    for g, (window, dil) in enumerate(DIL_PAIRS):
        assert window // dil == wb
        nb = n_q_blocks // dil
        refs = {"q": (q1_ref, qf), "k": (k1_ref, kf), "v": (v_ref, vf)}

        def rows(kind, start, g=g, dil=dil, refs=refs):
            direct, copies = refs[kind]
            if dil == 1:
                return direct[pl.ds(pl.multiple_of(start, wb), wb), :]
            src = copies if kind == "v" else copies.at[g - 1]
            return src[pl.ds(start, wb, stride=dil), :]

        def body(idx, _, g=g, dil=dil, nb=nb, rows=rows):
            r = idx // nb
            n = idx % nb
            cur = n * (wb * dil) + r
            q = rows("q", cur).astype(F32) * q_scale
            zero = jnp.zeros_like(q)
            qx = jnp.concatenate([jnp.where(first, q, zero), jnp.where(first, zero, q)],
                                 axis=0).astype(BF16)
            if nb > 1:
                prev = jnp.maximum(n - 1, 0) * (wb * dil) + r
                keys = jnp.concatenate([rows("k", prev), rows("k", cur)], axis=0).astype(BF16)
                vals = jnp.concatenate([rows("v", prev), rows("v", cur)], axis=0).astype(BF16)
                bias = bias_s[jnp.minimum(n, 1)]
            else:
                keys, vals = rows("k", cur).astype(BF16), rows("v", cur).astype(BF16)
                bias = bias_s[1, :, wb:]
            vx = jnp.concatenate([vals, jnp.ones_like(vals)], axis=1)
            s = lax.dot_general(qx, keys, (((1,), (1,)), ((), ())),
                                preferred_element_type=F32) + bias
            m = jnp.max(s, axis=-1, keepdims=True)
            p = jnp.exp(s - m).astype(BF16)
            o = jnp.dot(p, vx, preferred_element_type=F32)
            acc = jnp.where(first, o[0:wb, 0:LANES], o[wb:, 0:LANES])
            den = jnp.where(first, o[0:wb, LANES:], o[wb:, LANES:])
            mx = jnp.where(first, jnp.broadcast_to(m[0:wb], (wb, LANES)),
                           jnp.broadcast_to(m[wb:], (wb, LANES)))
            if dil == 1:
                dst = pl.ds(pl.multiple_of(cur, wb), wb)
            else:
                dst = pl.ds(cur, wb, stride=dil)
            acc_s[g, dst, :] = acc
            m_s[g, dst, :] = mx
            l_s[g, dst, :] = den
            return 0

        lax.fori_loop(0, n_q_blocks, body, 0, unroll=8)

    mr = 256
    for c in range(seq // mr):
        rs = slice(c * mr, (c + 1) * mr)
        m1, m2, m3 = m_s[0, rs, :], m_s[1, rs, :], m_s[2, rs, :]
        m = jnp.maximum(jnp.maximum(m1, m2), m3)
        w1, w2, w3 = jnp.exp(m1 - m), jnp.exp(m2 - m), jnp.exp(m3 - m)
        num = w1 * acc_s[0, rs, :] + w2 * acc_s[1, rs, :] + w3 * acc_s[2, rs, :]
        den = w1 * l_s[0, rs, :] + w2 * l_s[1, rs, :] + w3 * l_s[2, rs, :]
        o_ref[rs, :] = (num / den).astype(o_ref.dtype)


def _dilated_attention(proj3, n_heads):
    bsz, seq, _ = proj3.shape
    n_pairs = n_heads * HEAD_DIM // LANES
    slopes = 2.0 ** (-8.0 * jnp.arange(1, n_heads + 1, dtype=F32) / n_heads)
    groups = len(DIL_PAIRS)

    def spec(blk0):
        return pl.BlockSpec((None, seq, LANES), lambda b, p: (b, 0, blk0 + p))

    q_specs = [spec(g * n_pairs) for g in range(groups)]
    k_specs = [spec((groups + g) * n_pairs) for g in range(groups)]
    v_spec = spec(2 * groups * n_pairs)
    return pl.pallas_call(
        functools.partial(_dil_kernel, seq=seq),
        out_shape=jax.ShapeDtypeStruct((bsz, seq, n_pairs * LANES), BF16),
        grid=(bsz, n_pairs),
        in_specs=[pl.BlockSpec(memory_space=pltpu.SMEM)] + q_specs + k_specs + [v_spec],
        out_specs=pl.BlockSpec((None, seq, LANES), lambda b, p: (b, 0, p)),
        scratch_shapes=[
            pltpu.VMEM((groups - 1, seq, LANES), F32),
            pltpu.VMEM((groups - 1, seq, LANES), F32),
            pltpu.VMEM((seq, LANES), F32),
            pltpu.VMEM((2, 2 * WIN_BLOCK, 2 * WIN_BLOCK), F32),
            pltpu.VMEM((groups, seq, LANES), F32),
            pltpu.VMEM((groups, seq, LANES), F32),
            pltpu.VMEM((groups, seq, LANES), F32),
        ],
        compiler_params=_params(("parallel", "parallel")),
        name="dilated_attention",
    )(slopes, *([proj3] * 7))


def _conv_kernel(a_ref, gate_ref, w_ref, cb_ref, lng_ref, lnb_ref, o_ref, buf, *, ts, halo, rc):
    @pl.when(pl.program_id(1) == 0)
    def _():
        buf[0:halo, :] = jnp.zeros((halo, buf.shape[1]), F32)

    buf[halo:halo + ts, :] = a_ref[...].astype(F32) * jax.nn.sigmoid(gate_ref[...].astype(F32))
    off = halo - (CONV_K - 1)
    for c in range(ts // rc):
        acc = jnp.zeros((rc, buf.shape[1]), F32)
        for k in range(CONV_K):
            acc = acc + w_ref[k:k + 1, :] * buf[c * rc + off + k:c * rc + off + k + rc, :]
        y = _layer_norm(acc + cb_ref[...], lng_ref[...], lnb_ref[...])
        o_ref[c * rc:(c + 1) * rc, :] = _silu(y).astype(o_ref.dtype)
    buf[0:halo, :] = buf[ts:ts + halo, :]


def _conv_module(proj3, conv_w, conv_b, ln_g, ln_b, col0, ts=256):
    bsz, seq, _ = proj3.shape
    width = conv_w.shape[1]
    halo = 32
    vec = lambda a: a.reshape(1, width)
    small = pl.BlockSpec((1, width), lambda b, s: (0, 0))
    return pl.pallas_call(
        functools.partial(_conv_kernel, ts=ts, halo=halo, rc=64),
        out_shape=jax.ShapeDtypeStruct((bsz, seq, width), BF16),
        grid=(bsz, seq // ts),
        in_specs=[
            pl.BlockSpec((None, ts, width), lambda b, s: (b, s, col0 // width)),
            pl.BlockSpec((None, ts, width), lambda b, s: (b, s, col0 // width + 1)),
            pl.BlockSpec((CONV_K, width), lambda b, s: (0, 0)),
            small, small, small,
        ],
        out_specs=pl.BlockSpec((None, ts, width), lambda b, s: (b, s, 0)),
        scratch_shapes=[pltpu.VMEM((ts + halo, width), F32)],
        compiler_params=_params(("parallel", "arbitrary")),
        name="conv_module",
    )(proj3, proj3, conv_w, vec(conv_b), vec(ln_g), vec(ln_b))


def _route(logits_t, rb):
    mx = jnp.max(logits_t, axis=0, keepdims=True)
    ex = jnp.exp(logits_t - mx)
    probs = ex / jnp.sum(ex, axis=0, keepdims=True)
    sel = probs + rb
    srow = [sel[i:i + 1, :] for i in range(N_EXPERTS)]
    prow = [probs[i:i + 1, :] for i in range(N_EXPERTS)]
    epg = EXPERTS_PER_GROUP
    n_grp = N_EXPERTS // epg

    scores = []
    for g in range(n_grp):
        a, b, c, d = srow[g * epg:(g + 1) * epg]
        hi1, lo1 = jnp.maximum(a, b), jnp.minimum(a, b)
        hi2, lo2 = jnp.maximum(c, d), jnp.minimum(c, d)
        scores.append(jnp.maximum(hi1, hi2)
                      + jnp.maximum(jnp.minimum(hi1, hi2), jnp.maximum(lo1, lo2)))
    best = scores[0]
    grp = jnp.zeros(best.shape, jnp.int32)
    for g in range(1, n_grp):
        better = scores[g] > best
        grp = jnp.where(better, g, grp)
        best = jnp.where(better, scores[g], best)

    def pick(rows_, j):
        out = rows_[j]
        for g in range(1, n_grp):
            out = jnp.where(grp == g, rows_[g * epg + j], out)
        return out

    ing = [pick(srow, j) for j in range(epg)]
    ping = [pick(prow, j) for j in range(epg)]
    b0, i0, p0 = ing[0], jnp.zeros(best.shape, jnp.int32), ping[0]
    for j in range(1, epg):
        better = ing[j] > b0
        i0 = jnp.where(better, j, i0)
        p0 = jnp.where(better, ping[j], p0)
        b0 = jnp.where(better, ing[j], b0)
    b1 = jnp.full(best.shape, -jnp.inf, F32)
    i1 = jnp.zeros(best.shape, jnp.int32)
    p1 = jnp.zeros(best.shape, F32)
    for j in range(epg):
        better = jnp.logical_and(i0 != j, ing[j] > b1)
        i1 = jnp.where(better, j, i1)
        p1 = jnp.where(better, ping[j], p1)
        b1 = jnp.where(better, ing[j], b1)
    den = p0 + p1
    g0, g1 = p0 / den, p1 / den
    e0, e1 = grp * epg + i0, grp * epg + i1
    eid = lax.broadcasted_iota(jnp.int32, (LANES, e0.shape[1]), 0)
    return jnp.where(eid == e0, g0, 0.0) + jnp.where(eid == e1, g1, 0.0)


def _outproj_kernel(a_ref, b_ref, x_ref, mod_ref, g_ref, w_ref, rwh_ref, rwl_ref, rb_ref,
                    xo_ref, h_ref, gd_ref):
    ka = a_ref.shape[1]
    y = (jnp.dot(a_ref[...], w_ref[0:ka, :], preferred_element_type=F32)
         + jnp.dot(b_ref[...], w_ref[ka:, :], preferred_element_type=F32))
    xn = x_ref[...] + mod_ref[2:3, :] * y
    xo_ref[...] = xn
    h = _rms(xn, g_ref[...]) * (1.0 + mod_ref[4:5, :]) + mod_ref[3:4, :]
    hb = h.astype(BF16)
    h_ref[...] = hb
    hl = (h - hb.astype(F32)).astype(BF16)
    logits = (jnp.dot(hb, rwh_ref[...], preferred_element_type=F32)
              + jnp.dot(hb, rwl_ref[...], preferred_element_type=F32)
              + jnp.dot(hl, rwh_ref[...], preferred_element_type=F32))
    lt = logits.T
    gd_ref[...] = _route(lt[0:N_EXPERTS, :], rb_ref[...]).T


def _out_projection(a, b, x2, mod_l, g, w_bf, rw_hi, rw_lo, rb, seq, tm=512):
    t, d = x2.shape
    ka, kb = a.shape[1], b.shape[1]
    per_b = seq // tm
    return pl.pallas_call(
        _outproj_kernel,
        out_shape=(jax.ShapeDtypeStruct((t, d), F32),
                   jax.ShapeDtypeStruct((t, d), BF16),
                   jax.ShapeDtypeStruct((t, LANES), F32)),
        grid=(t // tm,),
        in_specs=[
            pl.BlockSpec((tm, ka), lambda i: (i, 0)),
            pl.BlockSpec((tm, kb), lambda i: (i, 0)),
            pl.BlockSpec((tm, d), lambda i: (i, 0)),
            pl.BlockSpec((None, 6, d), lambda i: (i // per_b, 0, 0)),
            pl.BlockSpec((1, d), lambda i: (0, 0)),
            pl.BlockSpec((ka + kb, d), lambda i: (0, 0)),
            pl.BlockSpec((d, LANES), lambda i: (0, 0)),
            pl.BlockSpec((d, LANES), lambda i: (0, 0)),
            pl.BlockSpec((N_EXPERTS, 1), lambda i: (0, 0)),
        ],
        out_specs=(pl.BlockSpec((tm, d), lambda i: (i, 0)),
                   pl.BlockSpec((tm, d), lambda i: (i, 0)),
                   pl.BlockSpec((tm, LANES), lambda i: (i, 0))),
        compiler_params=_params(("parallel",)),
        name="out_projection",
    )(a, b, x2, mod_l, g.reshape(1, d), w_bf, rw_hi, rw_lo, rb.reshape(N_EXPERTS, 1))


def _moe_kernel(h_ref, gd_ref, wgu_ref, wd_ref, x_ref, mod_ref, fg_ref, o_ref, acc, *, final):
    e = pl.program_id(1)

    @pl.when(e == 0)
    def _():
        acc[...] = jnp.zeros_like(acc)

    ff = wd_ref.shape[0]
    gu = jnp.dot(h_ref[...], wgu_ref[...].astype(BF16), preferred_element_type=F32)
    lane = lax.broadcasted_iota(jnp.int32, gd_ref.shape, 1)
    gate = jnp.sum(jnp.where(lane == e, gd_ref[...], 0.0), axis=1, keepdims=True)
    act = _silu(gu[:, :ff]) * gu[:, ff:] * gate
    acc[...] += jnp.dot(act.astype(BF16), wd_ref[...].astype(BF16), preferred_element_type=F32)

    @pl.when(e == pl.num_programs(1) - 1)
    def _():
        xn = x_ref[...] + mod_ref[5:6, :] * acc[...]
        if final:
            xn = _rms(xn, fg_ref[...])
        o_ref[...] = xn


def _moe(h, gd, w_gu, w_down, layer, x2, mod_l, final_g, seq, final, tm=1024):
    t, d = x2.shape
    _, n_exp, _, two_ff = w_gu.shape
    ff = two_ff // 2
    per_b = seq // tm
    return pl.pallas_call(
        functools.partial(_moe_kernel, final=final),
        out_shape=jax.ShapeDtypeStruct((t, d), F32),
        grid=(t // tm, n_exp),
        in_specs=[
            pl.BlockSpec((tm, d), lambda i, e: (i, 0)),
            pl.BlockSpec((tm, LANES), lambda i, e: (i, 0)),
            pl.BlockSpec((None, None, d, two_ff), lambda i, e: (layer, e, 0, 0)),
            pl.BlockSpec((None, None, ff, d), lambda i, e: (layer, e, 0, 0)),
            pl.BlockSpec((tm, d), lambda i, e: (i, 0)),
            pl.BlockSpec((None, 6, d), lambda i, e: (i // per_b, 0, 0)),
            pl.BlockSpec((1, d), lambda i, e: (0, 0)),
        ],
        out_specs=pl.BlockSpec((tm, d), lambda i, e: (i, 0)),
        scratch_shapes=[pltpu.VMEM((tm, d), F32)],
        compiler_params=_params(("parallel", "arbitrary")),
        name="moe",
    )(h, gd, w_gu, w_down, x2, mod_l, final_g.reshape(1, d))


def kernel(x, c, norm1_g, norm2_g, ada_w, ada_b, ev_w_in, ev_w_out, sgu_ln_g, sgu_ln_b, sgu_w, sgu_b, lam_q1, lam_k1, lam_q2, lam_k2, diff_subln_g, od_w_in, od_w_out, conv_w, conv_b, conv_ln_g, conv_ln_b, router_w, router_b, moe_w_gu, moe_w_down, final_g):
    bsz, seq, d = x.shape
    depth = ada_w.shape[0]
    half = d // 2
    diff_heads = half // (2 * HEAD_DIM)
    dil_heads = half // HEAD_DIM
    t = bsz * seq

    mod = _modulation(c, ada_w, ada_b)
    rw = jnp.pad(router_w, ((0, 0), (0, LANES - N_EXPERTS)))
    rw_hi = rw.astype(BF16)
    rw_lo = (rw - rw_hi.astype(F32)).astype(BF16)

    x2 = x.reshape(t, d)
    for l in range(depth):
        i = l // 2
        if l % 2 == 0:
            lam_init = 0.8 - 0.6 * math.exp(-0.3 * l)
            proj = _in_projection(x2, mod[l], norm1_g[l], ev_w_in[i].astype(BF16), seq)
            part_a = _spatial_gating(proj, sgu_ln_g[i], sgu_ln_b[i], sgu_w[i], sgu_b[i], half)
            part_b = _diff_attention(proj.reshape(bsz, seq, -1), lam_q1[i], lam_k1[i], lam_q2[i],
                                     lam_k2[i], diff_subln_g[i], lam_init, diff_heads, 2 * half)
            w_out = ev_w_out[i]
        else:
            proj = _in_projection(x2, mod[l], norm1_g[l], od_w_in[i].astype(BF16), seq)
            proj3 = proj.reshape(bsz, seq, -1)
            part_a = _dilated_attention(proj3, dil_heads).reshape(t, half)
            part_b = _conv_module(proj3, conv_w[i], conv_b[i], conv_ln_g[i], conv_ln_b[i],
                                  (2 * len(DIL_PAIRS) + 1) * half)
            w_out = od_w_out[i]
        x2, h2, gd = _out_projection(part_a, part_b.reshape(t, half), x2, mod[l], norm2_g[l],
                                     w_out.astype(BF16), rw_hi, rw_lo, router_b, seq)
        x2 = _moe(h2, gd, moe_w_gu, moe_w_down, l, x2, mod[l], final_g, seq,
                  final=(l == depth - 1))
    return x2.reshape(bsz, seq, d)
```

```python
import functools
import math

import jax
import jax.numpy as jnp
from jax import lax
from jax.experimental import pallas as pl
from jax.experimental.pallas import tpu as pltpu

F32 = jnp.float32
BF16 = jnp.bfloat16

HEAD_DIM = 64
LANES = 128
CHUNK = 128
SGU_GROUPS = 4
DIL_PAIRS = ((128, 1), (512, 4), (2048, 16))
WIN_BLOCK = 128
CONV_K = 31
N_EXPERTS = 16
EXPERTS_PER_GROUP = 4
RMS_EPS = 1e-6
LN_EPS = 1e-5
NEG = -1e30
VMEM_LIMIT = 56 * 1024 * 1024


def _params(sem):
    return pltpu.CompilerParams(dimension_semantics=sem, vmem_limit_bytes=VMEM_LIMIT)


def _gelu(x):
    return x * (0.5 * (1.0 + jnp.tanh(0.7978845608028654 * (x + 0.044715 * (x * x * x)))))


def _silu(x):
    return x * jax.nn.sigmoid(x)


def _rms(x, g):
    ms = jnp.mean(x * x, axis=-1, keepdims=True)
    return x * lax.rsqrt(ms + RMS_EPS) * g


def _layer_norm(x, g, b):
    mu = jnp.mean(x, axis=-1, keepdims=True)
    xc = x - mu
    var = jnp.mean(xc * xc, axis=-1, keepdims=True)
    return xc * lax.rsqrt(var + LN_EPS) * g + b


def _mod_kernel(c_ref, w_ref, b_ref, o_ref):
    ca = _silu(c_ref[...]).astype(BF16)
    o_ref[...] = jnp.dot(ca, w_ref[...].astype(BF16), preferred_element_type=F32) + b_ref[...]


def _modulation(c, ada_w, ada_b):
    depth, d, six_d = ada_w.shape
    bsz = c.shape[0]
    n = six_d // d
    out = pl.pallas_call(
        _mod_kernel,
        out_shape=jax.ShapeDtypeStruct((depth, bsz, six_d), F32),
        grid=(depth, n),
        in_specs=[
            pl.BlockSpec((bsz, d), lambda l, j: (0, 0)),
            pl.BlockSpec((None, d, d), lambda l, j: (l, 0, j)),
            pl.BlockSpec((None, 1, d), lambda l, j: (l, 0, j)),
        ],
        out_specs=pl.BlockSpec((None, bsz, d), lambda l, j: (l, 0, j)),
        compiler_params=_params(("parallel", "parallel")),
        name="modulation",
    )(c, ada_w, ada_b.reshape(depth, 1, six_d))
    return out.reshape(depth, bsz, n, d)


def _inproj_kernel(x_ref, mod_ref, g_ref, w_ref, o_ref, *, n_chunk):
    h = _rms(x_ref[...], g_ref[...]) * (1.0 + mod_ref[1:2, :]) + mod_ref[0:1, :]
    hb = h.astype(BF16)
    for j in range(0, o_ref.shape[-1], n_chunk):
        o_ref[:, j:j + n_chunk] = jnp.dot(
            hb, w_ref[:, j:j + n_chunk], preferred_element_type=F32).astype(o_ref.dtype)


def _in_projection(x2, mod_l, g, w_bf, seq, tm=512):
    t, d = x2.shape
    n = w_bf.shape[1]
    per_b = seq // tm
    return pl.pallas_call(
        functools.partial(_inproj_kernel, n_chunk=512),
        out_shape=jax.ShapeDtypeStruct((t, n), BF16),
        grid=(t // tm,),
        in_specs=[
            pl.BlockSpec((tm, d), lambda i: (i, 0)),
            pl.BlockSpec((None, 6, d), lambda i: (i // per_b, 0, 0)),
            pl.BlockSpec((1, d), lambda i: (0, 0)),
            pl.BlockSpec((d, n), lambda i: (0, 0)),
        ],
        out_specs=pl.BlockSpec((tm, n), lambda i: (i, 0)),
        compiler_params=_params(("parallel",)),
        name="in_projection",
    )(x2, mod_l, g.reshape(1, d), w_bf)


def _sgu_kernel(u_ref, v_ref, lng_ref, lnb_ref, w_ref, bias_ref, o_ref):
    u = _gelu(u_ref[...].astype(F32))
    v = _gelu(v_ref[...].astype(F32))
    vb = _layer_norm(v, lng_ref[...], lnb_ref[...]).astype(BF16)
    row = lax.broadcasted_iota(jnp.int32, (CHUNK, CHUNK), 0)
    col = lax.broadcasted_iota(jnp.int32, (CHUNK, CHUNK), 1)
    causal = col <= row
    gd = u.shape[1] // SGU_GROUPS
    for g in range(SGU_GROUPS):
        w = jnp.where(causal, w_ref[g], 0.0).astype(BF16)
        cs = slice(g * gd, (g + 1) * gd)
        for c in range(u.shape[0] // CHUNK):
            rs = slice(c * CHUNK, (c + 1) * CHUNK)
            mixed = jnp.dot(w, vb[rs, cs], preferred_element_type=F32) + bias_ref[:, cs]
            o_ref[rs, cs] = (u[rs, cs] * mixed).astype(o_ref.dtype)


def _spatial_gating(proj, ln_g, ln_b, w_s, b_s, width, tm=512):
    t = proj.shape[0]
    gd = width // SGU_GROUPS
    bias = jnp.repeat(b_s.T, gd, axis=1)
    return pl.pallas_call(
        _sgu_kernel,
        out_shape=jax.ShapeDtypeStruct((t, width), BF16),
        grid=(t // tm,),
        in_specs=[
            pl.BlockSpec((tm, width), lambda i: (i, 0)),
            pl.BlockSpec((tm, width), lambda i: (i, 1)),
            pl.BlockSpec((1, width), lambda i: (0, 0)),
            pl.BlockSpec((1, width), lambda i: (0, 0)),
            pl.BlockSpec((SGU_GROUPS, CHUNK, CHUNK), lambda i: (0, 0, 0)),
            pl.BlockSpec((CHUNK, width), lambda i: (0, 0)),
        ],
        out_specs=pl.BlockSpec((tm, width), lambda i: (i, 0)),
        compiler_params=_params(("parallel",)),
        name="spatial_gating",
    )(proj, proj, ln_g.reshape(1, width), ln_b.reshape(1, width), w_s, bias)


ONES_ROWS = 16


def _split3(x):
    hi = x.astype(BF16).astype(F32)
    r1 = x - hi
    mid = r1.astype(BF16).astype(F32)
    lo = (r1 - mid).astype(BF16).astype(F32)
    return hi, mid, lo


def _diff_kernel(slopes_ref, lq1_ref, lk1_ref, lq2_ref, lk2_ref, q_ref, k_ref, v_ref, g_ref,
                 o_ref, kx, vt, acc, *, tq, lam_init):
    h = pl.program_id(1)
    qi = pl.program_id(2)
    seq, vd = v_ref.shape
    hd = HEAD_DIM
    lane = lax.broadcasted_iota(jnp.int32, (tq, LANES), 1)

    def place(pieces, base, fill):
        out = fill
        for n, piece in enumerate(pieces):
            out = jnp.where(lane == base + n, piece, out)
        return out

    @pl.when(qi == 0)
    def _():
        slope = slopes_ref[h]
        for c in range(seq // tq):
            rs = slice(c * tq, (c + 1) * tq)
            k = k_ref[rs, :].astype(F32)
            pos = (lax.broadcasted_iota(jnp.int32, (tq, LANES), 0) + c * tq).astype(F32) * slope
            pieces = _split3(pos)
            zero = jnp.zeros_like(k)
            kx[0, rs, :] = jnp.where(lane < hd, k, place(pieces, hd, zero)).astype(BF16)
            kx[1, rs, :] = jnp.where(lane >= hd, k, place(pieces, 0, zero)).astype(BF16)
            vt[c, 0:vd, :] = v_ref[rs, :].astype(F32).T.astype(BF16)
            vt[c, vd:, :] = jnp.ones((ONES_ROWS, tq), BF16)

    q = q_ref[...].astype(F32) * hd ** -0.5
    zero = jnp.zeros_like(q)
    one = jnp.ones_like(q)
    qx = (jnp.where(lane < hd, q, place((one, one, one), hd, zero)).astype(BF16),
          jnp.where(lane >= hd, q, place((one, one, one), 0, zero)).astype(BF16))
    acc[...] = jnp.zeros_like(acc)
    key_le_query = (lax.broadcasted_iota(jnp.int32, (tq, tq), 0)
                    <= lax.broadcasted_iota(jnp.int32, (tq, tq), 1))

    def tile(j, ms, diag):
        start = pl.multiple_of(j * tq, tq)
        out = []
        for m in range(2):
            s = lax.dot_general(kx[m, pl.ds(start, tq), :], qx[m], (((1,), (1,)), ((), ())),
                                preferred_element_type=F32)
            if diag:
                s = jnp.where(key_le_query, s, NEG)
            m_new = jnp.maximum(ms[m], jnp.max(s, axis=0, keepdims=True))
            alpha = jnp.exp(ms[m] - m_new)
            p = jnp.exp(s - m_new).astype(BF16)
            acc[m] = alpha * acc[m] + jnp.dot(vt[j], p, preferred_element_type=F32)
            out.append(m_new)
        return tuple(out)

    init = (jnp.full((1, tq), NEG, F32), jnp.full((1, tq), NEG, F32))
    ms = lax.fori_loop(0, qi, lambda j, c: tile(j, c, False), init)
    tile(qi, ms, True)

    lam = (jnp.exp(jnp.sum(lq1_ref[...] * lk1_ref[...], axis=-1, keepdims=True))
           - jnp.exp(jnp.sum(lq2_ref[...] * lk2_ref[...], axis=-1, keepdims=True)) + lam_init)
    o_t = (acc[0, 0:vd, :] / acc[0, vd:vd + 1, :]
           - lam * (acc[1, 0:vd, :] / acc[1, vd:vd + 1, :]))
    o_ref[...] = (_rms(o_t.T, g_ref[...]) * (1.0 - lam_init)).astype(o_ref.dtype)


def _diff_attention(proj3, lq1, lk1, lq2, lk2, subln_g, lam_init, n_heads, col0, tq=512):
    bsz, seq, _ = proj3.shape
    slopes = 2.0 ** (-8.0 * jnp.arange(1, n_heads + 1, dtype=F32) / n_heads)
    qb, kb, vb = col0 // LANES, col0 // LANES + n_heads, col0 // LANES + 2 * n_heads
    vec = lambda a: a.reshape(1, -1)
    small = lambda n: pl.BlockSpec((1, n), lambda b, h, i: (0, 0))
    return pl.pallas_call(
        functools.partial(_diff_kernel, tq=tq, lam_init=lam_init),
        out_shape=jax.ShapeDtypeStruct((bsz, seq, n_heads * LANES), BF16),
        grid=(bsz, n_heads, seq // tq),
        in_specs=[
            pl.BlockSpec(memory_space=pltpu.SMEM),
            small(HEAD_DIM), small(HEAD_DIM), small(HEAD_DIM), small(HEAD_DIM),
            pl.BlockSpec((None, tq, LANES), lambda b, h, i: (b, i, qb + h)),
            pl.BlockSpec((None, seq, LANES), lambda b, h, i: (b, 0, kb + h)),
            pl.BlockSpec((None, seq, LANES), lambda b, h, i: (b, 0, vb + h)),
            small(LANES),
        ],
        out_specs=pl.BlockSpec((None, tq, LANES), lambda b, h, i: (b, i, h)),
        scratch_shapes=[
            pltpu.VMEM((2, seq, LANES), BF16),
            pltpu.VMEM((seq // tq, LANES + ONES_ROWS, tq), BF16),
            pltpu.VMEM((2, LANES + ONES_ROWS, tq), F32),
        ],
        compiler_params=_params(("parallel", "parallel", "arbitrary")),
        name="diff_attention",
    )(slopes, vec(lq1), vec(lk1), vec(lq2), vec(lk2), proj3, proj3, proj3, vec(subln_g))


def _dil_kernel(slopes_ref, q1_ref, q2_ref, q3_ref, k1_ref, k2_ref, k3_ref, v_ref, o_ref,
                qf, kf, vf, bias_s, acc_s, m_s, l_s, *, seq):
    hp = pl.program_id(1)
    wb = WIN_BLOCK
    hd = HEAD_DIM
    n_q_blocks = seq // wb

    qf[0] = q2_ref[...].astype(F32)
    qf[1] = q3_ref[...].astype(F32)
    kf[0] = k2_ref[...].astype(F32)
    kf[1] = k3_ref[...].astype(F32)
    vf[...] = v_ref[...].astype(F32)

    row = lax.broadcasted_iota(jnp.int32, (2 * wb, 2 * wb), 0)
    col = lax.broadcasted_iota(jnp.int32, (2 * wb, 2 * wb), 1)
    qi_ = jnp.where(row < wb, row, row - wb)
    dist = qi_ + wb - col
    slope = jnp.where(row < wb, slopes_ref[2 * hp], slopes_ref[2 * hp + 1])
    valid = jnp.logical_and(dist >= 0, dist <= wb)
    table = jnp.where(valid, -slope * dist.astype(F32), NEG)
    bias_s[1] = table
    bias_s[0] = jnp.where(col < wb, NEG, table)

    first = lax.broadcasted_iota(jnp.int32, (wb, LANES), 1) < hd
    q_scale = hd ** -0.5

    for g, (window, dil) in enumerate(DIL_PAIRS):
        assert window // dil == wb
        nb = n_q_blocks // dil
        refs = {"q": (q1_ref, qf), "k": (k1_ref, kf), "v": (v_ref, vf)}

        def rows(kind, start, g=g, dil=dil, refs=refs):
            direct, copies = refs[kind]
            if dil == 1:
                return direct[pl.ds(pl.multiple_of(start, wb), wb), :]
            src = copies if kind == "v" else copies.at[g - 1]
            return src[pl.ds(start, wb, stride=dil), :]

        def body(idx, _, g=g, dil=dil, nb=nb, rows=rows):
            r = idx // nb
            n = idx % nb
            cur = n * (wb * dil) + r
            q = rows("q", cur).astype(F32) * q_scale
            zero = jnp.zeros_like(q)
            qx = jnp.concatenate([jnp.where(first, q, zero), jnp.where(first, zero, q)],
                                 axis=0).astype(BF16)
            if nb > 1:
                prev = jnp.maximum(n - 1, 0) * (wb * dil) + r
                keys = jnp.concatenate([rows("k", prev), rows("k", cur)], axis=0).astype(BF16)
                vals = jnp.concatenate([rows("v", prev), rows("v", cur)], axis=0).astype(BF16)
                bias = bias_s[jnp.minimum(n, 1)]
            else:
                keys, vals = rows("k", cur).astype(BF16), rows("v", cur).astype(BF16)
                bias = bias_s[1, :, wb:]
            vx = jnp.concatenate([vals, jnp.ones_like(vals)], axis=1)
            s = lax.dot_general(qx, keys, (((1,), (1,)), ((), ())),
                                preferred_element_type=F32) + bias
            m = jnp.max(s, axis=-1, keepdims=True)
            p = jnp.exp(s - m).astype(BF16)
            o = jnp.dot(p, vx, preferred_element_type=F32)
            acc = jnp.where(first, o[0:wb, 0:LANES], o[wb:, 0:LANES])
            den = jnp.where(first, o[0:wb, LANES:], o[wb:, LANES:])
            mx = jnp.where(first, jnp.broadcast_to(m[0:wb], (wb, LANES)),
                           jnp.broadcast_to(m[wb:], (wb, LANES)))
            if dil == 1:
                dst = pl.ds(pl.multiple_of(cur, wb), wb)
            else:
                dst = pl.ds(cur, wb, stride=dil)
            acc_s[g, dst, :] = acc
            m_s[g, dst, :] = mx
            l_s[g, dst, :] = den
            return 0

        lax.fori_loop(0, n_q_blocks, body, 0, unroll=8)

    mr = 256
    for c in range(seq // mr):
        rs = slice(c * mr, (c + 1) * mr)
        m1, m2, m3 = m_s[0, rs, :], m_s[1, rs, :], m_s[2, rs, :]
        m = jnp.maximum(jnp.maximum(m1, m2), m3)
        w1, w2, w3 = jnp.exp(m1 - m), jnp.exp(m2 - m), jnp.exp(m3 - m)
        num = w1 * acc_s[0, rs, :] + w2 * acc_s[1, rs, :] + w3 * acc_s[2, rs, :]
        den = w1 * l_s[0, rs, :] + w2 * l_s[1, rs, :] + w3 * l_s[2, rs, :]
        o_ref[rs, :] = (num / den).astype(o_ref.dtype)


def _dilated_attention(proj3, n_heads):
    bsz, seq, _ = proj3.shape
    n_pairs = n_heads * HEAD_DIM // LANES
    slopes = 2.0 ** (-8.0 * jnp.arange(1, n_heads + 1, dtype=F32) / n_heads)
    groups = len(DIL_PAIRS)

    def spec(blk0):
        return pl.BlockSpec((None, seq, LANES), lambda b, p: (b, 0, blk0 + p))

    q_specs = [spec(g * n_pairs) for g in range(groups)]
    k_specs = [spec((groups + g) * n_pairs) for g in range(groups)]
    v_spec = spec(2 * groups * n_pairs)
    return pl.pallas_call(
        functools.partial(_dil_kernel, seq=seq),
        out_shape=jax.ShapeDtypeStruct((bsz, seq, n_pairs * LANES), BF16),
        grid=(bsz, n_pairs),
        in_specs=[pl.BlockSpec(memory_space=pltpu.SMEM)] + q_specs + k_specs + [v_spec],
        out_specs=pl.BlockSpec((None, seq, LANES), lambda b, p: (b, 0, p)),
        scratch_shapes=[
            pltpu.VMEM((groups - 1, seq, LANES), F32),
            pltpu.VMEM((groups - 1, seq, LANES), F32),
            pltpu.VMEM((seq, LANES), F32),
            pltpu.VMEM((2, 2 * WIN_BLOCK, 2 * WIN_BLOCK), F32),
            pltpu.VMEM((groups, seq, LANES), F32),
            pltpu.VMEM((groups, seq, LANES), F32),
            pltpu.VMEM((groups, seq, LANES), F32),
        ],
        compiler_params=_params(("parallel", "parallel")),
        name="dilated_attention",
    )(slopes, *([proj3] * 7))


def _conv_kernel(a_ref, gate_ref, w_ref, cb_ref, lng_ref, lnb_ref, o_ref, buf, *, ts, halo, rc):
    @pl.when(pl.program_id(1) == 0)
    def _():
        buf[0:halo, :] = jnp.zeros((halo, buf.shape[1]), F32)

    buf[halo:halo + ts, :] = a_ref[...].astype(F32) * jax.nn.sigmoid(gate_ref[...].astype(F32))
    off = halo - (CONV_K - 1)
    for c in range(ts // rc):
        acc = jnp.zeros((rc, buf.shape[1]), F32)
        for k in range(CONV_K):
            acc = acc + w_ref[k:k + 1, :] * buf[c * rc + off + k:c * rc + off + k + rc, :]
        y = _layer_norm(acc + cb_ref[...], lng_ref[...], lnb_ref[...])
        o_ref[c * rc:(c + 1) * rc, :] = _silu(y).astype(o_ref.dtype)
    buf[0:halo, :] = buf[ts:ts + halo, :]


def _conv_module(proj3, conv_w, conv_b, ln_g, ln_b, col0, ts=256):
    bsz, seq, _ = proj3.shape
    width = conv_w.shape[1]
    halo = 32
    vec = lambda a: a.reshape(1, width)
    small = pl.BlockSpec((1, width), lambda b, s: (0, 0))
    return pl.pallas_call(
        functools.partial(_conv_kernel, ts=ts, halo=halo, rc=64),
        out_shape=jax.ShapeDtypeStruct((bsz, seq, width), BF16),
        grid=(bsz, seq // ts),
        in_specs=[
            pl.BlockSpec((None, ts, width), lambda b, s: (b, s, col0 // width)),
            pl.BlockSpec((None, ts, width), lambda b, s: (b, s, col0 // width + 1)),
            pl.BlockSpec((CONV_K, width), lambda b, s: (0, 0)),
            small, small, small,
        ],
        out_specs=pl.BlockSpec((None, ts, width), lambda b, s: (b, s, 0)),
        scratch_shapes=[pltpu.VMEM((ts + halo, width), F32)],
        compiler_params=_params(("parallel", "arbitrary")),
        name="conv_module",
    )(proj3, proj3, conv_w, vec(conv_b), vec(ln_g), vec(ln_b))


SORT_TILE = 256
PAGE = 16
TOP_K = 2
SLOTS = TOP_K * SORT_TILE + N_EXPERTS * PAGE
PAGES_PER_TILE = SLOTS // PAGE
STEP_PAGES = 16
META_ROWS = 8


def _route(logits_t, rb):
    mx = jnp.max(logits_t, axis=0, keepdims=True)
    ex = jnp.exp(logits_t - mx)
    probs = ex / jnp.sum(ex, axis=0, keepdims=True)
    sel = probs + rb
    srow = [sel[i:i + 1, :] for i in range(N_EXPERTS)]
    prow = [probs[i:i + 1, :] for i in range(N_EXPERTS)]
    epg = EXPERTS_PER_GROUP
    n_grp = N_EXPERTS // epg

    scores = []
    for g in range(n_grp):
        a, b, c, d = srow[g * epg:(g + 1) * epg]
        hi1, lo1 = jnp.maximum(a, b), jnp.minimum(a, b)
        hi2, lo2 = jnp.maximum(c, d), jnp.minimum(c, d)
        scores.append(jnp.maximum(hi1, hi2)
                      + jnp.maximum(jnp.minimum(hi1, hi2), jnp.maximum(lo1, lo2)))
    best = scores[0]
    grp = jnp.zeros(best.shape, jnp.int32)
    for g in range(1, n_grp):
        better = scores[g] > best
        grp = jnp.where(better, g, grp)
        best = jnp.where(better, scores[g], best)

    def pick(rows_, j):
        out = rows_[j]
        for g in range(1, n_grp):
            out = jnp.where(grp == g, rows_[g * epg + j], out)
        return out

    ing = [pick(srow, j) for j in range(epg)]
    ping = [pick(prow, j) for j in range(epg)]
    b0, i0, p0 = ing[0], jnp.zeros(best.shape, jnp.int32), ping[0]
    for j in range(1, epg):
        better = ing[j] > b0
        i0 = jnp.where(better, j, i0)
        p0 = jnp.where(better, ping[j], p0)
        b0 = jnp.where(better, ing[j], b0)
    b1 = jnp.full(best.shape, -jnp.inf, F32)
    i1 = jnp.zeros(best.shape, jnp.int32)
    p1 = jnp.zeros(best.shape, F32)
    for j in range(epg):
        better = jnp.logical_and(i0 != j, ing[j] > b1)
        i1 = jnp.where(better, j, i1)
        p1 = jnp.where(better, ping[j], p1)
        b1 = jnp.where(better, ing[j], b1)
    den = p0 + p1
    return grp * epg + i0, grp * epg + i1, p0 / den, p1 / den


def _sort_tile(e0, e1, hb):
    st = hb.shape[0]
    eid = lax.broadcasted_iota(jnp.int32, (N_EXPERTS, st), 0)
    sel0 = jnp.where(eid == e0, 1.0, 0.0)
    sel1 = jnp.where(eid == e1, 1.0, 0.0)
    sel = sel0 + sel1
    before = (lax.broadcasted_iota(jnp.int32, (st, st), 0)
              < lax.broadcasted_iota(jnp.int32, (st, st), 1))
    rank = jnp.dot(sel.astype(BF16), jnp.where(before, 1.0, 0.0).astype(BF16),
                   preferred_element_type=F32)
    cnt = jnp.sum(sel, axis=1, keepdims=True)
    padded = jnp.ceil(cnt * (1.0 / PAGE)) * PAGE
    ecol = lax.broadcasted_iota(jnp.int32, (N_EXPERTS, 1), 0)
    seg = jnp.zeros((N_EXPERTS, 1), F32)
    run = jnp.zeros((1, 1), F32)
    for e in range(N_EXPERTS):
        seg = jnp.where(ecol == e, run, seg)
        run = run + padded[e:e + 1, :]
    base = seg + rank
    pos0 = jnp.sum(sel0 * base, axis=0, keepdims=True)
    pos1 = jnp.sum(sel1 * base, axis=0, keepdims=True)
    slot = lax.broadcasted_iota(jnp.int32, (SLOTS, st), 0)
    perm = (jnp.where(slot == pos0.astype(jnp.int32), 1.0, 0.0)
            + jnp.where(slot == pos1.astype(jnp.int32), 1.0, 0.0)).astype(BF16)
    rows = jnp.dot(perm, hb, preferred_element_type=F32).astype(BF16)
    return pos0, pos1, cnt, rows


def _outproj_kernel(a_ref, b_ref, x_ref, mod_ref, g_ref, w_ref, rwh_ref, rwl_ref, rb_ref,
                    xo_ref, xs_ref, meta_ref, cnt_ref):
    ka = a_ref.shape[1]
    y = (jnp.dot(a_ref[...], w_ref[0:ka, :], preferred_element_type=F32)
         + jnp.dot(b_ref[...], w_ref[ka:, :], preferred_element_type=F32))
    xn = x_ref[...] + mod_ref[2:3, :] * y
    xo_ref[...] = xn
    h = _rms(xn, g_ref[...]) * (1.0 + mod_ref[4:5, :]) + mod_ref[3:4, :]
    hb = h.astype(BF16)
    hl = (h - hb.astype(F32)).astype(BF16)
    logits = (jnp.dot(hb, rwh_ref[...], preferred_element_type=F32)
              + jnp.dot(hb, rwl_ref[...], preferred_element_type=F32)
              + jnp.dot(hl, rwh_ref[...], preferred_element_type=F32))
    lt = logits.T
    for c in range(hb.shape[0] // SORT_TILE):
        ts = slice(c * SORT_TILE, (c + 1) * SORT_TILE)
        e0, e1, g0, g1 = _route(lt[0:N_EXPERTS, ts], rb_ref[...])
        pos0, pos1, cnt, rows = _sort_tile(e0, e1, hb[ts, :])
        xs_ref[c * SLOTS:(c + 1) * SLOTS, :] = rows
        meta_ref[c] = jnp.concatenate(
            [pos0, pos1, g0, g1, jnp.zeros((META_ROWS - 4, SORT_TILE), F32)], axis=0)
        cnt_ref[c] = jnp.broadcast_to(cnt, (N_EXPERTS, LANES))


def _out_projection(a, b, x2, mod_l, g, w_bf, rw_hi, rw_lo, rb, seq, tm=512):
    t, d = x2.shape
    ka, kb = a.shape[1], b.shape[1]
    per_b = seq // tm
    sub = tm // SORT_TILE
    n_tiles = t // SORT_TILE
    return pl.pallas_call(
        _outproj_kernel,
        out_shape=(jax.ShapeDtypeStruct((t, d), F32),
                   jax.ShapeDtypeStruct((n_tiles * SLOTS, d), BF16),
                   jax.ShapeDtypeStruct((n_tiles, META_ROWS, SORT_TILE), F32),
                   jax.ShapeDtypeStruct((n_tiles, N_EXPERTS, LANES), F32)),
        grid=(t // tm,),
        in_specs=[
            pl.BlockSpec((tm, ka), lambda i: (i, 0)),
            pl.BlockSpec((tm, kb), lambda i: (i, 0)),
            pl.BlockSpec((tm, d), lambda i: (i, 0)),
            pl.BlockSpec((None, 6, d), lambda i: (i // per_b, 0, 0)),
            pl.BlockSpec((1, d), lambda i: (0, 0)),
            pl.BlockSpec((ka + kb, d), lambda i: (0, 0)),
            pl.BlockSpec((d, LANES), lambda i: (0, 0)),
            pl.BlockSpec((d, LANES), lambda i: (0, 0)),
            pl.BlockSpec((N_EXPERTS, 1), lambda i: (0, 0)),
        ],
        out_specs=(pl.BlockSpec((tm, d), lambda i: (i, 0)),
                   pl.BlockSpec((sub * SLOTS, d), lambda i: (i, 0)),
                   pl.BlockSpec((sub, META_ROWS, SORT_TILE), lambda i: (i, 0, 0)),
                   pl.BlockSpec((sub, N_EXPERTS, LANES), lambda i: (i, 0, 0))),
        compiler_params=_params(("parallel",)),
        name="out_projection",
    )(a, b, x2, mod_l, g.reshape(1, d), w_bf, rw_hi, rw_lo, rb.reshape(N_EXPERTS, 1))


def _expert_plan(counts, n_steps_max):
    n_tiles = counts.shape[0]
    npg = (counts + PAGE - 1) // PAGE
    seg = jnp.cumsum(npg, axis=1) - npg
    page_base = jnp.arange(n_tiles, dtype=jnp.int32)[:, None] * PAGES_PER_TILE + seg
    cum_t = jnp.cumsum(npg, axis=0)
    tot = cum_t[-1]
    steps_e = (tot + STEP_PAGES - 1) // STEP_PAGES
    step_end = jnp.cumsum(steps_e)
    step_start = step_end - steps_e
    n_live = step_end[-1]
    g = jnp.arange(n_steps_max, dtype=jnp.int32)
    ex = jnp.minimum(jnp.sum((step_end[None, :] <= g[:, None]).astype(jnp.int32), axis=1),
                     N_EXPERTS - 1)
    q = (g - step_start[ex])[:, None] * STEP_PAGES + jnp.arange(STEP_PAGES, dtype=jnp.int32)[None, :]
    valid = jnp.logical_and(q < tot[ex][:, None], (g < n_live)[:, None])
    cum_e = cum_t.T[ex]
    ti = jnp.minimum(jnp.sum((cum_e[:, None, :] <= q[:, :, None]).astype(jnp.int32), axis=-1),
                     n_tiles - 1)
    excl = cum_e - npg.T[ex]
    pid = (jnp.take_along_axis(page_base.T[ex], ti, axis=1)
           + q - jnp.take_along_axis(excl, ti, axis=1))
    spare = (n_tiles * PAGES_PER_TILE + (g % 2)[:, None] * STEP_PAGES
             + jnp.arange(STEP_PAGES, dtype=jnp.int32)[None, :])
    in_pages = jnp.where(valid, pid, 0).astype(jnp.int32).reshape(-1)
    out_pages = jnp.where(valid, pid, spare).astype(jnp.int32).reshape(-1)
    return in_pages, out_pages, ex.astype(jnp.int32), n_live.astype(jnp.int32).reshape(1)


def _expert_kernel(inp_ref, outp_ref, ex_ref, live_ref, *refs):
    x_refs = refs[:STEP_PAGES]
    wgu_ref, wd_ref, _, ys_ref, wgu_bf, wd_bf, ybuf, sems = refs[STEP_PAGES:]
    g = pl.program_id(0)
    live = live_ref[0]
    slot = g % 2
    ff = wd_ref.shape[0]

    def page_copies(step):
        return [pltpu.make_async_copy(ybuf.at[slot, pl.ds(k * PAGE, PAGE), :],
                                      ys_ref.at[outp_ref[step * STEP_PAGES + k]],
                                      sems.at[slot])
                for k in range(STEP_PAGES)]

    @pl.when(jnp.logical_and(g >= 2, g - 2 < live))
    def _():
        for cp in page_copies(g - 2):
            cp.wait()

    new_expert = jnp.logical_or(g == 0, ex_ref[g] != ex_ref[jnp.maximum(g - 1, 0)])

    @pl.when(jnp.logical_and(new_expert, g < live))
    def _():
        wgu_bf[...] = wgu_ref[...].astype(BF16)
        wd_bf[...] = wd_ref[...].astype(BF16)

    @pl.when(g < live)
    def _():
        x = jnp.concatenate([r[...] for r in x_refs], axis=0)
        gu = jnp.dot(x, wgu_bf[...], preferred_element_type=F32)
        act = (_silu(gu[:, :ff]) * gu[:, ff:]).astype(BF16)
        ybuf[slot] = jnp.dot(act, wd_bf[...], preferred_element_type=F32).astype(BF16)
        for cp in page_copies(g):
            cp.start()


def _experts(xs_pages, plan, w_gu, w_down, layer, n_steps):
    n_pages, _, d = xs_pages.shape
    two_ff = w_gu.shape[-1]
    ff = two_ff // 2
    in_pages, out_pages, ex, live = plan
    page_in = [pl.BlockSpec((None, PAGE, d),
                            lambda g, ip, op, e, nl, k=k: (ip[g * STEP_PAGES + k], 0, 0))
               for k in range(STEP_PAGES)]
    grid_spec = pltpu.PrefetchScalarGridSpec(
        num_scalar_prefetch=4,
        grid=(n_steps,),
        in_specs=page_in + [
            pl.BlockSpec((None, None, d, two_ff), lambda g, ip, op, e, nl: (layer, e[g], 0, 0)),
            pl.BlockSpec((None, None, ff, d), lambda g, ip, op, e, nl: (layer, e[g], 0, 0)),
            pl.BlockSpec(memory_space=pl.ANY),
        ],
        out_specs=pl.BlockSpec(memory_space=pl.ANY),
        scratch_shapes=[pltpu.VMEM((d, two_ff), BF16), pltpu.VMEM((ff, d), BF16),
                        pltpu.VMEM((2, STEP_PAGES * PAGE, d), BF16),
                        pltpu.SemaphoreType.DMA((2,))],
    )
    ys0 = jnp.zeros((n_pages + 2 * STEP_PAGES, PAGE, d), BF16)
    return pl.pallas_call(
        _expert_kernel,
        out_shape=jax.ShapeDtypeStruct(ys0.shape, BF16),
        grid_spec=grid_spec,
        input_output_aliases={4 + STEP_PAGES + 2: 0},
        compiler_params=_params(("arbitrary",)),
        name="experts",
    )(in_pages, out_pages, ex, live, *([xs_pages] * STEP_PAGES), w_gu, w_down, ys0)


def _combine_kernel(ys_ref, meta_ref, x_ref, mod_ref, fg_ref, o_ref, *, final):
    st = x_ref.shape[0]
    pos0 = meta_ref[0:1, :].astype(jnp.int32)
    pos1 = meta_ref[1:2, :].astype(jnp.int32)
    slot = lax.broadcasted_iota(jnp.int32, (SLOTS, st), 0)
    w = (jnp.where(slot == pos0, meta_ref[2:3, :], 0.0)
         + jnp.where(slot == pos1, meta_ref[3:4, :], 0.0)).astype(BF16)
    y = lax.dot_general(w, ys_ref[...], (((0,), (0,)), ((), ())), preferred_element_type=F32)
    xn = x_ref[...] + mod_ref[5:6, :] * y
    if final:
        xn = _rms(xn, fg_ref[...])
    o_ref[...] = xn


def _combine(ys_rows, meta, x2, mod_l, final_g, seq, final):
    t, d = x2.shape
    per_b = seq // SORT_TILE
    return pl.pallas_call(
        functools.partial(_combine_kernel, final=final),
        out_shape=jax.ShapeDtypeStruct((t, d), F32),
        grid=(t // SORT_TILE,),
        in_specs=[
            pl.BlockSpec((SLOTS, d), lambda i: (i, 0)),
            pl.BlockSpec((None, META_ROWS, SORT_TILE), lambda i: (i, 0, 0)),
            pl.BlockSpec((SORT_TILE, d), lambda i: (i, 0)),
            pl.BlockSpec((None, 6, d), lambda i: (i // per_b, 0, 0)),
            pl.BlockSpec((1, d), lambda i: (0, 0)),
        ],
        out_specs=pl.BlockSpec((SORT_TILE, d), lambda i: (i, 0)),
        compiler_params=_params(("parallel",)),
        name="moe_combine",
    )(ys_rows, meta, x2, mod_l, final_g.reshape(1, d))


def _moe(xs, meta, counts, w_gu, w_down, layer, x2, mod_l, final_g, seq, final):
    d = x2.shape[1]
    n_tiles = counts.shape[0]
    n_pages = n_tiles * PAGES_PER_TILE
    n_steps = n_pages // STEP_PAGES + N_EXPERTS + 2
    plan = _expert_plan(counts[:, :, 0].astype(jnp.int32), n_steps)
    ys = _experts(xs.reshape(n_pages, PAGE, d), plan, w_gu, w_down, layer, n_steps)
    return _combine(ys.reshape(-1, d), meta, x2, mod_l, final_g, seq, final)


def kernel(x, c, norm1_g, norm2_g, ada_w, ada_b, ev_w_in, ev_w_out, sgu_ln_g, sgu_ln_b, sgu_w, sgu_b, lam_q1, lam_k1, lam_q2, lam_k2, diff_subln_g, od_w_in, od_w_out, conv_w, conv_b, conv_ln_g, conv_ln_b, router_w, router_b, moe_w_gu, moe_w_down, final_g):
    bsz, seq, d = x.shape
    depth = ada_w.shape[0]
    half = d // 2
    diff_heads = half // (2 * HEAD_DIM)
    dil_heads = half // HEAD_DIM
    t = bsz * seq

    mod = _modulation(c, ada_w, ada_b)
    rw = jnp.pad(router_w, ((0, 0), (0, LANES - N_EXPERTS)))
    rw_hi = rw.astype(BF16)
    rw_lo = (rw - rw_hi.astype(F32)).astype(BF16)

    x2 = x.reshape(t, d)
    for l in range(depth):
        i = l // 2
        if l % 2 == 0:
            lam_init = 0.8 - 0.6 * math.exp(-0.3 * l)
            proj = _in_projection(x2, mod[l], norm1_g[l], ev_w_in[i].astype(BF16), seq)
            part_a = _spatial_gating(proj, sgu_ln_g[i], sgu_ln_b[i], sgu_w[i], sgu_b[i], half)
            part_b = _diff_attention(proj.reshape(bsz, seq, -1), lam_q1[i], lam_k1[i], lam_q2[i],
                                     lam_k2[i], diff_subln_g[i], lam_init, diff_heads, 2 * half)
            w_out = ev_w_out[i]
        else:
            proj = _in_projection(x2, mod[l], norm1_g[l], od_w_in[i].astype(BF16), seq)
            proj3 = proj.reshape(bsz, seq, -1)
            part_a = _dilated_attention(proj3, dil_heads).reshape(t, half)
            part_b = _conv_module(proj3, conv_w[i], conv_b[i], conv_ln_g[i], conv_ln_b[i],
                                  (2 * len(DIL_PAIRS) + 1) * half)
            w_out = od_w_out[i]
        x2, xs, meta, counts = _out_projection(part_a, part_b.reshape(t, half), x2, mod[l],
                                               norm2_g[l], w_out.astype(BF16), rw_hi, rw_lo,
                                               router_b, seq)
        x2 = _moe(xs, meta, counts, moe_w_gu, moe_w_down, l, x2, mod[l], final_g, seq,
                  final=(l == depth - 1))
    return x2.reshape(bsz, seq, d)
```

```python
import functools
import math

import jax
import jax.numpy as jnp
from jax import lax
from jax.experimental import pallas as pl
from jax.experimental.pallas import tpu as pltpu

F32 = jnp.float32
BF16 = jnp.bfloat16

HEAD_DIM = 64
LANES = 128
CHUNK = 128
SGU_GROUPS = 4
DIL_PAIRS = ((128, 1), (512, 4), (2048, 16))
WIN_BLOCK = 128
CONV_K = 31
CONV_SUBLANES = 8
N_EXPERTS = 16
EXPERTS_PER_GROUP = 4
RMS_EPS = 1e-6
LN_EPS = 1e-5
NEG = -1e30
LOG2E = 1.4426950408889634
VMEM_LIMIT = 56 * 1024 * 1024


def _params(sem):
    return pltpu.CompilerParams(dimension_semantics=sem, vmem_limit_bytes=VMEM_LIMIT)


def _gelu(x):
    return x * (0.5 * (1.0 + jnp.tanh(0.7978845608028654 * (x + 0.044715 * (x * x * x)))))


def _silu(x):
    return x * jax.nn.sigmoid(x)


def _rms(x, g):
    ms = jnp.mean(x * x, axis=-1, keepdims=True)
    return x * lax.rsqrt(ms + RMS_EPS) * g


def _layer_norm(x, g, b):
    mu = jnp.mean(x, axis=-1, keepdims=True)
    xc = x - mu
    var = jnp.mean(xc * xc, axis=-1, keepdims=True)
    return xc * lax.rsqrt(var + LN_EPS) * g + b


def _mod_kernel(c_ref, w_ref, b_ref, o_ref):
    ca = _silu(c_ref[...]).astype(BF16)
    o_ref[...] = jnp.dot(ca, w_ref[...].astype(BF16), preferred_element_type=F32) + b_ref[...]


def _modulation(c, ada_w, ada_b):
    depth, d, six_d = ada_w.shape
    bsz = c.shape[0]
    n = six_d // d
    out = pl.pallas_call(
        _mod_kernel,
        out_shape=jax.ShapeDtypeStruct((depth, bsz, six_d), F32),
        grid=(depth, n),
        in_specs=[
            pl.BlockSpec((bsz, d), lambda l, j: (0, 0)),
            pl.BlockSpec((None, d, d), lambda l, j: (l, 0, j)),
            pl.BlockSpec((None, 1, d), lambda l, j: (l, 0, j)),
        ],
        out_specs=pl.BlockSpec((None, bsz, d), lambda l, j: (l, 0, j)),
        compiler_params=_params(("parallel", "parallel")),
        name="modulation",
    )(c, ada_w, ada_b.reshape(depth, 1, six_d))
    return out.reshape(depth, bsz, n, d)


def _inproj_kernel(x_ref, mod_ref, g_ref, w_ref, o_ref, *, n_chunk):
    h = _rms(x_ref[...], g_ref[...]) * (1.0 + mod_ref[1:2, :]) + mod_ref[0:1, :]
    hb = h.astype(BF16)
    for j in range(0, o_ref.shape[-1], n_chunk):
        o_ref[:, j:j + n_chunk] = jnp.dot(
            hb, w_ref[:, j:j + n_chunk], preferred_element_type=F32).astype(o_ref.dtype)


def _in_projection(x2, mod_l, g, w_bf, seq, tm=512):
    t, d = x2.shape
    n = w_bf.shape[1]
    per_b = seq // tm
    return pl.pallas_call(
        functools.partial(_inproj_kernel, n_chunk=512),
        out_shape=jax.ShapeDtypeStruct((t, n), BF16),
        grid=(t // tm,),
        in_specs=[
            pl.BlockSpec((tm, d), lambda i: (i, 0)),
            pl.BlockSpec((None, 6, d), lambda i: (i // per_b, 0, 0)),
            pl.BlockSpec((1, d), lambda i: (0, 0)),
            pl.BlockSpec((d, n), lambda i: (0, 0)),
        ],
        out_specs=pl.BlockSpec((tm, n), lambda i: (i, 0)),
        compiler_params=_params(("parallel",)),
        name="in_projection",
    )(x2, mod_l, g.reshape(1, d), w_bf)


def _sgu_kernel(u_ref, v_ref, lng_ref, lnb_ref, w_ref, bias_ref, o_ref):
    u = _gelu(u_ref[...].astype(F32))
    v = _gelu(v_ref[...].astype(F32))
    vb = _layer_norm(v, lng_ref[...], lnb_ref[...]).astype(BF16)
    row = lax.broadcasted_iota(jnp.int32, (CHUNK, CHUNK), 0)
    col = lax.broadcasted_iota(jnp.int32, (CHUNK, CHUNK), 1)
    causal = col <= row
    gd = u.shape[1] // SGU_GROUPS
    for g in range(SGU_GROUPS):
        w = jnp.where(causal, w_ref[g], 0.0).astype(BF16)
        cs = slice(g * gd, (g + 1) * gd)
        for c in range(u.shape[0] // CHUNK):
            rs = slice(c * CHUNK, (c + 1) * CHUNK)
            mixed = jnp.dot(w, vb[rs, cs], preferred_element_type=F32) + bias_ref[:, cs]
            o_ref[rs, cs] = (u[rs, cs] * mixed).astype(o_ref.dtype)


def _spatial_gating(proj, ln_g, ln_b, w_s, b_s, width, tm=512):
    t = proj.shape[0]
    gd = width // SGU_GROUPS
    bias = jnp.repeat(b_s.T, gd, axis=1)
    return pl.pallas_call(
        _sgu_kernel,
        out_shape=jax.ShapeDtypeStruct((t, width), BF16),
        grid=(t // tm,),
        in_specs=[
            pl.BlockSpec((tm, width), lambda i: (i, 0)),
            pl.BlockSpec((tm, width), lambda i: (i, 1)),
            pl.BlockSpec((1, width), lambda i: (0, 0)),
            pl.BlockSpec((1, width), lambda i: (0, 0)),
            pl.BlockSpec((SGU_GROUPS, CHUNK, CHUNK), lambda i: (0, 0, 0)),
            pl.BlockSpec((CHUNK, width), lambda i: (0, 0)),
        ],
        out_specs=pl.BlockSpec((tm, width), lambda i: (i, 0)),
        compiler_params=_params(("parallel",)),
        name="spatial_gating",
    )(proj, proj, ln_g.reshape(1, width), ln_b.reshape(1, width), w_s, bias)


ONES_ROWS = 16


def _split3(x):
    hi = x.astype(BF16).astype(F32)
    r1 = x - hi
    mid = r1.astype(BF16).astype(F32)
    lo = (r1 - mid).astype(BF16).astype(F32)
    return hi, mid, lo


def _diff_kernel(slopes_ref, lq1_ref, lk1_ref, lq2_ref, lk2_ref, q_ref, k_ref, v_ref, g_ref,
                 o_ref, kx, qx, vt, acc, *, tq, lam_init):
    h = pl.program_id(1)
    seq, vd = v_ref.shape
    n_q = seq // tq
    hd = HEAD_DIM
    lane = lax.broadcasted_iota(jnp.int32, (tq, LANES), 1)

    def place(pieces, base, fill):
        out = fill
        for n, piece in enumerate(pieces):
            out = jnp.where(lane == base + n, piece, out)
        return out

    slope = slopes_ref[h] * LOG2E
    one = jnp.ones((tq, LANES), F32)
    for c in range(n_q):
        rs = slice(c * tq, (c + 1) * tq)
        k = k_ref[rs, :].astype(F32)
        pos = (lax.broadcasted_iota(jnp.int32, (tq, LANES), 0) + c * tq).astype(F32) * slope
        pieces = _split3(pos)
        zero = jnp.zeros_like(k)
        kx[0, rs, :] = jnp.where(lane < hd, k, place(pieces, hd, zero)).astype(BF16)
        kx[1, rs, :] = jnp.where(lane >= hd, k, place(pieces, 0, zero)).astype(BF16)
        vt[c, 0:vd, :] = v_ref[rs, :].astype(F32).T.astype(BF16)
        vt[c, vd:, :] = jnp.ones((ONES_ROWS, tq), BF16)
        q = q_ref[rs, :].astype(F32) * (hd ** -0.5 * LOG2E)
        qx[0, rs, :] = jnp.where(lane < hd, q, place((one, one, one), hd, zero)).astype(BF16)
        qx[1, rs, :] = jnp.where(lane >= hd, q, place((one, one, one), 0, zero)).astype(BF16)

    key_le_query = (lax.broadcasted_iota(jnp.int32, (tq, tq), 0)
                    <= lax.broadcasted_iota(jnp.int32, (tq, tq), 1))
    ms = {}
    for j in range(n_q):
        for qi in range(j, n_q):
            for m in range(2):
                s = lax.dot_general(kx[m, j * tq:(j + 1) * tq, :], qx[m, qi * tq:(qi + 1) * tq, :],
                                    (((1,), (1,)), ((), ())),
                                    preferred_element_type=F32)
                if j == qi:
                    s = jnp.where(key_le_query, s, NEG)
                m_tile = jnp.max(s, axis=0, keepdims=True)
                if j == 0:
                    ms[qi, m] = m_tile
                    p = jnp.exp2(s - m_tile).astype(BF16)
                    acc[qi, m] = jnp.dot(vt[j], p, preferred_element_type=F32)
                else:
                    m_new = jnp.maximum(ms[qi, m], m_tile)
                    alpha = jnp.exp2(ms[qi, m] - m_new)
                    ms[qi, m] = m_new
                    p = jnp.exp2(s - m_new).astype(BF16)
                    acc[qi, m] = alpha * acc[qi, m] + jnp.dot(vt[j], p, preferred_element_type=F32)

    lam = (jnp.exp(jnp.sum(lq1_ref[...] * lk1_ref[...], axis=-1, keepdims=True))
           - jnp.exp(jnp.sum(lq2_ref[...] * lk2_ref[...], axis=-1, keepdims=True)) + lam_init)
    for qi in range(n_q):
        o_t = (acc[qi, 0, 0:vd, :] / acc[qi, 0, vd:vd + 1, :]
               - lam * (acc[qi, 1, 0:vd, :] / acc[qi, 1, vd:vd + 1, :]))
        o_ref[qi * tq:(qi + 1) * tq, :] = (
            _rms(o_t.T, g_ref[...]) * (1.0 - lam_init)).astype(o_ref.dtype)


def _diff_attention(proj3, lq1, lk1, lq2, lk2, subln_g, lam_init, n_heads, col0, tq=512):
    bsz, seq, _ = proj3.shape
    slopes = 2.0 ** (-8.0 * jnp.arange(1, n_heads + 1, dtype=F32) / n_heads)
    qb, kb, vb = col0 // LANES, col0 // LANES + n_heads, col0 // LANES + 2 * n_heads
    vec = lambda a: a.reshape(1, -1)
    small = lambda n: pl.BlockSpec((1, n), lambda b, h: (0, 0))
    return pl.pallas_call(
        functools.partial(_diff_kernel, tq=tq, lam_init=lam_init),
        out_shape=jax.ShapeDtypeStruct((bsz, seq, n_heads * LANES), BF16),
        grid=(bsz, n_heads),
        in_specs=[
            pl.BlockSpec(memory_space=pltpu.SMEM),
            small(HEAD_DIM), small(HEAD_DIM), small(HEAD_DIM), small(HEAD_DIM),
            pl.BlockSpec((None, seq, LANES), lambda b, h: (b, 0, qb + h)),
            pl.BlockSpec((None, seq, LANES), lambda b, h: (b, 0, kb + h)),
            pl.BlockSpec((None, seq, LANES), lambda b, h: (b, 0, vb + h)),
            small(LANES),
        ],
        out_specs=pl.BlockSpec((None, seq, LANES), lambda b, h: (b, 0, h)),
        scratch_shapes=[
            pltpu.VMEM((2, seq, LANES), BF16),
            pltpu.VMEM((2, seq, LANES), BF16),
            pltpu.VMEM((seq // tq, LANES + ONES_ROWS, tq), BF16),
            pltpu.VMEM((seq // tq, 2, LANES + ONES_ROWS, tq), F32),
        ],
        compiler_params=_params(("parallel", "parallel")),
        name="diff_attention",
    )(slopes, vec(lq1), vec(lk1), vec(lq2), vec(lk2), proj3, proj3, proj3, vec(subln_g))


def _dil_kernel(slopes_ref, q1_ref, q2_ref, q3_ref, k1_ref, k2_ref, k3_ref, v_ref, o_ref,
                qf, kf, vf, bias_s, acc_s, m_s, l_s, *, seq):
    hp = pl.program_id(1)
    wb = WIN_BLOCK
    hd = HEAD_DIM
    n_q_blocks = seq // wb

    qf[0] = q2_ref[...].astype(F32)
    qf[1] = q3_ref[...].astype(F32)
    kf[0] = k2_ref[...].astype(F32)
    kf[1] = k3_ref[...].astype(F32)
    vf[...] = v_ref[...].astype(F32)

    row = lax.broadcasted_iota(jnp.int32, (2 * wb, 2 * wb), 0)
    col = lax.broadcasted_iota(jnp.int32, (2 * wb, 2 * wb), 1)
    qi_ = jnp.where(row < wb, row, row - wb)
    dist = qi_ + wb - col
    slope = jnp.where(row < wb, slopes_ref[2 * hp], slopes_ref[2 * hp + 1])
    valid = jnp.logical_and(dist >= 0, dist <= wb)
    table = jnp.where(valid, -(slope * LOG2E) * dist.astype(F32), NEG)
    bias_s[1] = table
    bias_s[0] = jnp.where(col < wb, NEG, table)

    first = lax.broadcasted_iota(jnp.int32, (wb, LANES), 1) < hd
    q_scale = hd ** -0.5 * LOG2E

    for g, (window, dil) in enumerate(DIL_PAIRS):
        assert window // dil == wb
        nb = n_q_blocks // dil
        refs = {"q": (q1_ref, qf), "k": (k1_ref, kf), "v": (v_ref, vf)}

        def rows(kind, start, g=g, dil=dil, refs=refs):
            direct, copies = refs[kind]
            if dil == 1:
                return direct[pl.ds(pl.multiple_of(start, wb), wb), :]
            src = copies if kind == "v" else copies.at[g - 1]
            return src[pl.ds(start, wb, stride=dil), :]

        def body(idx, _, g=g, dil=dil, nb=nb, rows=rows):
            r = idx // nb
            n = idx % nb
            cur = n * (wb * dil) + r
            q = rows("q", cur).astype(F32) * q_scale
            zero = jnp.zeros_like(q)
            qx = jnp.concatenate([jnp.where(first, q, zero), jnp.where(first, zero, q)],
                                 axis=0).astype(BF16)
            if nb > 1:
                prev = jnp.maximum(n - 1, 0) * (wb * dil) + r
                keys = jnp.concatenate([rows("k", prev), rows("k", cur)], axis=0).astype(BF16)
                vals = jnp.concatenate([rows("v", prev), rows("v", cur)], axis=0).astype(BF16)
                bias = bias_s[jnp.minimum(n, 1)]
            else:
                keys, vals = rows("k", cur).astype(BF16), rows("v", cur).astype(BF16)
                bias = bias_s[1, :, wb:]
            vx = jnp.concatenate([vals, jnp.ones_like(vals)], axis=1)
            s = lax.dot_general(qx, keys, (((1,), (1,)), ((), ())),
                                preferred_element_type=F32) + bias
            m = jnp.max(s, axis=-1, keepdims=True)
            p = jnp.exp2(s - m).astype(BF16)
            o = jnp.dot(p, vx, preferred_element_type=F32)
            acc = jnp.where(first, o[0:wb, 0:LANES], o[wb:, 0:LANES])
            den = jnp.where(first, o[0:wb, LANES:], o[wb:, LANES:])
            mx = jnp.where(first, jnp.broadcast_to(m[0:wb], (wb, LANES)),
                           jnp.broadcast_to(m[wb:], (wb, LANES)))
            if dil == 1:
                dst = pl.ds(pl.multiple_of(cur, wb), wb)
            else:
                dst = pl.ds(cur, wb, stride=dil)
            acc_s[g, dst, :] = acc
            m_s[g, dst, :] = mx
            l_s[g, dst, :] = den
            return 0

        lax.fori_loop(0, n_q_blocks, body, 0, unroll=8)

    mr = 256
    for c in range(seq // mr):
        rs = slice(c * mr, (c + 1) * mr)
        m1, m2, m3 = m_s[0, rs, :], m_s[1, rs, :], m_s[2, rs, :]
        m = jnp.maximum(jnp.maximum(m1, m2), m3)
        w1, w2, w3 = jnp.exp2(m1 - m), jnp.exp2(m2 - m), jnp.exp2(m3 - m)
        num = w1 * acc_s[0, rs, :] + w2 * acc_s[1, rs, :] + w3 * acc_s[2, rs, :]
        den = w1 * l_s[0, rs, :] + w2 * l_s[1, rs, :] + w3 * l_s[2, rs, :]
        o_ref[rs, :] = (num / den).astype(o_ref.dtype)


def _dilated_attention(proj3, n_heads):
    bsz, seq, _ = proj3.shape
    n_pairs = n_heads * HEAD_DIM // LANES
    slopes = 2.0 ** (-8.0 * jnp.arange(1, n_heads + 1, dtype=F32) / n_heads)
    groups = len(DIL_PAIRS)

    def spec(blk0):
        return pl.BlockSpec((None, seq, LANES), lambda b, p: (b, 0, blk0 + p))

    q_specs = [spec(g * n_pairs) for g in range(groups)]
    k_specs = [spec((groups + g) * n_pairs) for g in range(groups)]
    v_spec = spec(2 * groups * n_pairs)
    return pl.pallas_call(
        functools.partial(_dil_kernel, seq=seq),
        out_shape=jax.ShapeDtypeStruct((bsz, seq, n_pairs * LANES), BF16),
        grid=(bsz, n_pairs),
        in_specs=[pl.BlockSpec(memory_space=pltpu.SMEM)] + q_specs + k_specs + [v_spec],
        out_specs=pl.BlockSpec((None, seq, LANES), lambda b, p: (b, 0, p)),
        scratch_shapes=[
            pltpu.VMEM((groups - 1, seq, LANES), F32),
            pltpu.VMEM((groups - 1, seq, LANES), F32),
            pltpu.VMEM((seq, LANES), F32),
            pltpu.VMEM((2, 2 * WIN_BLOCK, 2 * WIN_BLOCK), F32),
            pltpu.VMEM((groups, seq, LANES), F32),
            pltpu.VMEM((groups, seq, LANES), F32),
            pltpu.VMEM((groups, seq, LANES), F32),
        ],
        compiler_params=_params(("parallel", "parallel")),
        name="dilated_attention",
    )(slopes, *([proj3] * 7))


def _conv_kernel(a_ref, gate_ref, w_ref, cb_ref, lng_ref, lnb_ref, o_ref, buf, win, *,
                 ts, halo, rc):
    @pl.when(pl.program_id(1) == 0)
    def _():
        buf[0:halo, :] = jnp.zeros((halo, buf.shape[1]), F32)

    buf[halo:halo + ts, :] = a_ref[...].astype(F32) * jax.nn.sigmoid(gate_ref[...].astype(F32))
    off = halo - (CONV_K - 1)
    sub = CONV_SUBLANES
    for c in range(ts // rc):
        acc = jnp.zeros((rc, buf.shape[1]), F32)
        for r in range(sub):
            span = rc + (CONV_K - 1 - r) // sub * sub
            win[r, 0:span, :] = buf[c * rc + off + r:c * rc + off + r + span, :]
            for k in range(r, CONV_K, sub):
                acc = acc + w_ref[k:k + 1, :] * win[r, k - r:k - r + rc, :]
        y = _layer_norm(acc + cb_ref[...], lng_ref[...], lnb_ref[...])
        o_ref[c * rc:(c + 1) * rc, :] = _silu(y).astype(o_ref.dtype)
    buf[0:halo, :] = buf[ts:ts + halo, :]


def _conv_module(proj3, conv_w, conv_b, ln_g, ln_b, col0, ts=256):
    bsz, seq, _ = proj3.shape
    width = conv_w.shape[1]
    halo = 32
    rc = 64
    vec = lambda a: a.reshape(1, width)
    small = pl.BlockSpec((1, width), lambda b, s: (0, 0))
    return pl.pallas_call(
        functools.partial(_conv_kernel, ts=ts, halo=halo, rc=rc),
        out_shape=jax.ShapeDtypeStruct((bsz, seq, width), BF16),
        grid=(bsz, seq // ts),
        in_specs=[
            pl.BlockSpec((None, ts, width), lambda b, s: (b, s, col0 // width)),
            pl.BlockSpec((None, ts, width), lambda b, s: (b, s, col0 // width + 1)),
            pl.BlockSpec((CONV_K, width), lambda b, s: (0, 0)),
            small, small, small,
        ],
        out_specs=pl.BlockSpec((None, ts, width), lambda b, s: (b, s, 0)),
        scratch_shapes=[pltpu.VMEM((ts + halo, width), F32),
                        pltpu.VMEM((CONV_SUBLANES, rc + halo, width), F32)],
        compiler_params=_params(("parallel", "arbitrary")),
        name="conv_module",
    )(proj3, proj3, conv_w, vec(conv_b), vec(ln_g), vec(ln_b))


SORT_TILE = 256
PAGE = 16
TOP_K = 2
SLOTS = TOP_K * SORT_TILE + N_EXPERTS * PAGE
PAGES_PER_TILE = SLOTS // PAGE
STEP_PAGES = 16
META_ROWS = 8


def _route(logits_t, rb):
    mx = jnp.max(logits_t, axis=0, keepdims=True)
    ex = jnp.exp(logits_t - mx)
    probs = ex / jnp.sum(ex, axis=0, keepdims=True)
    sel = probs + rb
    srow = [sel[i:i + 1, :] for i in range(N_EXPERTS)]
    prow = [probs[i:i + 1, :] for i in range(N_EXPERTS)]
    epg = EXPERTS_PER_GROUP
    n_grp = N_EXPERTS // epg

    scores = []
    for g in range(n_grp):
        a, b, c, d = srow[g * epg:(g + 1) * epg]
        hi1, lo1 = jnp.maximum(a, b), jnp.minimum(a, b)
        hi2, lo2 = jnp.maximum(c, d), jnp.minimum(c, d)
        scores.append(jnp.maximum(hi1, hi2)
                      + jnp.maximum(jnp.minimum(hi1, hi2), jnp.maximum(lo1, lo2)))
    best = scores[0]
    grp = jnp.zeros(best.shape, jnp.int32)
    for g in range(1, n_grp):
        better = scores[g] > best
        grp = jnp.where(better, g, grp)
        best = jnp.where(better, scores[g], best)

    def pick(rows_, j):
        out = rows_[j]
        for g in range(1, n_grp):
            out = jnp.where(grp == g, rows_[g * epg + j], out)
        return out

    ing = [pick(srow, j) for j in range(epg)]
    ping = [pick(prow, j) for j in range(epg)]
    b0, i0, p0 = ing[0], jnp.zeros(best.shape, jnp.int32), ping[0]
    for j in range(1, epg):
        better = ing[j] > b0
        i0 = jnp.where(better, j, i0)
        p0 = jnp.where(better, ping[j], p0)
        b0 = jnp.where(better, ing[j], b0)
    b1 = jnp.full(best.shape, -jnp.inf, F32)
    i1 = jnp.zeros(best.shape, jnp.int32)
    p1 = jnp.zeros(best.shape, F32)
    for j in range(epg):
        better = jnp.logical_and(i0 != j, ing[j] > b1)
        i1 = jnp.where(better, j, i1)
        p1 = jnp.where(better, ping[j], p1)
        b1 = jnp.where(better, ing[j], b1)
    den = p0 + p1
    return grp * epg + i0, grp * epg + i1, p0 / den, p1 / den


def _sort_tile(e0, e1, hb):
    st = hb.shape[0]
    eid = lax.broadcasted_iota(jnp.int32, (N_EXPERTS, st), 0)
    sel0 = jnp.where(eid == e0, 1.0, 0.0)
    sel1 = jnp.where(eid == e1, 1.0, 0.0)
    sel = sel0 + sel1
    before = (lax.broadcasted_iota(jnp.int32, (st, st), 0)
              < lax.broadcasted_iota(jnp.int32, (st, st), 1))
    rank = jnp.dot(sel.astype(BF16), jnp.where(before, 1.0, 0.0).astype(BF16),
                   preferred_element_type=F32)
    cnt = jnp.sum(sel, axis=1, keepdims=True)
    padded = jnp.ceil(cnt * (1.0 / PAGE)) * PAGE
    ecol = lax.broadcasted_iota(jnp.int32, (N_EXPERTS, 1), 0)
    seg = jnp.zeros((N_EXPERTS, 1), F32)
    run = jnp.zeros((1, 1), F32)
    for e in range(N_EXPERTS):
        seg = jnp.where(ecol == e, run, seg)
        run = run + padded[e:e + 1, :]
    base = seg + rank
    pos0 = jnp.sum(sel0 * base, axis=0, keepdims=True)
    pos1 = jnp.sum(sel1 * base, axis=0, keepdims=True)
    slot = lax.broadcasted_iota(jnp.int32, (SLOTS, st), 0)
    perm = (jnp.where(slot == pos0.astype(jnp.int32), 1.0, 0.0)
            + jnp.where(slot == pos1.astype(jnp.int32), 1.0, 0.0)).astype(BF16)
    rows = jnp.dot(perm, hb, preferred_element_type=F32).astype(BF16)
    return pos0, pos1, cnt, rows


def _outproj_kernel(a_ref, b_ref, x_ref, mod_ref, g_ref, w_ref, rw_ref, rb_ref,
                    xo_ref, xs_ref, meta_ref, cnt_ref):
    ka = a_ref.shape[1]
    y = (jnp.dot(a_ref[...], w_ref[0:ka, :], preferred_element_type=F32)
         + jnp.dot(b_ref[...], w_ref[ka:, :], preferred_element_type=F32))
    xn = x_ref[...] + mod_ref[2:3, :] * y
    xo_ref[...] = xn
    h = _rms(xn, g_ref[...]) * (1.0 + mod_ref[4:5, :]) + mod_ref[3:4, :]
    hb = h.astype(BF16)
    hl = (h - hb.astype(F32)).astype(BF16)
    parts = (jnp.dot(hb, rw_ref[...], preferred_element_type=F32)
             + jnp.dot(hl, rw_ref[...], preferred_element_type=F32)).T
    lt = parts[0:N_EXPERTS, :] + parts[N_EXPERTS:2 * N_EXPERTS, :]
    for c in range(hb.shape[0] // SORT_TILE):
        ts = slice(c * SORT_TILE, (c + 1) * SORT_TILE)
        e0, e1, g0, g1 = _route(lt[:, ts], rb_ref[...])
        pos0, pos1, cnt, rows = _sort_tile(e0, e1, hb[ts, :])
        xs_ref[c * SLOTS:(c + 1) * SLOTS, :] = rows
        meta_ref[c] = jnp.concatenate(
            [pos0, pos1, g0, g1, jnp.zeros((META_ROWS - 4, SORT_TILE), F32)], axis=0)
        cnt_ref[c] = jnp.broadcast_to(cnt, (N_EXPERTS, LANES))


def _out_projection(a, b, x2, mod_l, g, w_bf, rw_split, rb, seq, tm=512):
    t, d = x2.shape
    ka, kb = a.shape[1], b.shape[1]
    per_b = seq // tm
    sub = tm // SORT_TILE
    n_tiles = t // SORT_TILE
    return pl.pallas_call(
        _outproj_kernel,
        out_shape=(jax.ShapeDtypeStruct((t, d), F32),
                   jax.ShapeDtypeStruct((n_tiles * SLOTS, d), BF16),
                   jax.ShapeDtypeStruct((n_tiles, META_ROWS, SORT_TILE), F32),
                   jax.ShapeDtypeStruct((n_tiles, N_EXPERTS, LANES), F32)),
        grid=(t // tm,),
        in_specs=[
            pl.BlockSpec((tm, ka), lambda i: (i, 0)),
            pl.BlockSpec((tm, kb), lambda i: (i, 0)),
            pl.BlockSpec((tm, d), lambda i: (i, 0)),
            pl.BlockSpec((None, 6, d), lambda i: (i // per_b, 0, 0)),
            pl.BlockSpec((1, d), lambda i: (0, 0)),
            pl.BlockSpec((ka + kb, d), lambda i: (0, 0)),
            pl.BlockSpec((d, LANES), lambda i: (0, 0)),
            pl.BlockSpec((N_EXPERTS, 1), lambda i: (0, 0)),
        ],
        out_specs=(pl.BlockSpec((tm, d), lambda i: (i, 0)),
                   pl.BlockSpec((sub * SLOTS, d), lambda i: (i, 0)),
                   pl.BlockSpec((sub, META_ROWS, SORT_TILE), lambda i: (i, 0, 0)),
                   pl.BlockSpec((sub, N_EXPERTS, LANES), lambda i: (i, 0, 0))),
        compiler_params=_params(("parallel",)),
        name="out_projection",
    )(a, b, x2, mod_l, g.reshape(1, d), w_bf, rw_split, rb.reshape(N_EXPERTS, 1))


def _expert_plan(counts, n_steps_max):
    n_tiles = counts.shape[0]
    npg = (counts + PAGE - 1) // PAGE
    seg = jnp.cumsum(npg, axis=1) - npg
    page_base = jnp.arange(n_tiles, dtype=jnp.int32)[:, None] * PAGES_PER_TILE + seg
    cum_t = jnp.cumsum(npg, axis=0)
    tot = cum_t[-1]
    steps_e = (tot + STEP_PAGES - 1) // STEP_PAGES
    step_end = jnp.cumsum(steps_e)
    step_start = step_end - steps_e
    n_live = step_end[-1]
    g = jnp.arange(n_steps_max, dtype=jnp.int32)
    ex = jnp.minimum(jnp.sum((step_end[None, :] <= g[:, None]).astype(jnp.int32), axis=1),
                     N_EXPERTS - 1)
    q = (g - step_start[ex])[:, None] * STEP_PAGES + jnp.arange(STEP_PAGES, dtype=jnp.int32)[None, :]
    valid = jnp.logical_and(q < tot[ex][:, None], (g < n_live)[:, None])
    cum_e = cum_t.T[ex]
    ti = jnp.minimum(jnp.sum((cum_e[:, None, :] <= q[:, :, None]).astype(jnp.int32), axis=-1),
                     n_tiles - 1)
    excl = cum_e - npg.T[ex]
    pid = (jnp.take_along_axis(page_base.T[ex], ti, axis=1)
           + q - jnp.take_along_axis(excl, ti, axis=1))
    spare = (n_tiles * PAGES_PER_TILE + (g % 2)[:, None] * STEP_PAGES
             + jnp.arange(STEP_PAGES, dtype=jnp.int32)[None, :])
    in_pages = jnp.where(valid, pid, 0).astype(jnp.int32).reshape(-1)
    out_pages = jnp.where(valid, pid, spare).astype(jnp.int32).reshape(-1)
    return in_pages, out_pages, ex.astype(jnp.int32), n_live.astype(jnp.int32).reshape(1)


def _expert_kernel(inp_ref, outp_ref, ex_ref, live_ref, *refs):
    x_refs = refs[:STEP_PAGES]
    wgu_ref, wd_ref, _, ys_ref, wgu_bf, wd_bf, ybuf, sems = refs[STEP_PAGES:]
    g = pl.program_id(0)
    live = live_ref[0]
    slot = g % 2
    ff = wd_ref.shape[0]

    def page_copies(step):
        return [pltpu.make_async_copy(ybuf.at[slot, pl.ds(k * PAGE, PAGE), :],
                                      ys_ref.at[outp_ref[step * STEP_PAGES + k]],
                                      sems.at[slot])
                for k in range(STEP_PAGES)]

    @pl.when(jnp.logical_and(g >= 2, g - 2 < live))
    def _():
        for cp in page_copies(g - 2):
            cp.wait()

    new_expert = jnp.logical_or(g == 0, ex_ref[g] != ex_ref[jnp.maximum(g - 1, 0)])

    @pl.when(jnp.logical_and(new_expert, g < live))
    def _():
        wgu_bf[...] = wgu_ref[...].astype(BF16)
        wd_bf[...] = wd_ref[...].astype(BF16)

    @pl.when(g < live)
    def _():
        x = jnp.concatenate([r[...] for r in x_refs], axis=0)
        gu = jnp.dot(x, wgu_bf[...], preferred_element_type=F32)
        act = (_silu(gu[:, :ff]) * gu[:, ff:]).astype(BF16)
        ybuf[slot] = jnp.dot(act, wd_bf[...], preferred_element_type=F32).astype(BF16)
        for cp in page_copies(g):
            cp.start()


def _experts(xs_pages, plan, w_gu, w_down, layer, n_steps):
    n_pages, _, d = xs_pages.shape
    two_ff = w_gu.shape[-1]
    ff = two_ff // 2
    in_pages, out_pages, ex, live = plan
    page_in = [pl.BlockSpec((None, PAGE, d),
                            lambda g, ip, op, e, nl, k=k: (ip[g * STEP_PAGES + k], 0, 0))
               for k in range(STEP_PAGES)]
    grid_spec = pltpu.PrefetchScalarGridSpec(
        num_scalar_prefetch=4,
        grid=(n_steps,),
        in_specs=page_in + [
            pl.BlockSpec((None, None, d, two_ff), lambda g, ip, op, e, nl: (layer, e[g], 0, 0)),
            pl.BlockSpec((None, None, ff, d), lambda g, ip, op, e, nl: (layer, e[g], 0, 0)),
            pl.BlockSpec(memory_space=pl.ANY),
        ],
        out_specs=pl.BlockSpec(memory_space=pl.ANY),
        scratch_shapes=[pltpu.VMEM((d, two_ff), BF16), pltpu.VMEM((ff, d), BF16),
                        pltpu.VMEM((2, STEP_PAGES * PAGE, d), BF16),
                        pltpu.SemaphoreType.DMA((2,))],
    )
    ys0 = jnp.zeros((n_pages + 2 * STEP_PAGES, PAGE, d), BF16)
    return pl.pallas_call(
        _expert_kernel,
        out_shape=jax.ShapeDtypeStruct(ys0.shape, BF16),
        grid_spec=grid_spec,
        input_output_aliases={4 + STEP_PAGES + 2: 0},
        compiler_params=_params(("arbitrary",)),
        name="experts",
    )(in_pages, out_pages, ex, live, *([xs_pages] * STEP_PAGES), w_gu, w_down, ys0)


def _combine_kernel(ys_ref, meta_ref, x_ref, mod_ref, fg_ref, o_ref, *, final):
    st = x_ref.shape[0]
    pos0 = meta_ref[0:1, :].astype(jnp.int32)
    pos1 = meta_ref[1:2, :].astype(jnp.int32)
    slot = lax.broadcasted_iota(jnp.int32, (SLOTS, st), 0)
    w = (jnp.where(slot == pos0, meta_ref[2:3, :], 0.0)
         + jnp.where(slot == pos1, meta_ref[3:4, :], 0.0)).astype(BF16)
    y = lax.dot_general(w, ys_ref[...], (((0,), (0,)), ((), ())), preferred_element_type=F32)
    xn = x_ref[...] + mod_ref[5:6, :] * y
    if final:
        xn = _rms(xn, fg_ref[...])
    o_ref[...] = xn


def _combine(ys_rows, meta, x2, mod_l, final_g, seq, final):
    t, d = x2.shape
    per_b = seq // SORT_TILE
    return pl.pallas_call(
        functools.partial(_combine_kernel, final=final),
        out_shape=jax.ShapeDtypeStruct((t, d), F32),
        grid=(t // SORT_TILE,),
        in_specs=[
            pl.BlockSpec((SLOTS, d), lambda i: (i, 0)),
            pl.BlockSpec((None, META_ROWS, SORT_TILE), lambda i: (i, 0, 0)),
            pl.BlockSpec((SORT_TILE, d), lambda i: (i, 0)),
            pl.BlockSpec((None, 6, d), lambda i: (i // per_b, 0, 0)),
            pl.BlockSpec((1, d), lambda i: (0, 0)),
        ],
        out_specs=pl.BlockSpec((SORT_TILE, d), lambda i: (i, 0)),
        compiler_params=_params(("parallel",)),
        name="moe_combine",
    )(ys_rows, meta, x2, mod_l, final_g.reshape(1, d))


def _moe(xs, meta, counts, w_gu, w_down, layer, x2, mod_l, final_g, seq, final):
    d = x2.shape[1]
    n_tiles = counts.shape[0]
    n_pages = n_tiles * PAGES_PER_TILE
    n_steps = n_pages // STEP_PAGES + N_EXPERTS + 2
    plan = _expert_plan(counts[:, :, 0].astype(jnp.int32), n_steps)
    ys = _experts(xs.reshape(n_pages, PAGE, d), plan, w_gu, w_down, layer, n_steps)
    return _combine(ys.reshape(-1, d), meta, x2, mod_l, final_g, seq, final)


def kernel(x, c, norm1_g, norm2_g, ada_w, ada_b, ev_w_in, ev_w_out, sgu_ln_g, sgu_ln_b, sgu_w, sgu_b, lam_q1, lam_k1, lam_q2, lam_k2, diff_subln_g, od_w_in, od_w_out, conv_w, conv_b, conv_ln_g, conv_ln_b, router_w, router_b, moe_w_gu, moe_w_down, final_g):
    bsz, seq, d = x.shape
    depth = ada_w.shape[0]
    half = d // 2
    diff_heads = half // (2 * HEAD_DIM)
    dil_heads = half // HEAD_DIM
    t = bsz * seq

    mod = _modulation(c, ada_w, ada_b)
    rw_hi = router_w.astype(BF16)
    rw_lo = (router_w - rw_hi.astype(F32)).astype(BF16)
    rw_split = jnp.pad(jnp.concatenate([rw_hi, rw_lo], axis=1),
                       ((0, 0), (0, LANES - 2 * N_EXPERTS)))

    x2 = x.reshape(t, d)
    for l in range(depth):
        i = l // 2
        if l % 2 == 0:
            lam_init = 0.8 - 0.6 * math.exp(-0.3 * l)
            proj = _in_projection(x2, mod[l], norm1_g[l], ev_w_in[i].astype(BF16), seq)
            part_a = _spatial_gating(proj, sgu_ln_g[i], sgu_ln_b[i], sgu_w[i], sgu_b[i], half)
            part_b = _diff_attention(proj.reshape(bsz, seq, -1), lam_q1[i], lam_k1[i], lam_q2[i],
                                     lam_k2[i], diff_subln_g[i], lam_init, diff_heads, 2 * half)
            w_out = ev_w_out[i]
        else:
            proj = _in_projection(x2, mod[l], norm1_g[l], od_w_in[i].astype(BF16), seq)
            proj3 = proj.reshape(bsz, seq, -1)
            part_a = _dilated_attention(proj3, dil_heads).reshape(t, half)
            part_b = _conv_module(proj3, conv_w[i], conv_b[i], conv_ln_g[i], conv_ln_b[i],
                                  (2 * len(DIL_PAIRS) + 1) * half)
            w_out = od_w_out[i]
        x2, xs, meta, counts = _out_projection(part_a, part_b.reshape(t, half), x2, mod[l],
                                               norm2_g[l], w_out.astype(BF16), rw_split,
                                               router_b, seq)
        x2 = _moe(xs, meta, counts, moe_w_gu, moe_w_down, l, x2, mod[l], final_g, seq,
                  final=(l == depth - 1))
    return x2.reshape(bsz, seq, d)
```

```python
import functools
import math

import jax
import jax.numpy as jnp
from jax import lax
from jax.experimental import pallas as pl
from jax.experimental.pallas import tpu as pltpu

F32 = jnp.float32
BF16 = jnp.bfloat16

HEAD_DIM = 64
LANES = 128
CHUNK = 128
SGU_GROUPS = 4
DIL_PAIRS = ((128, 1), (512, 4), (2048, 16))
WIN_BLOCK = 128
CONV_K = 31
CONV_SUBLANES = 8
N_EXPERTS = 16
EXPERTS_PER_GROUP = 4
RMS_EPS = 1e-6
LN_EPS = 1e-5
NEG = -1e30
LOG2E = 1.4426950408889634
VMEM_LIMIT = 56 * 1024 * 1024


def _params(sem):
    return pltpu.CompilerParams(dimension_semantics=sem, vmem_limit_bytes=VMEM_LIMIT)


def _gelu(x):
    return x * (0.5 * (1.0 + jnp.tanh(0.7978845608028654 * (x + 0.044715 * (x * x * x)))))


def _silu(x):
    return x * jax.nn.sigmoid(x)


def _rms(x, g):
    ms = jnp.mean(x * x, axis=-1, keepdims=True)
    return x * lax.rsqrt(ms + RMS_EPS) * g


def _layer_norm(x, g, b):
    mu = jnp.mean(x, axis=-1, keepdims=True)
    xc = x - mu
    var = jnp.mean(xc * xc, axis=-1, keepdims=True)
    return xc * lax.rsqrt(var + LN_EPS) * g + b


def _mod_kernel(c_ref, w_ref, b_ref, o_ref):
    ca = _silu(c_ref[...]).astype(BF16)
    o_ref[...] = jnp.dot(ca, w_ref[...].astype(BF16), preferred_element_type=F32) + b_ref[...]


def _modulation(c, ada_w, ada_b):
    depth, d, six_d = ada_w.shape
    bsz = c.shape[0]
    n = six_d // d
    out = pl.pallas_call(
        _mod_kernel,
        out_shape=jax.ShapeDtypeStruct((depth, bsz, six_d), F32),
        grid=(depth, n),
        in_specs=[
            pl.BlockSpec((bsz, d), lambda l, j: (0, 0)),
            pl.BlockSpec((None, d, d), lambda l, j: (l, 0, j)),
            pl.BlockSpec((None, 1, d), lambda l, j: (l, 0, j)),
        ],
        out_specs=pl.BlockSpec((None, bsz, d), lambda l, j: (l, 0, j)),
        compiler_params=_params(("parallel", "parallel")),
        name="modulation",
    )(c, ada_w, ada_b.reshape(depth, 1, six_d))
    return out.reshape(depth, bsz, n, d)


SORT_TILE = 256
PAGE = 16
TOP_K = 2
SLOTS = TOP_K * SORT_TILE + N_EXPERTS * PAGE
PAGES_PER_TILE = SLOTS // PAGE
STEP_PAGES = 32
META_ROWS = 8


def _unsort(ys, meta):
    pos0 = meta[0:1, :].astype(jnp.int32)
    pos1 = meta[1:2, :].astype(jnp.int32)
    slot = lax.broadcasted_iota(jnp.int32, (SLOTS, meta.shape[1]), 0)
    w = (jnp.where(slot == pos0, meta[2:3, :], 0.0)
         + jnp.where(slot == pos1, meta[3:4, :], 0.0)).astype(BF16)
    return lax.dot_general(w, ys, (((0,), (0,)), ((), ())), preferred_element_type=F32)


def _project(x, mod_ref, g_ref, w_ref, o_ref, n_chunk):
    h = _rms(x, g_ref[...]) * (1.0 + mod_ref[1:2, :]) + mod_ref[0:1, :]
    hb = h.astype(BF16)
    for j in range(0, o_ref.shape[-1], n_chunk):
        o_ref[:, j:j + n_chunk] = jnp.dot(
            hb, w_ref[:, j:j + n_chunk], preferred_element_type=F32).astype(o_ref.dtype)


def _inproj_kernel(x_ref, mod_ref, g_ref, w_ref, o_ref, *, n_chunk):
    _project(x_ref[...], mod_ref, g_ref, w_ref, o_ref, n_chunk)


def _inproj_moe_kernel(ys_ref, meta_ref, x_ref, pmod_ref, mod_ref, g_ref, w_ref, xo_ref, o_ref,
                       *, n_chunk):
    for c in range(meta_ref.shape[0]):
        ts = slice(c * SORT_TILE, (c + 1) * SORT_TILE)
        y = _unsort(ys_ref[c * SLOTS:(c + 1) * SLOTS, :], meta_ref[c])
        xo_ref[ts, :] = x_ref[ts, :] + pmod_ref[5:6, :] * y
    _project(xo_ref[...], mod_ref, g_ref, w_ref, o_ref, n_chunk)


def _in_projection(x2, mod_l, g, w_bf, seq, moe=None, tm=512):
    t, d = x2.shape
    n = w_bf.shape[1]
    per_b = seq // tm
    sub = tm // SORT_TILE
    row_spec = pl.BlockSpec((tm, d), lambda i: (i, 0))
    mod_spec = pl.BlockSpec((None, 6, d), lambda i: (i // per_b, 0, 0))
    tail_specs = [mod_spec, pl.BlockSpec((1, d), lambda i: (0, 0)),
                  pl.BlockSpec((d, n), lambda i: (0, 0))]
    proj_shape = jax.ShapeDtypeStruct((t, n), BF16)
    proj_spec = pl.BlockSpec((tm, n), lambda i: (i, 0))
    if moe is None:
        return x2, pl.pallas_call(
            functools.partial(_inproj_kernel, n_chunk=512),
            out_shape=proj_shape,
            grid=(t // tm,),
            in_specs=[row_spec] + tail_specs,
            out_specs=proj_spec,
            compiler_params=_params(("parallel",)),
            name="in_projection",
        )(x2, mod_l, g.reshape(1, d), w_bf)
    ys_rows, meta, prev_mod = moe
    return pl.pallas_call(
        functools.partial(_inproj_moe_kernel, n_chunk=512),
        out_shape=(jax.ShapeDtypeStruct((t, d), F32), proj_shape),
        grid=(t // tm,),
        in_specs=[pl.BlockSpec((sub * SLOTS, d), lambda i: (i, 0)),
                  pl.BlockSpec((sub, META_ROWS, SORT_TILE), lambda i: (i, 0, 0)),
                  row_spec, mod_spec] + tail_specs,
        out_specs=(row_spec, proj_spec),
        compiler_params=_params(("parallel",)),
        name="moe_combine_in_projection",
    )(ys_rows, meta, x2, prev_mod, mod_l, g.reshape(1, d), w_bf)


def _sgu_kernel(u_ref, v_ref, lng_ref, lnb_ref, w_ref, bias_ref, o_ref):
    u = _gelu(u_ref[...].astype(F32))
    v = _gelu(v_ref[...].astype(F32))
    vb = _layer_norm(v, lng_ref[...], lnb_ref[...]).astype(BF16)
    row = lax.broadcasted_iota(jnp.int32, (CHUNK, CHUNK), 0)
    col = lax.broadcasted_iota(jnp.int32, (CHUNK, CHUNK), 1)
    causal = col <= row
    gd = u.shape[1] // SGU_GROUPS
    for g in range(SGU_GROUPS):
        w = jnp.where(causal, w_ref[g], 0.0).astype(BF16)
        cs = slice(g * gd, (g + 1) * gd)
        for c in range(u.shape[0] // CHUNK):
            rs = slice(c * CHUNK, (c + 1) * CHUNK)
            mixed = jnp.dot(w, vb[rs, cs], preferred_element_type=F32) + bias_ref[:, cs]
            o_ref[rs, cs] = (u[rs, cs] * mixed).astype(o_ref.dtype)


def _spatial_gating(proj, ln_g, ln_b, w_s, b_s, width, tm=512):
    t = proj.shape[0]
    gd = width // SGU_GROUPS
    bias = jnp.repeat(b_s.T, gd, axis=1)
    return pl.pallas_call(
        _sgu_kernel,
        out_shape=jax.ShapeDtypeStruct((t, width), BF16),
        grid=(t // tm,),
        in_specs=[
            pl.BlockSpec((tm, width), lambda i: (i, 0)),
            pl.BlockSpec((tm, width), lambda i: (i, 1)),
            pl.BlockSpec((1, width), lambda i: (0, 0)),
            pl.BlockSpec((1, width), lambda i: (0, 0)),
            pl.BlockSpec((SGU_GROUPS, CHUNK, CHUNK), lambda i: (0, 0, 0)),
            pl.BlockSpec((CHUNK, width), lambda i: (0, 0)),
        ],
        out_specs=pl.BlockSpec((tm, width), lambda i: (i, 0)),
        compiler_params=_params(("parallel",)),
        name="spatial_gating",
    )(proj, proj, ln_g.reshape(1, width), ln_b.reshape(1, width), w_s, bias)


ONES_ROWS = 16


def _split3(x):
    hi = x.astype(BF16).astype(F32)
    r1 = x - hi
    mid = r1.astype(BF16).astype(F32)
    lo = (r1 - mid).astype(BF16).astype(F32)
    return hi, mid, lo


def _diff_kernel(slopes_ref, lq1_ref, lk1_ref, lq2_ref, lk2_ref, q_ref, k_ref, v_ref, g_ref,
                 o_ref, kx, qx, vt, acc, *, tq, lam_init):
    h = pl.program_id(1)
    seq, vd = v_ref.shape
    n_q = seq // tq
    hd = HEAD_DIM
    lane = lax.broadcasted_iota(jnp.int32, (tq, LANES), 1)

    def place(pieces, base, fill):
        out = fill
        for n, piece in enumerate(pieces):
            out = jnp.where(lane == base + n, piece, out)
        return out

    slope = slopes_ref[h] * LOG2E
    one = jnp.ones((tq, LANES), F32)
    for c in range(n_q):
        rs = slice(c * tq, (c + 1) * tq)
        k = k_ref[rs, :].astype(F32)
        pos = (lax.broadcasted_iota(jnp.int32, (tq, LANES), 0) + c * tq).astype(F32) * slope
        pieces = _split3(pos)
        zero = jnp.zeros_like(k)
        kx[0, rs, :] = jnp.where(lane < hd, k, place(pieces, hd, zero)).astype(BF16)
        kx[1, rs, :] = jnp.where(lane >= hd, k, place(pieces, 0, zero)).astype(BF16)
        vt[c, 0:vd, :] = v_ref[rs, :].astype(F32).T.astype(BF16)
        vt[c, vd:, :] = jnp.ones((ONES_ROWS, tq), BF16)
        q = q_ref[rs, :].astype(F32) * (hd ** -0.5 * LOG2E)
        qx[0, rs, :] = jnp.where(lane < hd, q, place((one, one, one), hd, zero)).astype(BF16)
        qx[1, rs, :] = jnp.where(lane >= hd, q, place((one, one, one), 0, zero)).astype(BF16)

    key_le_query = (lax.broadcasted_iota(jnp.int32, (tq, tq), 0)
                    <= lax.broadcasted_iota(jnp.int32, (tq, tq), 1))
    ms = {}
    for j in range(n_q):
        for qi in range(j, n_q):
            for m in range(2):
                s = lax.dot_general(kx[m, j * tq:(j + 1) * tq, :], qx[m, qi * tq:(qi + 1) * tq, :],
                                    (((1,), (1,)), ((), ())),
                                    preferred_element_type=F32)
                if j == qi:
                    s = jnp.where(key_le_query, s, NEG)
                m_tile = jnp.max(s, axis=0, keepdims=True)
                if j == 0:
                    ms[qi, m] = m_tile
                    p = jnp.exp2(s - m_tile).astype(BF16)
                    acc[qi, m] = jnp.dot(vt[j], p, preferred_element_type=F32)
                else:
                    m_new = jnp.maximum(ms[qi, m], m_tile)
                    alpha = jnp.exp2(ms[qi, m] - m_new)
                    ms[qi, m] = m_new
                    p = jnp.exp2(s - m_new).astype(BF16)
                    acc[qi, m] = alpha * acc[qi, m] + jnp.dot(vt[j], p, preferred_element_type=F32)

    lam = (jnp.exp(jnp.sum(lq1_ref[...] * lk1_ref[...], axis=-1, keepdims=True))
           - jnp.exp(jnp.sum(lq2_ref[...] * lk2_ref[...], axis=-1, keepdims=True)) + lam_init)
    for qi in range(n_q):
        o_t = (acc[qi, 0, 0:vd, :] / acc[qi, 0, vd:vd + 1, :]
               - lam * (acc[qi, 1, 0:vd, :] / acc[qi, 1, vd:vd + 1, :]))
        o_ref[qi * tq:(qi + 1) * tq, :] = (
            _rms(o_t.T, g_ref[...]) * (1.0 - lam_init)).astype(o_ref.dtype)


def _diff_attention(proj3, lq1, lk1, lq2, lk2, subln_g, lam_init, n_heads, col0, tq=512):
    bsz, seq, _ = proj3.shape
    slopes = 2.0 ** (-8.0 * jnp.arange(1, n_heads + 1, dtype=F32) / n_heads)
    qb, kb, vb = col0 // LANES, col0 // LANES + n_heads, col0 // LANES + 2 * n_heads
    vec = lambda a: a.reshape(1, -1)
    small = lambda n: pl.BlockSpec((1, n), lambda b, h: (0, 0))
    return pl.pallas_call(
        functools.partial(_diff_kernel, tq=tq, lam_init=lam_init),
        out_shape=jax.ShapeDtypeStruct((bsz, seq, n_heads * LANES), BF16),
        grid=(bsz, n_heads),
        in_specs=[
            pl.BlockSpec(memory_space=pltpu.SMEM),
            small(HEAD_DIM), small(HEAD_DIM), small(HEAD_DIM), small(HEAD_DIM),
            pl.BlockSpec((None, seq, LANES), lambda b, h: (b, 0, qb + h)),
            pl.BlockSpec((None, seq, LANES), lambda b, h: (b, 0, kb + h)),
            pl.BlockSpec((None, seq, LANES), lambda b, h: (b, 0, vb + h)),
            small(LANES),
        ],
        out_specs=pl.BlockSpec((None, seq, LANES), lambda b, h: (b, 0, h)),
        scratch_shapes=[
            pltpu.VMEM((2, seq, LANES), BF16),
            pltpu.VMEM((2, seq, LANES), BF16),
            pltpu.VMEM((seq // tq, LANES + ONES_ROWS, tq), BF16),
            pltpu.VMEM((seq // tq, 2, LANES + ONES_ROWS, tq), F32),
        ],
        compiler_params=_params(("parallel", "parallel")),
        name="diff_attention",
    )(slopes, vec(lq1), vec(lk1), vec(lq2), vec(lk2), proj3, proj3, proj3, vec(subln_g))


def _dil_kernel(slopes_ref, q1_ref, q2_ref, q3_ref, k1_ref, k2_ref, k3_ref, v_ref, o_ref,
                qf, kf, vf, bias_s, acc_s, m_s, l_s, *, seq):
    hp = pl.program_id(1)
    wb = WIN_BLOCK
    hd = HEAD_DIM
    n_q_blocks = seq // wb

    qf[0] = q2_ref[...].astype(F32)
    qf[1] = q3_ref[...].astype(F32)
    kf[0] = k2_ref[...].astype(F32)
    kf[1] = k3_ref[...].astype(F32)
    vf[...] = v_ref[...].astype(F32)

    row = lax.broadcasted_iota(jnp.int32, (2 * wb, 2 * wb), 0)
    col = lax.broadcasted_iota(jnp.int32, (2 * wb, 2 * wb), 1)
    qi_ = jnp.where(row < wb, row, row - wb)
    dist = qi_ + wb - col
    slope = jnp.where(row < wb, slopes_ref[2 * hp], slopes_ref[2 * hp + 1])
    valid = jnp.logical_and(dist >= 0, dist <= wb)
    table = jnp.where(valid, -(slope * LOG2E) * dist.astype(F32), NEG)
    bias_s[1] = table
    bias_s[0] = jnp.where(col < wb, NEG, table)

    first = lax.broadcasted_iota(jnp.int32, (wb, LANES), 1) < hd
    q_scale = hd ** -0.5 * LOG2E

    for g, (window, dil) in enumerate(DIL_PAIRS):
        assert window // dil == wb
        nb = n_q_blocks // dil
        refs = {"q": (q1_ref, qf), "k": (k1_ref, kf), "v": (v_ref, vf)}

        def rows(kind, start, g=g, dil=dil, refs=refs):
            direct, copies = refs[kind]
            if dil == 1:
                return direct[pl.ds(pl.multiple_of(start, wb), wb), :]
            src = copies if kind == "v" else copies.at[g - 1]
            return src[pl.ds(start, wb, stride=dil), :]

        def body(idx, _, g=g, dil=dil, nb=nb, rows=rows):
            r = idx // nb
            n = idx % nb
            cur = n * (wb * dil) + r
            q = rows("q", cur).astype(F32) * q_scale
            zero = jnp.zeros_like(q)
            qx = jnp.concatenate([jnp.where(first, q, zero), jnp.where(first, zero, q)],
                                 axis=0).astype(BF16)
            if nb > 1:
                prev = jnp.maximum(n - 1, 0) * (wb * dil) + r
                keys = jnp.concatenate([rows("k", prev), rows("k", cur)], axis=0).astype(BF16)
                vals = jnp.concatenate([rows("v", prev), rows("v", cur)], axis=0).astype(BF16)
                bias = bias_s[jnp.minimum(n, 1)]
            else:
                keys, vals = rows("k", cur).astype(BF16), rows("v", cur).astype(BF16)
                bias = bias_s[1, :, wb:]
            vx = jnp.concatenate([vals, jnp.ones_like(vals)], axis=1)
            s = lax.dot_general(qx, keys, (((1,), (1,)), ((), ())),
                                preferred_element_type=F32) + bias
            m = jnp.max(s, axis=-1, keepdims=True)
            p = jnp.exp2(s - m).astype(BF16)
            o = jnp.dot(p, vx, preferred_element_type=F32)
            acc = jnp.where(first, o[0:wb, 0:LANES], o[wb:, 0:LANES])
            den = jnp.where(first, o[0:wb, LANES:], o[wb:, LANES:])
            mx = jnp.where(first, jnp.broadcast_to(m[0:wb], (wb, LANES)),
                           jnp.broadcast_to(m[wb:], (wb, LANES)))
            if dil == 1:
                dst = pl.ds(pl.multiple_of(cur, wb), wb)
            else:
                dst = pl.ds(cur, wb, stride=dil)
            acc_s[g, dst, :] = acc
            m_s[g, dst, :] = mx
            l_s[g, dst, :] = den
            return 0

        lax.fori_loop(0, n_q_blocks, body, 0, unroll=8)

    mr = 256
    for c in range(seq // mr):
        rs = slice(c * mr, (c + 1) * mr)
        m1, m2, m3 = m_s[0, rs, :], m_s[1, rs, :], m_s[2, rs, :]
        m = jnp.maximum(jnp.maximum(m1, m2), m3)
        w1, w2, w3 = jnp.exp2(m1 - m), jnp.exp2(m2 - m), jnp.exp2(m3 - m)
        num = w1 * acc_s[0, rs, :] + w2 * acc_s[1, rs, :] + w3 * acc_s[2, rs, :]
        den = w1 * l_s[0, rs, :] + w2 * l_s[1, rs, :] + w3 * l_s[2, rs, :]
        o_ref[rs, :] = (num / den).astype(o_ref.dtype)


def _dilated_attention(proj3, n_heads):
    bsz, seq, _ = proj3.shape
    n_pairs = n_heads * HEAD_DIM // LANES
    slopes = 2.0 ** (-8.0 * jnp.arange(1, n_heads + 1, dtype=F32) / n_heads)
    groups = len(DIL_PAIRS)

    def spec(blk0):
        return pl.BlockSpec((None, seq, LANES), lambda b, p: (b, 0, blk0 + p))

    q_specs = [spec(g * n_pairs) for g in range(groups)]
    k_specs = [spec((groups + g) * n_pairs) for g in range(groups)]
    v_spec = spec(2 * groups * n_pairs)
    return pl.pallas_call(
        functools.partial(_dil_kernel, seq=seq),
        out_shape=jax.ShapeDtypeStruct((bsz, seq, n_pairs * LANES), BF16),
        grid=(bsz, n_pairs),
        in_specs=[pl.BlockSpec(memory_space=pltpu.SMEM)] + q_specs + k_specs + [v_spec],
        out_specs=pl.BlockSpec((None, seq, LANES), lambda b, p: (b, 0, p)),
        scratch_shapes=[
            pltpu.VMEM((groups - 1, seq, LANES), F32),
            pltpu.VMEM((groups - 1, seq, LANES), F32),
            pltpu.VMEM((seq, LANES), F32),
            pltpu.VMEM((2, 2 * WIN_BLOCK, 2 * WIN_BLOCK), F32),
            pltpu.VMEM((groups, seq, LANES), F32),
            pltpu.VMEM((groups, seq, LANES), F32),
            pltpu.VMEM((groups, seq, LANES), F32),
        ],
        compiler_params=_params(("parallel", "parallel")),
        name="dilated_attention",
    )(slopes, *([proj3] * 7))


def _conv_kernel(a_ref, gate_ref, w_ref, cb_ref, lng_ref, lnb_ref, o_ref, buf, win, *,
                 ts, halo, rc):
    @pl.when(pl.program_id(1) == 0)
    def _():
        buf[0:halo, :] = jnp.zeros((halo, buf.shape[1]), F32)

    buf[halo:halo + ts, :] = a_ref[...].astype(F32) * jax.nn.sigmoid(gate_ref[...].astype(F32))
    off = halo - (CONV_K - 1)
    sub = CONV_SUBLANES
    for c in range(ts // rc):
        acc = jnp.zeros((rc, buf.shape[1]), F32)
        for r in range(sub):
            span = rc + (CONV_K - 1 - r) // sub * sub
            win[r, 0:span, :] = buf[c * rc + off + r:c * rc + off + r + span, :]
            for k in range(r, CONV_K, sub):
                acc = acc + w_ref[k:k + 1, :] * win[r, k - r:k - r + rc, :]
        y = _layer_norm(acc + cb_ref[...], lng_ref[...], lnb_ref[...])
        o_ref[c * rc:(c + 1) * rc, :] = _silu(y).astype(o_ref.dtype)
    buf[0:halo, :] = buf[ts:ts + halo, :]


def _conv_module(proj3, conv_w, conv_b, ln_g, ln_b, col0, ts=256):
    bsz, seq, _ = proj3.shape
    width = conv_w.shape[1]
    halo = 32
    rc = 64
    vec = lambda a: a.reshape(1, width)
    small = pl.BlockSpec((1, width), lambda b, s: (0, 0))
    return pl.pallas_call(
        functools.partial(_conv_kernel, ts=ts, halo=halo, rc=rc),
        out_shape=jax.ShapeDtypeStruct((bsz, seq, width), BF16),
        grid=(bsz, seq // ts),
        in_specs=[
            pl.BlockSpec((None, ts, width), lambda b, s: (b, s, col0 // width)),
            pl.BlockSpec((None, ts, width), lambda b, s: (b, s, col0 // width + 1)),
            pl.BlockSpec((CONV_K, width), lambda b, s: (0, 0)),
            small, small, small,
        ],
        out_specs=pl.BlockSpec((None, ts, width), lambda b, s: (b, s, 0)),
        scratch_shapes=[pltpu.VMEM((ts + halo, width), F32),
                        pltpu.VMEM((CONV_SUBLANES, rc + halo, width), F32)],
        compiler_params=_params(("parallel", "arbitrary")),
        name="conv_module",
    )(proj3, proj3, conv_w, vec(conv_b), vec(ln_g), vec(ln_b))


def _route(logits_t, rb):
    mx = jnp.max(logits_t, axis=0, keepdims=True)
    ex = jnp.exp(logits_t - mx)
    probs = ex / jnp.sum(ex, axis=0, keepdims=True)
    sel = probs + rb
    srow = [sel[i:i + 1, :] for i in range(N_EXPERTS)]
    prow = [probs[i:i + 1, :] for i in range(N_EXPERTS)]
    epg = EXPERTS_PER_GROUP
    n_grp = N_EXPERTS // epg

    scores = []
    for g in range(n_grp):
        a, b, c, d = srow[g * epg:(g + 1) * epg]
        hi1, lo1 = jnp.maximum(a, b), jnp.minimum(a, b)
        hi2, lo2 = jnp.maximum(c, d), jnp.minimum(c, d)
        scores.append(jnp.maximum(hi1, hi2)
                      + jnp.maximum(jnp.minimum(hi1, hi2), jnp.maximum(lo1, lo2)))
    best = scores[0]
    grp = jnp.zeros(best.shape, jnp.int32)
    for g in range(1, n_grp):
        better = scores[g] > best
        grp = jnp.where(better, g, grp)
        best = jnp.where(better, scores[g], best)

    def pick(rows_, j):
        out = rows_[j]
        for g in range(1, n_grp):
            out = jnp.where(grp == g, rows_[g * epg + j], out)
        return out

    ing = [pick(srow, j) for j in range(epg)]
    ping = [pick(prow, j) for j in range(epg)]
    b0, i0, p0 = ing[0], jnp.zeros(best.shape, jnp.int32), ping[0]
    for j in range(1, epg):
        better = ing[j] > b0
        i0 = jnp.where(better, j, i0)
        p0 = jnp.where(better, ping[j], p0)
        b0 = jnp.where(better, ing[j], b0)
    b1 = jnp.full(best.shape, -jnp.inf, F32)
    i1 = jnp.zeros(best.shape, jnp.int32)
    p1 = jnp.zeros(best.shape, F32)
    for j in range(epg):
        better = jnp.logical_and(i0 != j, ing[j] > b1)
        i1 = jnp.where(better, j, i1)
        p1 = jnp.where(better, ping[j], p1)
        b1 = jnp.where(better, ing[j], b1)
    den = p0 + p1
    return grp * epg + i0, grp * epg + i1, p0 / den, p1 / den


def _sort_tile(e0, e1, hb):
    st = hb.shape[0]
    eid = lax.broadcasted_iota(jnp.int32, (N_EXPERTS, st), 0)
    sel0 = jnp.where(eid == e0, 1.0, 0.0)
    sel1 = jnp.where(eid == e1, 1.0, 0.0)
    sel = sel0 + sel1
    before = (lax.broadcasted_iota(jnp.int32, (st, st), 0)
              < lax.broadcasted_iota(jnp.int32, (st, st), 1))
    rank = jnp.dot(sel.astype(BF16), jnp.where(before, 1.0, 0.0).astype(BF16),
                   preferred_element_type=F32)
    cnt = jnp.sum(sel, axis=1, keepdims=True)
    padded = jnp.ceil(cnt * (1.0 / PAGE)) * PAGE
    ecol = lax.broadcasted_iota(jnp.int32, (N_EXPERTS, 1), 0)
    seg = jnp.zeros((N_EXPERTS, 1), F32)
    run = jnp.zeros((1, 1), F32)
    for e in range(N_EXPERTS):
        seg = jnp.where(ecol == e, run, seg)
        run = run + padded[e:e + 1, :]
    base = seg + rank
    pos0 = jnp.sum(sel0 * base, axis=0, keepdims=True)
    pos1 = jnp.sum(sel1 * base, axis=0, keepdims=True)
    slot = lax.broadcasted_iota(jnp.int32, (SLOTS, st), 0)
    perm = (jnp.where(slot == pos0.astype(jnp.int32), 1.0, 0.0)
            + jnp.where(slot == pos1.astype(jnp.int32), 1.0, 0.0)).astype(BF16)
    rows = jnp.dot(perm, hb, preferred_element_type=F32).astype(BF16)
    return pos0, pos1, cnt, rows


def _outproj_kernel(a_ref, b_ref, x_ref, mod_ref, g_ref, w_ref, rw_ref, rb_ref,
                    xo_ref, xs_ref, meta_ref, cnt_ref):
    ka = a_ref.shape[1]
    y = (jnp.dot(a_ref[...], w_ref[0:ka, :], preferred_element_type=F32)
         + jnp.dot(b_ref[...], w_ref[ka:, :], preferred_element_type=F32))
    xn = x_ref[...] + mod_ref[2:3, :] * y
    xo_ref[...] = xn
    h = _rms(xn, g_ref[...]) * (1.0 + mod_ref[4:5, :]) + mod_ref[3:4, :]
    hb = h.astype(BF16)
    hl = (h - hb.astype(F32)).astype(BF16)
    parts = (jnp.dot(hb, rw_ref[...], preferred_element_type=F32)
             + jnp.dot(hl, rw_ref[...], preferred_element_type=F32)).T
    lt = parts[0:N_EXPERTS, :] + parts[N_EXPERTS:2 * N_EXPERTS, :]
    for c in range(hb.shape[0] // SORT_TILE):
        ts = slice(c * SORT_TILE, (c + 1) * SORT_TILE)
        e0, e1, g0, g1 = _route(lt[:, ts], rb_ref[...])
        pos0, pos1, cnt, rows = _sort_tile(e0, e1, hb[ts, :])
        xs_ref[c * SLOTS:(c + 1) * SLOTS, :] = rows
        meta_ref[c] = jnp.concatenate(
            [pos0, pos1, g0, g1, jnp.zeros((META_ROWS - 4, SORT_TILE), F32)], axis=0)
        cnt_ref[c] = jnp.broadcast_to(cnt, (N_EXPERTS, LANES))


def _out_projection(a, b, x2, mod_l, g, w_bf, rw_split, rb, seq, tm=512):
    t, d = x2.shape
    ka, kb = a.shape[1], b.shape[1]
    per_b = seq // tm
    sub = tm // SORT_TILE
    n_tiles = t // SORT_TILE
    return pl.pallas_call(
        _outproj_kernel,
        out_shape=(jax.ShapeDtypeStruct((t, d), F32),
                   jax.ShapeDtypeStruct((n_tiles * SLOTS, d), BF16),
                   jax.ShapeDtypeStruct((n_tiles, META_ROWS, SORT_TILE), F32),
                   jax.ShapeDtypeStruct((n_tiles, N_EXPERTS, LANES), F32)),
        grid=(t // tm,),
        in_specs=[
            pl.BlockSpec((tm, ka), lambda i: (i, 0)),
            pl.BlockSpec((tm, kb), lambda i: (i, 0)),
            pl.BlockSpec((tm, d), lambda i: (i, 0)),
            pl.BlockSpec((None, 6, d), lambda i: (i // per_b, 0, 0)),
            pl.BlockSpec((1, d), lambda i: (0, 0)),
            pl.BlockSpec((ka + kb, d), lambda i: (0, 0)),
            pl.BlockSpec((d, LANES), lambda i: (0, 0)),
            pl.BlockSpec((N_EXPERTS, 1), lambda i: (0, 0)),
        ],
        out_specs=(pl.BlockSpec((tm, d), lambda i: (i, 0)),
                   pl.BlockSpec((sub * SLOTS, d), lambda i: (i, 0)),
                   pl.BlockSpec((sub, META_ROWS, SORT_TILE), lambda i: (i, 0, 0)),
                   pl.BlockSpec((sub, N_EXPERTS, LANES), lambda i: (i, 0, 0))),
        compiler_params=_params(("parallel",)),
        name="out_projection",
    )(a, b, x2, mod_l, g.reshape(1, d), w_bf, rw_split, rb.reshape(N_EXPERTS, 1))


def _expert_plan(counts, n_steps_max):
    n_tiles = counts.shape[0]
    n_groups = N_EXPERTS + 1
    npg = (counts + PAGE - 1) // PAGE
    npg = jnp.concatenate([npg, PAGES_PER_TILE - jnp.sum(npg, axis=1, keepdims=True)], axis=1)
    seg = jnp.cumsum(npg, axis=1) - npg
    page_base = jnp.arange(n_tiles, dtype=jnp.int32)[:, None] * PAGES_PER_TILE + seg
    cum_t = jnp.cumsum(npg, axis=0)
    tot = cum_t[-1]
    steps_e = (tot + STEP_PAGES - 1) // STEP_PAGES
    step_end = jnp.cumsum(steps_e)
    step_start = step_end - steps_e
    n_live = step_end[-1]
    g = jnp.arange(n_steps_max, dtype=jnp.int32)
    ex = jnp.minimum(jnp.sum((step_end[None, :] <= g[:, None]).astype(jnp.int32), axis=1),
                     n_groups - 1)
    q = (g - step_start[ex])[:, None] * STEP_PAGES + jnp.arange(STEP_PAGES, dtype=jnp.int32)[None, :]
    valid = jnp.logical_and(q < tot[ex][:, None], (g < n_live)[:, None])
    cum_e = cum_t.T[ex]
    ti = jnp.minimum(jnp.sum((cum_e[:, None, :] <= q[:, :, None]).astype(jnp.int32), axis=-1),
                     n_tiles - 1)
    excl = cum_e - npg.T[ex]
    pid = (jnp.take_along_axis(page_base.T[ex], ti, axis=1)
           + q - jnp.take_along_axis(excl, ti, axis=1))
    spare = (n_tiles * PAGES_PER_TILE + (g % 2)[:, None] * STEP_PAGES
             + jnp.arange(STEP_PAGES, dtype=jnp.int32)[None, :])
    first = jnp.where(g < n_live, pid[:, 0], pid[0, 0])[:, None]
    in_pages = jnp.where(valid, pid, first).astype(jnp.int32).reshape(-1)
    out_pages = jnp.where(valid, pid, spare).astype(jnp.int32).reshape(-1)
    return in_pages, out_pages, ex.astype(jnp.int32), n_live.astype(jnp.int32).reshape(1)


def _expert_kernel(inp_ref, outp_ref, ex_ref, live_ref, *refs):
    x_refs = refs[:STEP_PAGES]
    wgu_ref, wd_ref, ys_ref, wgu_bf, wd_bf, ybuf, zbuf, sems = refs[STEP_PAGES:]
    g = pl.program_id(0)
    live = live_ref[0]
    slot = g % 2
    ff = wd_ref.shape[0]
    n_spare = 2 * STEP_PAGES

    def page_copies(step, zeros=False):
        return [pltpu.make_async_copy(
            zbuf.at[k] if zeros else ybuf.at[slot, pl.ds(k * PAGE, PAGE), :],
            ys_ref.at[outp_ref[step * STEP_PAGES + k]], sems.at[slot])
                for k in range(STEP_PAGES)]

    @pl.when(g == 0)
    def _():
        zbuf[...] = jnp.zeros_like(zbuf)
        for j in range(n_spare // STEP_PAGES):
            first = ys_ref.shape[0] - n_spare + j * STEP_PAGES
            cp = pltpu.make_async_copy(zbuf, ys_ref.at[pl.ds(first, STEP_PAGES)], sems.at[1])
            cp.start()
            cp.wait()

    @pl.when(jnp.logical_and(g >= 2, g - 2 < live))
    def _():
        for cp in page_copies(g - 2):
            cp.wait()

    group = ex_ref[g]
    is_expert = jnp.logical_and(g < live, group < N_EXPERTS)
    new_expert = jnp.logical_or(g == 0, group != ex_ref[jnp.maximum(g - 1, 0)])

    @pl.when(jnp.logical_and(new_expert, is_expert))
    def _():
        wgu_bf[...] = wgu_ref[...].astype(BF16)
        wd_bf[...] = wd_ref[...].astype(BF16)

    @pl.when(is_expert)
    def _():
        x = jnp.concatenate([r[...] for r in x_refs], axis=0)
        gu = jnp.dot(x, wgu_bf[...], preferred_element_type=F32)
        act = (_silu(gu[:, :ff]) * gu[:, ff:]).astype(BF16)
        ybuf[slot] = jnp.dot(act, wd_bf[...], preferred_element_type=F32).astype(BF16)
        for cp in page_copies(g):
            cp.start()

    @pl.when(jnp.logical_and(g < live, group == N_EXPERTS))
    def _():
        for cp in page_copies(g, zeros=True):
            cp.start()


def _experts(xs_pages, plan, w_gu, w_down, layer, n_steps):
    n_pages, _, d = xs_pages.shape
    two_ff = w_gu.shape[-1]
    ff = two_ff // 2
    in_pages, out_pages, ex, live = plan
    page_in = [pl.BlockSpec((None, PAGE, d),
                            lambda g, ip, op, e, nl, k=k: (ip[g * STEP_PAGES + k], 0, 0))
               for k in range(STEP_PAGES)]
    grid_spec = pltpu.PrefetchScalarGridSpec(
        num_scalar_prefetch=4,
        grid=(n_steps,),
        in_specs=page_in + [
            pl.BlockSpec((None, None, d, two_ff),
                         lambda g, ip, op, e, nl: (layer, jnp.minimum(e[g], N_EXPERTS - 1), 0, 0)),
            pl.BlockSpec((None, None, ff, d),
                         lambda g, ip, op, e, nl: (layer, jnp.minimum(e[g], N_EXPERTS - 1), 0, 0)),
        ],
        out_specs=pl.BlockSpec(memory_space=pl.ANY),
        scratch_shapes=[pltpu.VMEM((d, two_ff), BF16), pltpu.VMEM((ff, d), BF16),
                        pltpu.VMEM((2, STEP_PAGES * PAGE, d), BF16),
                        pltpu.VMEM((STEP_PAGES, PAGE, d), BF16),
                        pltpu.SemaphoreType.DMA((2,))],
    )
    return pl.pallas_call(
        _expert_kernel,
        out_shape=jax.ShapeDtypeStruct((n_pages + 2 * STEP_PAGES, PAGE, d), BF16),
        grid_spec=grid_spec,
        compiler_params=_params(("arbitrary",)),
        name="experts",
    )(in_pages, out_pages, ex, live, *([xs_pages] * STEP_PAGES), w_gu, w_down)


def _combine_kernel(ys_ref, meta_ref, x_ref, mod_ref, fg_ref, o_ref):
    y = _unsort(ys_ref[...], meta_ref[...])
    o_ref[...] = _rms(x_ref[...] + mod_ref[5:6, :] * y, fg_ref[...])


def _final_combine(ys_rows, meta, x2, mod_l, final_g, seq):
    t, d = x2.shape
    per_b = seq // SORT_TILE
    return pl.pallas_call(
        _combine_kernel,
        out_shape=jax.ShapeDtypeStruct((t, d), F32),
        grid=(t // SORT_TILE,),
        in_specs=[
            pl.BlockSpec((SLOTS, d), lambda i: (i, 0)),
            pl.BlockSpec((None, META_ROWS, SORT_TILE), lambda i: (i, 0, 0)),
            pl.BlockSpec((SORT_TILE, d), lambda i: (i, 0)),
            pl.BlockSpec((None, 6, d), lambda i: (i // per_b, 0, 0)),
            pl.BlockSpec((1, d), lambda i: (0, 0)),
        ],
        out_specs=pl.BlockSpec((SORT_TILE, d), lambda i: (i, 0)),
        compiler_params=_params(("parallel",)),
        name="moe_combine_final_norm",
    )(ys_rows, meta, x2, mod_l, final_g.reshape(1, d))


def _moe_experts(xs, counts, w_gu, w_down, layer):
    d = xs.shape[1]
    n_tiles = counts.shape[0]
    n_pages = n_tiles * PAGES_PER_TILE
    n_steps = n_pages // STEP_PAGES + N_EXPERTS + 1 + 2
    plan = _expert_plan(counts[:, :, 0].astype(jnp.int32), n_steps)
    ys = _experts(xs.reshape(n_pages, PAGE, d), plan, w_gu, w_down, layer, n_steps)
    return ys.reshape(-1, d)


def kernel(x, c, norm1_g, norm2_g, ada_w, ada_b, ev_w_in, ev_w_out, sgu_ln_g, sgu_ln_b, sgu_w, sgu_b, lam_q1, lam_k1, lam_q2, lam_k2, diff_subln_g, od_w_in, od_w_out, conv_w, conv_b, conv_ln_g, conv_ln_b, router_w, router_b, moe_w_gu, moe_w_down, final_g):
    bsz, seq, d = x.shape
    depth = ada_w.shape[0]
    half = d // 2
    diff_heads = half // (2 * HEAD_DIM)
    dil_heads = half // HEAD_DIM
    t = bsz * seq

    mod = _modulation(c, ada_w, ada_b)
    rw_hi = router_w.astype(BF16)
    rw_lo = (router_w - rw_hi.astype(F32)).astype(BF16)
    rw_split = jnp.pad(jnp.concatenate([rw_hi, rw_lo], axis=1),
                       ((0, 0), (0, LANES - 2 * N_EXPERTS)))

    x2 = x.reshape(t, d)
    moe = None
    for l in range(depth):
        i = l // 2
        w_in = (ev_w_in if l % 2 == 0 else od_w_in)[i].astype(BF16)
        x2, proj = _in_projection(x2, mod[l], norm1_g[l], w_in, seq, moe)
        if l % 2 == 0:
            lam_init = 0.8 - 0.6 * math.exp(-0.3 * l)
            part_a = _spatial_gating(proj, sgu_ln_g[i], sgu_ln_b[i], sgu_w[i], sgu_b[i], half)
            part_b = _diff_attention(proj.reshape(bsz, seq, -1), lam_q1[i], lam_k1[i], lam_q2[i],
                                     lam_k2[i], diff_subln_g[i], lam_init, diff_heads, 2 * half)
            w_out = ev_w_out[i]
        else:
            proj3 = proj.reshape(bsz, seq, -1)
            part_a = _dilated_attention(proj3, dil_heads).reshape(t, half)
            part_b = _conv_module(proj3, conv_w[i], conv_b[i], conv_ln_g[i], conv_ln_b[i],
                                  (2 * len(DIL_PAIRS) + 1) * half)
            w_out = od_w_out[i]
        x2, xs, meta, counts = _out_projection(part_a, part_b.reshape(t, half), x2, mod[l],
                                               norm2_g[l], w_out.astype(BF16), rw_split,
                                               router_b, seq)
        moe = (_moe_experts(xs, counts, moe_w_gu, moe_w_down, l), meta, mod[l])
    out = _final_combine(moe[0], moe[1], x2, mod[depth - 1], final_g, seq)
    return out.reshape(bsz, seq, d)
```

```python
import functools
import math

import jax
import jax.numpy as jnp
from jax import lax
from jax.experimental import pallas as pl
from jax.experimental.pallas import tpu as pltpu

F32 = jnp.float32
BF16 = jnp.bfloat16

HEAD_DIM = 64
LANES = 128
CHUNK = 128
SGU_GROUPS = 4
DIL_PAIRS = ((128, 1), (512, 4), (2048, 16))
WIN_BLOCK = 128
CONV_K = 31
CONV_SUBLANES = 8
N_EXPERTS = 16
EXPERTS_PER_GROUP = 4
RMS_EPS = 1e-6
LN_EPS = 1e-5
NEG = -1e30
LOG2E = 1.4426950408889634
VMEM_LIMIT = 56 * 1024 * 1024


def _params(sem):
    return pltpu.CompilerParams(dimension_semantics=sem, vmem_limit_bytes=VMEM_LIMIT)


def _gelu(x):
    return x * (0.5 * (1.0 + jnp.tanh(0.7978845608028654 * (x + 0.044715 * (x * x * x)))))


def _silu(x):
    return x * jax.nn.sigmoid(x)


def _rms(x, g):
    ms = jnp.mean(x * x, axis=-1, keepdims=True)
    return x * lax.rsqrt(ms + RMS_EPS) * g


def _layer_norm(x, g, b):
    mu = jnp.mean(x, axis=-1, keepdims=True)
    xc = x - mu
    var = jnp.mean(xc * xc, axis=-1, keepdims=True)
    return xc * lax.rsqrt(var + LN_EPS) * g + b


def _mod_kernel(c_ref, w_ref, b_ref, o_ref):
    ca = _silu(c_ref[...]).astype(BF16)
    o_ref[...] = jnp.dot(ca, w_ref[...].astype(BF16), preferred_element_type=F32) + b_ref[...]


def _modulation(c, ada_w, ada_b):
    depth, d, six_d = ada_w.shape
    bsz = c.shape[0]
    n = six_d // d
    out = pl.pallas_call(
        _mod_kernel,
        out_shape=jax.ShapeDtypeStruct((depth, bsz, six_d), F32),
        grid=(depth, n),
        in_specs=[
            pl.BlockSpec((bsz, d), lambda l, j: (0, 0)),
            pl.BlockSpec((None, d, d), lambda l, j: (l, 0, j)),
            pl.BlockSpec((None, 1, d), lambda l, j: (l, 0, j)),
        ],
        out_specs=pl.BlockSpec((None, bsz, d), lambda l, j: (l, 0, j)),
        compiler_params=_params(("parallel", "parallel")),
        name="modulation",
    )(c, ada_w, ada_b.reshape(depth, 1, six_d))
    return out.reshape(depth, bsz, n, d)


SORT_TILE = 256
PAGE = 16
TOP_K = 2
SLOTS = TOP_K * SORT_TILE + N_EXPERTS * PAGE
PAGES_PER_TILE = SLOTS // PAGE
STEP_PAGES = 32
META_ROWS = 8


def _unsort(ys, meta):
    pos0 = meta[0:1, :].astype(jnp.int32)
    pos1 = meta[1:2, :].astype(jnp.int32)
    slot = lax.broadcasted_iota(jnp.int32, (SLOTS, meta.shape[1]), 0)
    w = (jnp.where(slot == pos0, meta[2:3, :], 0.0)
         + jnp.where(slot == pos1, meta[3:4, :], 0.0)).astype(BF16)
    return lax.dot_general(w, ys, (((0,), (0,)), ((), ())), preferred_element_type=F32)


def _project(x, mod_ref, g_ref, w_ref, o_ref, n_chunk):
    h = _rms(x, g_ref[...]) * (1.0 + mod_ref[1:2, :]) + mod_ref[0:1, :]
    hb = h.astype(BF16)
    for j in range(0, o_ref.shape[-1], n_chunk):
        o_ref[:, j:j + n_chunk] = jnp.dot(
            hb, w_ref[:, j:j + n_chunk], preferred_element_type=F32).astype(o_ref.dtype)


def _inproj_kernel(x_ref, mod_ref, g_ref, w_ref, o_ref, *, n_chunk):
    _project(x_ref[...], mod_ref, g_ref, w_ref, o_ref, n_chunk)


def _inproj_moe_kernel(ys_ref, meta_ref, x_ref, pmod_ref, mod_ref, g_ref, w_ref, xo_ref, o_ref,
                       *, n_chunk):
    for c in range(meta_ref.shape[0]):
        ts = slice(c * SORT_TILE, (c + 1) * SORT_TILE)
        y = _unsort(ys_ref[c * SLOTS:(c + 1) * SLOTS, :], meta_ref[c])
        xo_ref[ts, :] = x_ref[ts, :] + pmod_ref[5:6, :] * y
    _project(xo_ref[...], mod_ref, g_ref, w_ref, o_ref, n_chunk)


def _in_projection(x2, mod_l, g, w_bf, seq, moe=None, tm=512):
    t, d = x2.shape
    n = w_bf.shape[1]
    per_b = seq // tm
    sub = tm // SORT_TILE
    row_spec = pl.BlockSpec((tm, d), lambda i: (i, 0))
    mod_spec = pl.BlockSpec((None, 6, d), lambda i: (i // per_b, 0, 0))
    tail_specs = [mod_spec, pl.BlockSpec((1, d), lambda i: (0, 0)),
                  pl.BlockSpec((d, n), lambda i: (0, 0))]
    proj_shape = jax.ShapeDtypeStruct((t, n), BF16)
    proj_spec = pl.BlockSpec((tm, n), lambda i: (i, 0))
    if moe is None:
        return x2, pl.pallas_call(
            functools.partial(_inproj_kernel, n_chunk=512),
            out_shape=proj_shape,
            grid=(t // tm,),
            in_specs=[row_spec] + tail_specs,
            out_specs=proj_spec,
            compiler_params=_params(("parallel",)),
            name="in_projection",
        )(x2, mod_l, g.reshape(1, d), w_bf)
    ys_rows, meta, prev_mod = moe
    return pl.pallas_call(
        functools.partial(_inproj_moe_kernel, n_chunk=512),
        out_shape=(jax.ShapeDtypeStruct((t, d), F32), proj_shape),
        grid=(t // tm,),
        in_specs=[pl.BlockSpec((sub * SLOTS, d), lambda i: (i, 0)),
                  pl.BlockSpec((sub, META_ROWS, SORT_TILE), lambda i: (i, 0, 0)),
                  row_spec, mod_spec] + tail_specs,
        out_specs=(row_spec, proj_spec),
        compiler_params=_params(("parallel",)),
        name="moe_combine_in_projection",
    )(ys_rows, meta, x2, prev_mod, mod_l, g.reshape(1, d), w_bf)


def _sgu_kernel(u_ref, v_ref, lng_ref, lnb_ref, w_ref, bias_ref, o_ref):
    u = _gelu(u_ref[...].astype(F32))
    v = _gelu(v_ref[...].astype(F32))
    vb = _layer_norm(v, lng_ref[...], lnb_ref[...]).astype(BF16)
    row = lax.broadcasted_iota(jnp.int32, (CHUNK, CHUNK), 0)
    col = lax.broadcasted_iota(jnp.int32, (CHUNK, CHUNK), 1)
    causal = col <= row
    gd = u.shape[1] // SGU_GROUPS
    for g in range(SGU_GROUPS):
        w = jnp.where(causal, w_ref[g], 0.0).astype(BF16)
        cs = slice(g * gd, (g + 1) * gd)
        for c in range(u.shape[0] // CHUNK):
            rs = slice(c * CHUNK, (c + 1) * CHUNK)
            mixed = jnp.dot(w, vb[rs, cs], preferred_element_type=F32) + bias_ref[:, cs]
            o_ref[rs, cs] = (u[rs, cs] * mixed).astype(o_ref.dtype)


def _spatial_gating(proj, ln_g, ln_b, w_s, b_s, width, tm=512):
    t = proj.shape[0]
    gd = width // SGU_GROUPS
    bias = jnp.repeat(b_s.T, gd, axis=1)
    return pl.pallas_call(
        _sgu_kernel,
        out_shape=jax.ShapeDtypeStruct((t, width), BF16),
        grid=(t // tm,),
        in_specs=[
            pl.BlockSpec((tm, width), lambda i: (i, 0)),
            pl.BlockSpec((tm, width), lambda i: (i, 1)),
            pl.BlockSpec((1, width), lambda i: (0, 0)),
            pl.BlockSpec((1, width), lambda i: (0, 0)),
            pl.BlockSpec((SGU_GROUPS, CHUNK, CHUNK), lambda i: (0, 0, 0)),
            pl.BlockSpec((CHUNK, width), lambda i: (0, 0)),
        ],
        out_specs=pl.BlockSpec((tm, width), lambda i: (i, 0)),
        compiler_params=_params(("parallel",)),
        name="spatial_gating",
    )(proj, proj, ln_g.reshape(1, width), ln_b.reshape(1, width), w_s, bias)


ONES_ROWS = 16


def _split3(x):
    hi = x.astype(BF16).astype(F32)
    r1 = x - hi
    mid = r1.astype(BF16).astype(F32)
    lo = (r1 - mid).astype(BF16).astype(F32)
    return hi, mid, lo


def _diff_kernel(slopes_ref, lq1_ref, lk1_ref, lq2_ref, lk2_ref, q_ref, k_ref, v_ref, g_ref,
                 o_ref, kx, qx, vt, acc, *, tq, lam_init):
    h = pl.program_id(1)
    seq, vd = v_ref.shape
    n_q = seq // tq
    hd = HEAD_DIM
    lane = lax.broadcasted_iota(jnp.int32, (tq, LANES), 1)

    def place(pieces, base, fill):
        out = fill
        for n, piece in enumerate(pieces):
            out = jnp.where(lane == base + n, piece, out)
        return out

    slope = slopes_ref[h] * LOG2E
    one = jnp.ones((tq, LANES), F32)
    for c in range(n_q):
        rs = slice(c * tq, (c + 1) * tq)
        k = k_ref[rs, :].astype(F32)
        pos = (lax.broadcasted_iota(jnp.int32, (tq, LANES), 0) + c * tq).astype(F32) * slope
        pieces = _split3(pos)
        zero = jnp.zeros_like(k)
        kx[0, rs, :] = jnp.where(lane < hd, k, place(pieces, hd, zero)).astype(BF16)
        kx[1, rs, :] = jnp.where(lane >= hd, k, place(pieces, 0, zero)).astype(BF16)
        vt[0:vd, rs] = v_ref[rs, :].astype(F32).T.astype(BF16)
        vt[vd:, rs] = jnp.ones((ONES_ROWS, tq), BF16)
        q = q_ref[rs, :].astype(F32) * (hd ** -0.5 * LOG2E)
        qx[0, rs, :] = jnp.where(lane < hd, q, place((one, one, one), hd, zero)).astype(BF16)
        qx[1, rs, :] = jnp.where(lane >= hd, q, place((one, one, one), 0, zero)).astype(BF16)

    kc = tq
    chunks = {qi: [(k0, min(k0 + kc, (qi + 1) * tq)) for k0 in range(0, (qi + 1) * tq, kc)]
              for qi in range(n_q)}
    ms = {}
    for c in range(max(len(v) for v in chunks.values())):
        for qi in range(n_q):
            if c >= len(chunks[qi]):
                continue
            k0, k1 = chunks[qi][c]
            for m in range(2):
                s = lax.dot_general(kx[m, k0:k1, :], qx[m, qi * tq:(qi + 1) * tq, :],
                                    (((1,), (1,)), ((), ())),
                                    preferred_element_type=F32)
                if k1 > qi * tq + 1:
                    key_i = lax.broadcasted_iota(jnp.int32, s.shape, 0) + k0
                    qry_i = lax.broadcasted_iota(jnp.int32, s.shape, 1) + qi * tq
                    s = jnp.where(key_i <= qry_i, s, NEG)
                m_tile = jnp.max(s, axis=0, keepdims=True)
                if c == 0:
                    ms[qi, m] = m_tile
                    p = jnp.exp2(s - m_tile).astype(BF16)
                    acc[qi, m] = jnp.dot(vt[:, k0:k1], p, preferred_element_type=F32)
                else:
                    m_new = jnp.maximum(ms[qi, m], m_tile)
                    alpha = jnp.exp2(ms[qi, m] - m_new)
                    ms[qi, m] = m_new
                    p = jnp.exp2(s - m_new).astype(BF16)
                    acc[qi, m] = (alpha * acc[qi, m]
                                  + jnp.dot(vt[:, k0:k1], p, preferred_element_type=F32))

    lam = (jnp.exp(jnp.sum(lq1_ref[...] * lk1_ref[...], axis=-1, keepdims=True))
           - jnp.exp(jnp.sum(lq2_ref[...] * lk2_ref[...], axis=-1, keepdims=True)) + lam_init)
    for qi in range(n_q):
        o_t = (acc[qi, 0, 0:vd, :] / acc[qi, 0, vd:vd + 1, :]
               - lam * (acc[qi, 1, 0:vd, :] / acc[qi, 1, vd:vd + 1, :]))
        o_ref[qi * tq:(qi + 1) * tq, :] = (
            _rms(o_t.T, g_ref[...]) * (1.0 - lam_init)).astype(o_ref.dtype)


def _diff_attention(proj3, lq1, lk1, lq2, lk2, subln_g, lam_init, n_heads, col0, tq=1024):
    bsz, seq, _ = proj3.shape
    slopes = 2.0 ** (-8.0 * jnp.arange(1, n_heads + 1, dtype=F32) / n_heads)
    qb, kb, vb = col0 // LANES, col0 // LANES + n_heads, col0 // LANES + 2 * n_heads
    vec = lambda a: a.reshape(1, -1)
    small = lambda n: pl.BlockSpec((1, n), lambda b, h: (0, 0))
    return pl.pallas_call(
        functools.partial(_diff_kernel, tq=tq, lam_init=lam_init),
        out_shape=jax.ShapeDtypeStruct((bsz, seq, n_heads * LANES), BF16),
        grid=(bsz, n_heads),
        in_specs=[
            pl.BlockSpec(memory_space=pltpu.SMEM),
            small(HEAD_DIM), small(HEAD_DIM), small(HEAD_DIM), small(HEAD_DIM),
            pl.BlockSpec((None, seq, LANES), lambda b, h: (b, 0, qb + h)),
            pl.BlockSpec((None, seq, LANES), lambda b, h: (b, 0, kb + h)),
            pl.BlockSpec((None, seq, LANES), lambda b, h: (b, 0, vb + h)),
            small(LANES),
        ],
        out_specs=pl.BlockSpec((None, seq, LANES), lambda b, h: (b, 0, h)),
        scratch_shapes=[
            pltpu.VMEM((2, seq, LANES), BF16),
            pltpu.VMEM((2, seq, LANES), BF16),
            pltpu.VMEM((LANES + ONES_ROWS, seq), BF16),
            pltpu.VMEM((seq // tq, 2, LANES + ONES_ROWS, tq), F32),
        ],
        compiler_params=_params(("parallel", "parallel")),
        name="diff_attention",
    )(slopes, vec(lq1), vec(lk1), vec(lq2), vec(lk2), proj3, proj3, proj3, vec(subln_g))


def _dil_kernel(slopes_ref, q1_ref, q2_ref, q3_ref, k1_ref, k2_ref, k3_ref, v_ref, o_ref,
                qf, kf, vf, bias_s, acc_s, m_s, l_s, *, seq):
    hp = pl.program_id(1)
    wb = WIN_BLOCK
    hd = HEAD_DIM
    n_q_blocks = seq // wb

    qf[0] = q2_ref[...].astype(F32)
    qf[1] = q3_ref[...].astype(F32)
    kf[0] = k2_ref[...].astype(F32)
    kf[1] = k3_ref[...].astype(F32)
    vf[...] = v_ref[...].astype(F32)

    row = lax.broadcasted_iota(jnp.int32, (2 * wb, 2 * wb), 0)
    col = lax.broadcasted_iota(jnp.int32, (2 * wb, 2 * wb), 1)
    qi_ = jnp.where(row < wb, row, row - wb)
    dist = qi_ + wb - col
    slope = jnp.where(row < wb, slopes_ref[2 * hp], slopes_ref[2 * hp + 1])
    valid = jnp.logical_and(dist >= 0, dist <= wb)
    table = jnp.where(valid, -(slope * LOG2E) * dist.astype(F32), NEG)
    bias_s[1] = table
    bias_s[0] = jnp.where(col < wb, NEG, table)

    first = lax.broadcasted_iota(jnp.int32, (wb, LANES), 1) < hd
    q_scale = hd ** -0.5 * LOG2E

    for g, (window, dil) in enumerate(DIL_PAIRS):
        assert window // dil == wb
        nb = n_q_blocks // dil
        refs = {"q": (q1_ref, qf), "k": (k1_ref, kf), "v": (v_ref, vf)}

        def rows(kind, start, g=g, dil=dil, refs=refs):
            direct, copies = refs[kind]
            if dil == 1:
                return direct[pl.ds(pl.multiple_of(start, wb), wb), :]
            src = copies if kind == "v" else copies.at[g - 1]
            return src[pl.ds(start, wb, stride=dil), :]

        def body(idx, _, g=g, dil=dil, nb=nb, rows=rows):
            r = idx // nb
            n = idx % nb
            cur = n * (wb * dil) + r
            q = rows("q", cur).astype(F32) * q_scale
            zero = jnp.zeros_like(q)
            qx = jnp.concatenate([jnp.where(first, q, zero), jnp.where(first, zero, q)],
                                 axis=0).astype(BF16)
            if nb > 1:
                prev = jnp.maximum(n - 1, 0) * (wb * dil) + r
                keys = jnp.concatenate([rows("k", prev), rows("k", cur)], axis=0).astype(BF16)
                vals = jnp.concatenate([rows("v", prev), rows("v", cur)], axis=0).astype(BF16)
                bias = bias_s[jnp.minimum(n, 1)]
            else:
                keys, vals = rows("k", cur).astype(BF16), rows("v", cur).astype(BF16)
                bias = bias_s[1, :, wb:]
            vx = jnp.concatenate([vals, jnp.ones_like(vals)], axis=1)
            s = lax.dot_general(qx, keys, (((1,), (1,)), ((), ())),
                                preferred_element_type=F32) + bias
            m = jnp.max(s, axis=-1, keepdims=True)
            p = jnp.exp2(s - m).astype(BF16)
            o = jnp.dot(p, vx, preferred_element_type=F32)
            acc = jnp.where(first, o[0:wb, 0:LANES], o[wb:, 0:LANES])
            den = jnp.where(first, o[0:wb, LANES:], o[wb:, LANES:])
            mx = jnp.where(first, jnp.broadcast_to(m[0:wb], (wb, LANES)),
                           jnp.broadcast_to(m[wb:], (wb, LANES)))
            if dil == 1:
                dst = pl.ds(pl.multiple_of(cur, wb), wb)
            else:
                dst = pl.ds(cur, wb, stride=dil)
            acc_s[g, dst, :] = acc
            m_s[g, dst, :] = mx
            l_s[g, dst, :] = den
            return 0

        lax.fori_loop(0, n_q_blocks, body, 0, unroll=8)

    mr = 256
    for c in range(seq // mr):
        rs = slice(c * mr, (c + 1) * mr)
        m1, m2, m3 = m_s[0, rs, :], m_s[1, rs, :], m_s[2, rs, :]
        m = jnp.maximum(jnp.maximum(m1, m2), m3)
        w1, w2, w3 = jnp.exp2(m1 - m), jnp.exp2(m2 - m), jnp.exp2(m3 - m)
        num = w1 * acc_s[0, rs, :] + w2 * acc_s[1, rs, :] + w3 * acc_s[2, rs, :]
        den = w1 * l_s[0, rs, :] + w2 * l_s[1, rs, :] + w3 * l_s[2, rs, :]
        o_ref[rs, :] = (num / den).astype(o_ref.dtype)


def _dilated_attention(proj3, n_heads):
    bsz, seq, _ = proj3.shape
    n_pairs = n_heads * HEAD_DIM // LANES
    slopes = 2.0 ** (-8.0 * jnp.arange(1, n_heads + 1, dtype=F32) / n_heads)
    groups = len(DIL_PAIRS)

    def spec(blk0):
        return pl.BlockSpec((None, seq, LANES), lambda b, p: (b, 0, blk0 + p))

    q_specs = [spec(g * n_pairs) for g in range(groups)]
    k_specs = [spec((groups + g) * n_pairs) for g in range(groups)]
    v_spec = spec(2 * groups * n_pairs)
    return pl.pallas_call(
        functools.partial(_dil_kernel, seq=seq),
        out_shape=jax.ShapeDtypeStruct((bsz, seq, n_pairs * LANES), BF16),
        grid=(bsz, n_pairs),
        in_specs=[pl.BlockSpec(memory_space=pltpu.SMEM)] + q_specs + k_specs + [v_spec],
        out_specs=pl.BlockSpec((None, seq, LANES), lambda b, p: (b, 0, p)),
        scratch_shapes=[
            pltpu.VMEM((groups - 1, seq, LANES), F32),
            pltpu.VMEM((groups - 1, seq, LANES), F32),
            pltpu.VMEM((seq, LANES), F32),
            pltpu.VMEM((2, 2 * WIN_BLOCK, 2 * WIN_BLOCK), F32),
            pltpu.VMEM((groups, seq, LANES), F32),
            pltpu.VMEM((groups, seq, LANES), F32),
            pltpu.VMEM((groups, seq, LANES), F32),
        ],
        compiler_params=_params(("parallel", "parallel")),
        name="dilated_attention",
    )(slopes, *([proj3] * 7))


def _conv_kernel(a_ref, gate_ref, w_ref, cb_ref, lng_ref, lnb_ref, o_ref, buf, win, *,
                 ts, halo, rc):
    @pl.when(pl.program_id(1) == 0)
    def _():
        buf[0:halo, :] = jnp.zeros((halo, buf.shape[1]), F32)

    buf[halo:halo + ts, :] = a_ref[...].astype(F32) * jax.nn.sigmoid(gate_ref[...].astype(F32))
    off = halo - (CONV_K - 1)
    sub = CONV_SUBLANES
    for c in range(ts // rc):
        acc = jnp.zeros((rc, buf.shape[1]), F32)
        for r in range(sub):
            span = rc + (CONV_K - 1 - r) // sub * sub
            win[r, 0:span, :] = buf[c * rc + off + r:c * rc + off + r + span, :]
            for k in range(r, CONV_K, sub):
                acc = acc + w_ref[k:k + 1, :] * win[r, k - r:k - r + rc, :]
        y = _layer_norm(acc + cb_ref[...], lng_ref[...], lnb_ref[...])
        o_ref[c * rc:(c + 1) * rc, :] = _silu(y).astype(o_ref.dtype)
    buf[0:halo, :] = buf[ts:ts + halo, :]


def _conv_module(proj3, conv_w, conv_b, ln_g, ln_b, col0, ts=256):
    bsz, seq, _ = proj3.shape
    width = conv_w.shape[1]
    halo = 32
    rc = 64
    vec = lambda a: a.reshape(1, width)
    small = pl.BlockSpec((1, width), lambda b, s: (0, 0))
    return pl.pallas_call(
        functools.partial(_conv_kernel, ts=ts, halo=halo, rc=rc),
        out_shape=jax.ShapeDtypeStruct((bsz, seq, width), BF16),
        grid=(bsz, seq // ts),
        in_specs=[
            pl.BlockSpec((None, ts, width), lambda b, s: (b, s, col0 // width)),
            pl.BlockSpec((None, ts, width), lambda b, s: (b, s, col0 // width + 1)),
            pl.BlockSpec((CONV_K, width), lambda b, s: (0, 0)),
            small, small, small,
        ],
        out_specs=pl.BlockSpec((None, ts, width), lambda b, s: (b, s, 0)),
        scratch_shapes=[pltpu.VMEM((ts + halo, width), F32),
                        pltpu.VMEM((CONV_SUBLANES, rc + halo, width), F32)],
        compiler_params=_params(("parallel", "arbitrary")),
        name="conv_module",
    )(proj3, proj3, conv_w, vec(conv_b), vec(ln_g), vec(ln_b))


def _route(logits_t, rb):
    mx = jnp.max(logits_t, axis=0, keepdims=True)
    ex = jnp.exp(logits_t - mx)
    probs = ex / jnp.sum(ex, axis=0, keepdims=True)
    sel = probs + rb
    srow = [sel[i:i + 1, :] for i in range(N_EXPERTS)]
    prow = [probs[i:i + 1, :] for i in range(N_EXPERTS)]
    epg = EXPERTS_PER_GROUP
    n_grp = N_EXPERTS // epg

    scores = []
    for g in range(n_grp):
        a, b, c, d = srow[g * epg:(g + 1) * epg]
        hi1, lo1 = jnp.maximum(a, b), jnp.minimum(a, b)
        hi2, lo2 = jnp.maximum(c, d), jnp.minimum(c, d)
        scores.append(jnp.maximum(hi1, hi2)
                      + jnp.maximum(jnp.minimum(hi1, hi2), jnp.maximum(lo1, lo2)))
    best = scores[0]
    grp = jnp.zeros(best.shape, jnp.int32)
    for g in range(1, n_grp):
        better = scores[g] > best
        grp = jnp.where(better, g, grp)
        best = jnp.where(better, scores[g], best)

    def pick(rows_, j):
        out = rows_[j]
        for g in range(1, n_grp):
            out = jnp.where(grp == g, rows_[g * epg + j], out)
        return out

    ing = [pick(srow, j) for j in range(epg)]
    ping = [pick(prow, j) for j in range(epg)]
    b0, i0, p0 = ing[0], jnp.zeros(best.shape, jnp.int32), ping[0]
    for j in range(1, epg):
        better = ing[j] > b0
        i0 = jnp.where(better, j, i0)
        p0 = jnp.where(better, ping[j], p0)
        b0 = jnp.where(better, ing[j], b0)
    b1 = jnp.full(best.shape, -jnp.inf, F32)
    i1 = jnp.zeros(best.shape, jnp.int32)
    p1 = jnp.zeros(best.shape, F32)
    for j in range(epg):
        better = jnp.logical_and(i0 != j, ing[j] > b1)
        i1 = jnp.where(better, j, i1)
        p1 = jnp.where(better, ping[j], p1)
        b1 = jnp.where(better, ing[j], b1)
    den = p0 + p1
    return grp * epg + i0, grp * epg + i1, p0 / den, p1 / den


def _sort_tile(e0, e1, hb):
    st = hb.shape[0]
    eid = lax.broadcasted_iota(jnp.int32, (N_EXPERTS, st), 0)
    sel0 = jnp.where(eid == e0, 1.0, 0.0)
    sel1 = jnp.where(eid == e1, 1.0, 0.0)
    sel = sel0 + sel1
    before = (lax.broadcasted_iota(jnp.int32, (st, st), 0)
              < lax.broadcasted_iota(jnp.int32, (st, st), 1))
    rank = jnp.dot(sel.astype(BF16), jnp.where(before, 1.0, 0.0).astype(BF16),
                   preferred_element_type=F32)
    cnt = jnp.sum(sel, axis=1, keepdims=True)
    padded = jnp.ceil(cnt * (1.0 / PAGE)) * PAGE
    ecol = lax.broadcasted_iota(jnp.int32, (N_EXPERTS, 1), 0)
    seg = jnp.zeros((N_EXPERTS, 1), F32)
    run = jnp.zeros((1, 1), F32)
    for e in range(N_EXPERTS):
        seg = jnp.where(ecol == e, run, seg)
        run = run + padded[e:e + 1, :]
    base = seg + rank
    pos0 = jnp.sum(sel0 * base, axis=0, keepdims=True)
    pos1 = jnp.sum(sel1 * base, axis=0, keepdims=True)
    slot = lax.broadcasted_iota(jnp.int32, (SLOTS, st), 0)
    perm = (jnp.where(slot == pos0.astype(jnp.int32), 1.0, 0.0)
            + jnp.where(slot == pos1.astype(jnp.int32), 1.0, 0.0)).astype(BF16)
    rows = jnp.dot(perm, hb, preferred_element_type=F32).astype(BF16)
    return pos0, pos1, cnt, rows


def _outproj_kernel(a_ref, b_ref, x_ref, mod_ref, g_ref, w_ref, rw_ref, rb_ref,
                    xo_ref, xs_ref, meta_ref, cnt_ref):
    ka = a_ref.shape[1]
    for c in range(x_ref.shape[0] // SORT_TILE):
        ts = slice(c * SORT_TILE, (c + 1) * SORT_TILE)
        y = (jnp.dot(a_ref[ts, :], w_ref[0:ka, :], preferred_element_type=F32)
             + jnp.dot(b_ref[ts, :], w_ref[ka:, :], preferred_element_type=F32))
        xn = x_ref[ts, :] + mod_ref[2:3, :] * y
        xo_ref[ts, :] = xn
        h = _rms(xn, g_ref[...]) * (1.0 + mod_ref[4:5, :]) + mod_ref[3:4, :]
        hb = h.astype(BF16)
        hl = (h - hb.astype(F32)).astype(BF16)
        parts = (jnp.dot(hb, rw_ref[...], preferred_element_type=F32)
                 + jnp.dot(hl, rw_ref[...], preferred_element_type=F32)).T
        lt = parts[0:N_EXPERTS, :] + parts[N_EXPERTS:2 * N_EXPERTS, :]
        e0, e1, g0, g1 = _route(lt, rb_ref[...])
        pos0, pos1, cnt, rows = _sort_tile(e0, e1, hb)
        xs_ref[c * SLOTS:(c + 1) * SLOTS, :] = rows
        meta_ref[c] = jnp.concatenate(
            [pos0, pos1, g0, g1, jnp.zeros((META_ROWS - 4, SORT_TILE), F32)], axis=0)
        cnt_ref[c] = jnp.broadcast_to(cnt, (N_EXPERTS, LANES))


def _out_projection(a, b, x2, mod_l, g, w_bf, rw_split, rb, seq, tm=1024):
    t, d = x2.shape
    ka, kb = a.shape[1], b.shape[1]
    per_b = seq // tm
    sub = tm // SORT_TILE
    n_tiles = t // SORT_TILE
    return pl.pallas_call(
        _outproj_kernel,
        out_shape=(jax.ShapeDtypeStruct((t, d), F32),
                   jax.ShapeDtypeStruct((n_tiles * SLOTS, d), BF16),
                   jax.ShapeDtypeStruct((n_tiles, META_ROWS, SORT_TILE), F32),
                   jax.ShapeDtypeStruct((n_tiles, N_EXPERTS, LANES), F32)),
        grid=(t // tm,),
        in_specs=[
            pl.BlockSpec((tm, ka), lambda i: (i, 0)),
            pl.BlockSpec((tm, kb), lambda i: (i, 0)),
            pl.BlockSpec((tm, d), lambda i: (i, 0)),
            pl.BlockSpec((None, 6, d), lambda i: (i // per_b, 0, 0)),
            pl.BlockSpec((1, d), lambda i: (0, 0)),
            pl.BlockSpec((ka + kb, d), lambda i: (0, 0)),
            pl.BlockSpec((d, LANES), lambda i: (0, 0)),
            pl.BlockSpec((N_EXPERTS, 1), lambda i: (0, 0)),
        ],
        out_specs=(pl.BlockSpec((tm, d), lambda i: (i, 0)),
                   pl.BlockSpec((sub * SLOTS, d), lambda i: (i, 0)),
                   pl.BlockSpec((sub, META_ROWS, SORT_TILE), lambda i: (i, 0, 0)),
                   pl.BlockSpec((sub, N_EXPERTS, LANES), lambda i: (i, 0, 0))),
        compiler_params=_params(("parallel",)),
        name="out_projection",
    )(a, b, x2, mod_l, g.reshape(1, d), w_bf, rw_split, rb.reshape(N_EXPERTS, 1))


def _expert_plan(counts, n_steps_max):
    n_tiles = counts.shape[0]
    n_groups = N_EXPERTS + 1
    npg = (counts + PAGE - 1) // PAGE
    npg = jnp.concatenate([npg, PAGES_PER_TILE - jnp.sum(npg, axis=1, keepdims=True)], axis=1)
    seg = jnp.cumsum(npg, axis=1) - npg
    page_base = jnp.arange(n_tiles, dtype=jnp.int32)[:, None] * PAGES_PER_TILE + seg
    cum_t = jnp.cumsum(npg, axis=0)
    tot = cum_t[-1]
    steps_e = (tot + STEP_PAGES - 1) // STEP_PAGES
    step_end = jnp.cumsum(steps_e)
    step_start = step_end - steps_e
    n_live = step_end[-1]
    g = jnp.arange(n_steps_max, dtype=jnp.int32)
    ex = jnp.minimum(jnp.sum((step_end[None, :] <= g[:, None]).astype(jnp.int32), axis=1),
                     n_groups - 1)
    q = (g - step_start[ex])[:, None] * STEP_PAGES + jnp.arange(STEP_PAGES, dtype=jnp.int32)[None, :]
    valid = jnp.logical_and(q < tot[ex][:, None], (g < n_live)[:, None])
    cum_e = cum_t.T[ex]
    ti = jnp.minimum(jnp.sum((cum_e[:, None, :] <= q[:, :, None]).astype(jnp.int32), axis=-1),
                     n_tiles - 1)
    excl = cum_e - npg.T[ex]
    pid = (jnp.take_along_axis(page_base.T[ex], ti, axis=1)
           + q - jnp.take_along_axis(excl, ti, axis=1))
    spare = (n_tiles * PAGES_PER_TILE + (g % 2)[:, None] * STEP_PAGES
             + jnp.arange(STEP_PAGES, dtype=jnp.int32)[None, :])
    first = jnp.where(g < n_live, pid[:, 0], pid[0, 0])[:, None]
    in_pages = jnp.where(valid, pid, first).astype(jnp.int32).reshape(-1)
    out_pages = jnp.where(valid, pid, spare).astype(jnp.int32).reshape(-1)
    return in_pages, out_pages, ex.astype(jnp.int32), n_live.astype(jnp.int32).reshape(1)


def _expert_kernel(inp_ref, outp_ref, ex_ref, live_ref, xs_ref, wgu_ref, wd_ref, ys_ref,
                   wgu_bf, wd_bf, xbuf, ybuf, zbuf, in_sems, out_sems):
    g = pl.program_id(0)
    live = live_ref[0]
    slot = g % 2
    ff = wd_ref.shape[0]
    n_spare = 2 * STEP_PAGES
    half_rows = STEP_PAGES * PAGE // 2

    def is_expert_step(step):
        group = ex_ref[jnp.minimum(step, pl.num_programs(0) - 1)]
        return jnp.logical_and(step < live, group < N_EXPERTS)

    def in_copies(step, buf):
        return [pltpu.make_async_copy(xs_ref.at[inp_ref[step * STEP_PAGES + k]],
                                      xbuf.at[buf, pl.ds(k * PAGE, PAGE), :], in_sems.at[buf])
                for k in range(STEP_PAGES)]

    def out_copies(step, pages, zeros=False):
        return [pltpu.make_async_copy(
            zbuf.at[k] if zeros else ybuf.at[slot, pl.ds(k * PAGE, PAGE), :],
            ys_ref.at[outp_ref[step * STEP_PAGES + k]], out_sems.at[slot])
                for k in pages]

    @pl.when(g == 0)
    def _():
        zbuf[...] = jnp.zeros_like(zbuf)
        for j in range(n_spare // STEP_PAGES):
            first = ys_ref.shape[0] - n_spare + j * STEP_PAGES
            cp = pltpu.make_async_copy(zbuf, ys_ref.at[pl.ds(first, STEP_PAGES)], out_sems.at[1])
            cp.start()
            cp.wait()

        @pl.when(is_expert_step(0))
        def _():
            for cp in in_copies(0, 0):
                cp.start()

    @pl.when(jnp.logical_and(g >= 2, g - 2 < live))
    def _():
        for cp in out_copies(g - 2, range(STEP_PAGES)):
            cp.wait()

    group = ex_ref[g]
    is_expert = is_expert_step(g)
    new_expert = jnp.logical_or(g == 0, group != ex_ref[jnp.maximum(g - 1, 0)])

    @pl.when(jnp.logical_and(new_expert, is_expert))
    def _():
        wgu_bf[...] = wgu_ref[...].astype(BF16)
        wd_bf[...] = wd_ref[...].astype(BF16)

    @pl.when(is_expert)
    def _():
        for cp in in_copies(g, slot):
            cp.wait()

        @pl.when(is_expert_step(g + 1))
        def _():
            for cp in in_copies(g + 1, 1 - slot):
                cp.start()

        gu = jnp.dot(xbuf[slot], wgu_bf[...], preferred_element_type=F32)
        act = (_silu(gu[:, :ff]) * gu[:, ff:]).astype(BF16)
        for hlf in range(2):
            rows = slice(hlf * half_rows, (hlf + 1) * half_rows)
            ybuf[slot, rows, :] = jnp.dot(act[rows, :], wd_bf[...],
                                          preferred_element_type=F32).astype(BF16)
            for cp in out_copies(g, range(hlf * STEP_PAGES // 2, (hlf + 1) * STEP_PAGES // 2)):
                cp.start()

    @pl.when(jnp.logical_and(g < live, group == N_EXPERTS))
    def _():
        for cp in out_copies(g, range(STEP_PAGES), zeros=True):
            cp.start()


def _experts(xs_pages, plan, w_gu, w_down, layer, n_steps):
    n_pages, _, d = xs_pages.shape
    two_ff = w_gu.shape[-1]
    ff = two_ff // 2
    in_pages, out_pages, ex, live = plan
    grid_spec = pltpu.PrefetchScalarGridSpec(
        num_scalar_prefetch=4,
        grid=(n_steps,),
        in_specs=[
            pl.BlockSpec(memory_space=pl.ANY),
            pl.BlockSpec((None, None, d, two_ff),
                         lambda g, ip, op, e, nl: (layer, jnp.minimum(e[g], N_EXPERTS - 1), 0, 0)),
            pl.BlockSpec((None, None, ff, d),
                         lambda g, ip, op, e, nl: (layer, jnp.minimum(e[g], N_EXPERTS - 1), 0, 0)),
        ],
        out_specs=pl.BlockSpec(memory_space=pl.ANY),
        scratch_shapes=[pltpu.VMEM((d, two_ff), BF16), pltpu.VMEM((ff, d), BF16),
                        pltpu.VMEM((2, STEP_PAGES * PAGE, d), BF16),
                        pltpu.VMEM((2, STEP_PAGES * PAGE, d), BF16),
                        pltpu.VMEM((STEP_PAGES, PAGE, d), BF16),
                        pltpu.SemaphoreType.DMA((2,)),
                        pltpu.SemaphoreType.DMA((2,))],
    )
    return pl.pallas_call(
        _expert_kernel,
        out_shape=jax.ShapeDtypeStruct((n_pages + 2 * STEP_PAGES, PAGE, d), BF16),
        grid_spec=grid_spec,
        compiler_params=_params(("arbitrary",)),
        name="experts",
    )(in_pages, out_pages, ex, live, xs_pages, w_gu, w_down)


def _combine_kernel(ys_ref, meta_ref, x_ref, mod_ref, fg_ref, o_ref):
    y = _unsort(ys_ref[...], meta_ref[...])
    o_ref[...] = _rms(x_ref[...] + mod_ref[5:6, :] * y, fg_ref[...])


def _final_combine(ys_rows, meta, x2, mod_l, final_g, seq):
    t, d = x2.shape
    per_b = seq // SORT_TILE
    return pl.pallas_call(
        _combine_kernel,
        out_shape=jax.ShapeDtypeStruct((t, d), F32),
        grid=(t // SORT_TILE,),
        in_specs=[
            pl.BlockSpec((SLOTS, d), lambda i: (i, 0)),
            pl.BlockSpec((None, META_ROWS, SORT_TILE), lambda i: (i, 0, 0)),
            pl.BlockSpec((SORT_TILE, d), lambda i: (i, 0)),
            pl.BlockSpec((None, 6, d), lambda i: (i // per_b, 0, 0)),
            pl.BlockSpec((1, d), lambda i: (0, 0)),
        ],
        out_specs=pl.BlockSpec((SORT_TILE, d), lambda i: (i, 0)),
        compiler_params=_params(("parallel",)),
        name="moe_combine_final_norm",
    )(ys_rows, meta, x2, mod_l, final_g.reshape(1, d))


def _moe_experts(xs, counts, w_gu, w_down, layer):
    d = xs.shape[1]
    n_tiles = counts.shape[0]
    n_pages = n_tiles * PAGES_PER_TILE
    n_steps = n_pages // STEP_PAGES + N_EXPERTS + 1 + 2
    plan = _expert_plan(counts[:, :, 0].astype(jnp.int32), n_steps)
    ys = _experts(xs.reshape(n_pages, PAGE, d), plan, w_gu, w_down, layer, n_steps)
    return ys.reshape(-1, d)


def kernel(x, c, norm1_g, norm2_g, ada_w, ada_b, ev_w_in, ev_w_out, sgu_ln_g, sgu_ln_b, sgu_w, sgu_b, lam_q1, lam_k1, lam_q2, lam_k2, diff_subln_g, od_w_in, od_w_out, conv_w, conv_b, conv_ln_g, conv_ln_b, router_w, router_b, moe_w_gu, moe_w_down, final_g):
    bsz, seq, d = x.shape
    depth = ada_w.shape[0]
    half = d // 2
    diff_heads = half // (2 * HEAD_DIM)
    dil_heads = half // HEAD_DIM
    t = bsz * seq

    mod = _modulation(c, ada_w, ada_b)
    rw_hi = router_w.astype(BF16)
    rw_lo = (router_w - rw_hi.astype(F32)).astype(BF16)
    rw_split = jnp.pad(jnp.concatenate([rw_hi, rw_lo], axis=1),
                       ((0, 0), (0, LANES - 2 * N_EXPERTS)))

    x2 = x.reshape(t, d)
    moe = None
    for l in range(depth):
        i = l // 2
        w_in = (ev_w_in if l % 2 == 0 else od_w_in)[i].astype(BF16)
        x2, proj = _in_projection(x2, mod[l], norm1_g[l], w_in, seq, moe)
        if l % 2 == 0:
            lam_init = 0.8 - 0.6 * math.exp(-0.3 * l)
            part_a = _spatial_gating(proj, sgu_ln_g[i], sgu_ln_b[i], sgu_w[i], sgu_b[i], half)
            part_b = _diff_attention(proj.reshape(bsz, seq, -1), lam_q1[i], lam_k1[i], lam_q2[i],
                                     lam_k2[i], diff_subln_g[i], lam_init, diff_heads, 2 * half)
            w_out = ev_w_out[i]
        else:
            proj3 = proj.reshape(bsz, seq, -1)
            part_a = _dilated_attention(proj3, dil_heads).reshape(t, half)
            part_b = _conv_module(proj3, conv_w[i], conv_b[i], conv_ln_g[i], conv_ln_b[i],
                                  (2 * len(DIL_PAIRS) + 1) * half)
            w_out = od_w_out[i]
        x2, xs, meta, counts = _out_projection(part_a, part_b.reshape(t, half), x2, mod[l],
                                               norm2_g[l], w_out.astype(BF16), rw_split,
                                               router_b, seq)
        moe = (_moe_experts(xs, counts, moe_w_gu, moe_w_down, l), meta, mod[l])
    out = _final_combine(moe[0], moe[1], x2, mod[depth - 1], final_g, seq)
    return out.reshape(bsz, seq, d)
```

```python
import functools
import math

import jax
import jax.numpy as jnp
from jax import lax
from jax.experimental import pallas as pl
from jax.experimental.pallas import tpu as pltpu

F32 = jnp.float32
BF16 = jnp.bfloat16

HEAD_DIM = 64
LANES = 128
CHUNK = 128
SGU_GROUPS = 4
DIL_PAIRS = ((128, 1), (512, 4), (2048, 16))
WIN_BLOCK = 128
CONV_K = 31
CONV_SUBLANES = 8
N_EXPERTS = 16
EXPERTS_PER_GROUP = 4
RMS_EPS = 1e-6
LN_EPS = 1e-5
NEG = -1e30
LOG2E = 1.4426950408889634
VMEM_LIMIT = 56 * 1024 * 1024


def _params(sem):
    return pltpu.CompilerParams(dimension_semantics=sem, vmem_limit_bytes=VMEM_LIMIT)


def _gelu(x):
    return x * (0.5 * (1.0 + jnp.tanh(0.7978845608028654 * (x + 0.044715 * (x * x * x)))))


def _silu(x):
    return x * jax.nn.sigmoid(x)


def _rms(x, g):
    ms = jnp.mean(x * x, axis=-1, keepdims=True)
    return x * lax.rsqrt(ms + RMS_EPS) * g


def _layer_norm(x, g, b):
    mu = jnp.mean(x, axis=-1, keepdims=True)
    xc = x - mu
    var = jnp.mean(xc * xc, axis=-1, keepdims=True)
    return xc * lax.rsqrt(var + LN_EPS) * g + b


def _mod_kernel(c_ref, w_ref, b_ref, o_ref):
    ca = _silu(c_ref[...]).astype(BF16)
    o_ref[...] = jnp.dot(ca, w_ref[...].astype(BF16), preferred_element_type=F32) + b_ref[...]


def _modulation(c, ada_w, ada_b):
    depth, d, six_d = ada_w.shape
    bsz = c.shape[0]
    n = six_d // d
    out = pl.pallas_call(
        _mod_kernel,
        out_shape=jax.ShapeDtypeStruct((depth, bsz, six_d), F32),
        grid=(depth, n),
        in_specs=[
            pl.BlockSpec((bsz, d), lambda l, j: (0, 0)),
            pl.BlockSpec((None, d, d), lambda l, j: (l, 0, j)),
            pl.BlockSpec((None, 1, d), lambda l, j: (l, 0, j)),
        ],
        out_specs=pl.BlockSpec((None, bsz, d), lambda l, j: (l, 0, j)),
        compiler_params=_params(("parallel", "parallel")),
        name="modulation",
    )(c, ada_w, ada_b.reshape(depth, 1, six_d))
    return out.reshape(depth, bsz, n, d)


SORT_TILE = 256
PAGE = 16
TOP_K = 2
SLOTS = TOP_K * SORT_TILE + N_EXPERTS * PAGE
PAGES_PER_TILE = SLOTS // PAGE
STEP_PAGES = 32
META_ROWS = 8


def _unsort(ys, meta):
    pos0 = meta[0:1, :].astype(jnp.int32)
    pos1 = meta[1:2, :].astype(jnp.int32)
    slot = lax.broadcasted_iota(jnp.int32, (SLOTS, meta.shape[1]), 0)
    w = (jnp.where(slot == pos0, meta[2:3, :], 0.0)
         + jnp.where(slot == pos1, meta[3:4, :], 0.0)).astype(BF16)
    return lax.dot_general(w, ys, (((0,), (0,)), ((), ())), preferred_element_type=F32)


def _project(x, mod_ref, g_ref, w_ref, o_ref, n_chunk):
    h = _rms(x, g_ref[...]) * (1.0 + mod_ref[1:2, :]) + mod_ref[0:1, :]
    hb = h.astype(BF16)
    for j in range(0, o_ref.shape[-1], n_chunk):
        o_ref[:, j:j + n_chunk] = jnp.dot(
            hb, w_ref[:, j:j + n_chunk], preferred_element_type=F32).astype(o_ref.dtype)


def _inproj_kernel(x_ref, mod_ref, g_ref, w_ref, o_ref, *, n_chunk):
    _project(x_ref[...], mod_ref, g_ref, w_ref, o_ref, n_chunk)


def _inproj_moe_kernel(ys_ref, meta_ref, x_ref, pmod_ref, mod_ref, g_ref, w_ref, xo_ref, o_ref,
                       *, n_chunk):
    for c in range(meta_ref.shape[0]):
        ts = slice(c * SORT_TILE, (c + 1) * SORT_TILE)
        y = _unsort(ys_ref[c * SLOTS:(c + 1) * SLOTS, :], meta_ref[c])
        xo_ref[ts, :] = x_ref[ts, :] + pmod_ref[5:6, :] * y
    _project(xo_ref[...], mod_ref, g_ref, w_ref, o_ref, n_chunk)


def _in_projection(x2, mod_l, g, w_bf, seq, moe=None, tm=512):
    t, d = x2.shape
    n = w_bf.shape[1]
    per_b = seq // tm
    sub = tm // SORT_TILE
    row_spec = pl.BlockSpec((tm, d), lambda i: (i, 0))
    mod_spec = pl.BlockSpec((None, 6, d), lambda i: (i // per_b, 0, 0))
    tail_specs = [mod_spec, pl.BlockSpec((1, d), lambda i: (0, 0)),
                  pl.BlockSpec((d, n), lambda i: (0, 0))]
    proj_shape = jax.ShapeDtypeStruct((t, n), BF16)
    proj_spec = pl.BlockSpec((tm, n), lambda i: (i, 0))
    if moe is None:
        return x2, pl.pallas_call(
            functools.partial(_inproj_kernel, n_chunk=512),
            out_shape=proj_shape,
            grid=(t // tm,),
            in_specs=[row_spec] + tail_specs,
            out_specs=proj_spec,
            compiler_params=_params(("parallel",)),
            name="in_projection",
        )(x2, mod_l, g.reshape(1, d), w_bf)
    ys_rows, meta, prev_mod = moe
    return pl.pallas_call(
        functools.partial(_inproj_moe_kernel, n_chunk=512),
        out_shape=(jax.ShapeDtypeStruct((t, d), F32), proj_shape),
        grid=(t // tm,),
        in_specs=[pl.BlockSpec((sub * SLOTS, d), lambda i: (i, 0)),
                  pl.BlockSpec((sub, META_ROWS, SORT_TILE), lambda i: (i, 0, 0)),
                  row_spec, mod_spec] + tail_specs,
        out_specs=(row_spec, proj_spec),
        compiler_params=_params(("parallel",)),
        name="moe_combine_in_projection",
    )(ys_rows, meta, x2, prev_mod, mod_l, g.reshape(1, d), w_bf)


def _sgu_kernel(u_ref, v_ref, lng_ref, lnb_ref, w_ref, bias_ref, o_ref):
    u = _gelu(u_ref[...].astype(F32))
    v = _gelu(v_ref[...].astype(F32))
    vb = _layer_norm(v, lng_ref[...], lnb_ref[...]).astype(BF16)
    row = lax.broadcasted_iota(jnp.int32, (CHUNK, CHUNK), 0)
    col = lax.broadcasted_iota(jnp.int32, (CHUNK, CHUNK), 1)
    causal = col <= row
    gd = u.shape[1] // SGU_GROUPS
    for g in range(SGU_GROUPS):
        w = jnp.where(causal, w_ref[g], 0.0).astype(BF16)
        cs = slice(g * gd, (g + 1) * gd)
        for c in range(u.shape[0] // CHUNK):
            rs = slice(c * CHUNK, (c + 1) * CHUNK)
            mixed = jnp.dot(w, vb[rs, cs], preferred_element_type=F32) + bias_ref[:, cs]
            o_ref[rs, cs] = (u[rs, cs] * mixed).astype(o_ref.dtype)


def _spatial_gating(proj, ln_g, ln_b, w_s, b_s, width, tm=512):
    t = proj.shape[0]
    gd = width // SGU_GROUPS
    bias = jnp.repeat(b_s.T, gd, axis=1)
    return pl.pallas_call(
        _sgu_kernel,
        out_shape=jax.ShapeDtypeStruct((t, width), BF16),
        grid=(t // tm,),
        in_specs=[
            pl.BlockSpec((tm, width), lambda i: (i, 0)),
            pl.BlockSpec((tm, width), lambda i: (i, 1)),
            pl.BlockSpec((1, width), lambda i: (0, 0)),
            pl.BlockSpec((1, width), lambda i: (0, 0)),
            pl.BlockSpec((SGU_GROUPS, CHUNK, CHUNK), lambda i: (0, 0, 0)),
            pl.BlockSpec((CHUNK, width), lambda i: (0, 0)),
        ],
        out_specs=pl.BlockSpec((tm, width), lambda i: (i, 0)),
        compiler_params=_params(("parallel",)),
        name="spatial_gating",
    )(proj, proj, ln_g.reshape(1, width), ln_b.reshape(1, width), w_s, bias)


ONES_ROWS = 16


def _split3(x):
    hi = x.astype(BF16).astype(F32)
    r1 = x - hi
    mid = r1.astype(BF16).astype(F32)
    lo = (r1 - mid).astype(BF16).astype(F32)
    return hi, mid, lo


def _diff_kernel(slopes_ref, lq1_ref, lk1_ref, lq2_ref, lk2_ref, q_ref, k_ref, v_ref, g_ref,
                 o_ref, kx, qx, vt, acc, *, tq, lam_init):
    h = pl.program_id(1)
    seq, vd = v_ref.shape
    n_q = seq // tq
    hd = HEAD_DIM
    lane = lax.broadcasted_iota(jnp.int32, (tq, LANES), 1)

    def place(pieces, base, fill):
        out = fill
        for n, piece in enumerate(pieces):
            out = jnp.where(lane == base + n, piece, out)
        return out

    slope = slopes_ref[h] * LOG2E
    one = jnp.ones((tq, LANES), F32)
    for c in range(n_q):
        rs = slice(c * tq, (c + 1) * tq)
        k = k_ref[rs, :].astype(F32)
        pos = (lax.broadcasted_iota(jnp.int32, (tq, LANES), 0) + c * tq).astype(F32) * slope
        pieces = _split3(pos)
        zero = jnp.zeros_like(k)
        kx[0, rs, :] = jnp.where(lane < hd, k, place(pieces, hd, zero)).astype(BF16)
        kx[1, rs, :] = jnp.where(lane >= hd, k, place(pieces, 0, zero)).astype(BF16)
        vt[0:vd, rs] = v_ref[rs, :].astype(F32).T.astype(BF16)
        vt[vd:, rs] = jnp.ones((ONES_ROWS, tq), BF16)
        q = q_ref[rs, :].astype(F32) * (hd ** -0.5 * LOG2E)
        qx[0, rs, :] = jnp.where(lane < hd, q, place((one, one, one), hd, zero)).astype(BF16)
        qx[1, rs, :] = jnp.where(lane >= hd, q, place((one, one, one), 0, zero)).astype(BF16)

    kc = tq
    chunks = {qi: [(k0, min(k0 + kc, (qi + 1) * tq)) for k0 in range(0, (qi + 1) * tq, kc)]
              for qi in range(n_q)}
    ms = {}
    for c in range(max(len(v) for v in chunks.values())):
        for qi in range(n_q):
            if c >= len(chunks[qi]):
                continue
            k0, k1 = chunks[qi][c]
            for m in range(2):
                s = lax.dot_general(kx[m, k0:k1, :], qx[m, qi * tq:(qi + 1) * tq, :],
                                    (((1,), (1,)), ((), ())),
                                    preferred_element_type=F32)
                if k1 > qi * tq + 1:
                    key_i = lax.broadcasted_iota(jnp.int32, s.shape, 0) + k0
                    qry_i = lax.broadcasted_iota(jnp.int32, s.shape, 1) + qi * tq
                    s = jnp.where(key_i <= qry_i, s, NEG)
                m_tile = jnp.max(s, axis=0, keepdims=True)
                if c == 0:
                    ms[qi, m] = m_tile
                    p = jnp.exp2(s - m_tile).astype(BF16)
                    acc[qi, m] = jnp.dot(vt[:, k0:k1], p, preferred_element_type=F32)
                else:
                    m_new = jnp.maximum(ms[qi, m], m_tile)
                    alpha = jnp.exp2(ms[qi, m] - m_new)
                    ms[qi, m] = m_new
                    p = jnp.exp2(s - m_new).astype(BF16)
                    acc[qi, m] = (alpha * acc[qi, m]
                                  + jnp.dot(vt[:, k0:k1], p, preferred_element_type=F32))

    lam = (jnp.exp(jnp.sum(lq1_ref[...] * lk1_ref[...], axis=-1, keepdims=True))
           - jnp.exp(jnp.sum(lq2_ref[...] * lk2_ref[...], axis=-1, keepdims=True)) + lam_init)
    for qi in range(n_q):
        o_t = (acc[qi, 0, 0:vd, :] / acc[qi, 0, vd:vd + 1, :]
               - lam * (acc[qi, 1, 0:vd, :] / acc[qi, 1, vd:vd + 1, :]))
        o_ref[qi * tq:(qi + 1) * tq, :] = (
            _rms(o_t.T, g_ref[...]) * (1.0 - lam_init)).astype(o_ref.dtype)


def _diff_attention(proj3, lq1, lk1, lq2, lk2, subln_g, lam_init, n_heads, col0, tq=1024):
    bsz, seq, _ = proj3.shape
    slopes = 2.0 ** (-8.0 * jnp.arange(1, n_heads + 1, dtype=F32) / n_heads)
    qb, kb, vb = col0 // LANES, col0 // LANES + n_heads, col0 // LANES + 2 * n_heads
    vec = lambda a: a.reshape(1, -1)
    small = lambda n: pl.BlockSpec((1, n), lambda b, h: (0, 0))
    return pl.pallas_call(
        functools.partial(_diff_kernel, tq=tq, lam_init=lam_init),
        out_shape=jax.ShapeDtypeStruct((bsz, seq, n_heads * LANES), BF16),
        grid=(bsz, n_heads),
        in_specs=[
            pl.BlockSpec(memory_space=pltpu.SMEM),
            small(HEAD_DIM), small(HEAD_DIM), small(HEAD_DIM), small(HEAD_DIM),
            pl.BlockSpec((None, seq, LANES), lambda b, h: (b, 0, qb + h)),
            pl.BlockSpec((None, seq, LANES), lambda b, h: (b, 0, kb + h)),
            pl.BlockSpec((None, seq, LANES), lambda b, h: (b, 0, vb + h)),
            small(LANES),
        ],
        out_specs=pl.BlockSpec((None, seq, LANES), lambda b, h: (b, 0, h)),
        scratch_shapes=[
            pltpu.VMEM((2, seq, LANES), BF16),
            pltpu.VMEM((2, seq, LANES), BF16),
            pltpu.VMEM((LANES + ONES_ROWS, seq), BF16),
            pltpu.VMEM((seq // tq, 2, LANES + ONES_ROWS, tq), F32),
        ],
        compiler_params=_params(("parallel", "parallel")),
        name="diff_attention",
    )(slopes, vec(lq1), vec(lk1), vec(lq2), vec(lk2), proj3, proj3, proj3, vec(subln_g))


def _dil_kernel(slopes_ref, q1_ref, q2_ref, q3_ref, k1_ref, k2_ref, k3_ref, v_ref, o_ref,
                qf, kf, vf, bias_s, acc_s, m_s, l_s, *, seq):
    hp = pl.program_id(1)
    wb = WIN_BLOCK
    hd = HEAD_DIM
    n_q_blocks = seq // wb

    qf[0] = q2_ref[...].astype(F32)
    qf[1] = q3_ref[...].astype(F32)
    kf[0] = k2_ref[...].astype(F32)
    kf[1] = k3_ref[...].astype(F32)
    vf[...] = v_ref[...].astype(F32)

    row = lax.broadcasted_iota(jnp.int32, (2 * wb, 2 * wb), 0)
    col = lax.broadcasted_iota(jnp.int32, (2 * wb, 2 * wb), 1)
    qi_ = jnp.where(row < wb, row, row - wb)
    dist = qi_ + wb - col
    slope = jnp.where(row < wb, slopes_ref[2 * hp], slopes_ref[2 * hp + 1])
    valid = jnp.logical_and(dist >= 0, dist <= wb)
    table = jnp.where(valid, -(slope * LOG2E) * dist.astype(F32), NEG)
    bias_s[1] = table
    bias_s[0] = jnp.where(col < wb, NEG, table)

    first = lax.broadcasted_iota(jnp.int32, (wb, LANES), 1) < hd
    q_scale = hd ** -0.5 * LOG2E

    for g, (window, dil) in enumerate(DIL_PAIRS):
        assert window // dil == wb
        nb = n_q_blocks // dil
        refs = {"q": (q1_ref, qf), "k": (k1_ref, kf), "v": (v_ref, vf)}

        def rows(kind, start, g=g, dil=dil, refs=refs):
            direct, copies = refs[kind]
            if dil == 1:
                return direct[pl.ds(pl.multiple_of(start, wb), wb), :]
            src = copies if kind == "v" else copies.at[g - 1]
            return src[pl.ds(start, wb, stride=dil), :]

        def body(idx, _, g=g, dil=dil, nb=nb, rows=rows):
            r = idx // nb
            n = idx % nb
            cur = n * (wb * dil) + r
            q = rows("q", cur).astype(F32) * q_scale
            zero = jnp.zeros_like(q)
            qx = jnp.concatenate([jnp.where(first, q, zero), jnp.where(first, zero, q)],
                                 axis=0).astype(BF16)
            if nb > 1:
                prev = jnp.maximum(n - 1, 0) * (wb * dil) + r
                keys = jnp.concatenate([rows("k", prev), rows("k", cur)], axis=0).astype(BF16)
                vals = jnp.concatenate([rows("v", prev), rows("v", cur)], axis=0).astype(BF16)
                bias = bias_s[jnp.minimum(n, 1)]
            else:
                keys, vals = rows("k", cur).astype(BF16), rows("v", cur).astype(BF16)
                bias = bias_s[1, :, wb:]
            vx = jnp.concatenate([vals, jnp.ones_like(vals)], axis=1)
            s = lax.dot_general(qx, keys, (((1,), (1,)), ((), ())),
                                preferred_element_type=F32) + bias
            m = jnp.max(s, axis=-1, keepdims=True)
            p = jnp.exp2(s - m).astype(BF16)
            o = jnp.dot(p, vx, preferred_element_type=F32)
            acc = jnp.where(first, o[0:wb, 0:LANES], o[wb:, 0:LANES])
            den = jnp.where(first, o[0:wb, LANES:], o[wb:, LANES:])
            mx = jnp.where(first, jnp.broadcast_to(m[0:wb], (wb, LANES)),
                           jnp.broadcast_to(m[wb:], (wb, LANES)))
            if dil == 1:
                dst = pl.ds(pl.multiple_of(cur, wb), wb)
            else:
                dst = pl.ds(cur, wb, stride=dil)
            acc_s[g, dst, :] = acc
            m_s[g, dst, :] = mx
            l_s[g, dst, :] = den
            return 0

        lax.fori_loop(0, n_q_blocks, body, 0, unroll=8)

    mr = 256
    for c in range(seq // mr):
        rs = slice(c * mr, (c + 1) * mr)
        m1, m2, m3 = m_s[0, rs, :], m_s[1, rs, :], m_s[2, rs, :]
        m = jnp.maximum(jnp.maximum(m1, m2), m3)
        w1, w2, w3 = jnp.exp2(m1 - m), jnp.exp2(m2 - m), jnp.exp2(m3 - m)
        num = w1 * acc_s[0, rs, :] + w2 * acc_s[1, rs, :] + w3 * acc_s[2, rs, :]
        den = w1 * l_s[0, rs, :] + w2 * l_s[1, rs, :] + w3 * l_s[2, rs, :]
        o_ref[rs, :] = (num / den).astype(o_ref.dtype)


def _dilated_attention(proj3, n_heads):
    bsz, seq, _ = proj3.shape
    n_pairs = n_heads * HEAD_DIM // LANES
    slopes = 2.0 ** (-8.0 * jnp.arange(1, n_heads + 1, dtype=F32) / n_heads)
    groups = len(DIL_PAIRS)

    def spec(blk0):
        return pl.BlockSpec((None, seq, LANES), lambda b, p: (b, 0, blk0 + p))

    q_specs = [spec(g * n_pairs) for g in range(groups)]
    k_specs = [spec((groups + g) * n_pairs) for g in range(groups)]
    v_spec = spec(2 * groups * n_pairs)
    return pl.pallas_call(
        functools.partial(_dil_kernel, seq=seq),
        out_shape=jax.ShapeDtypeStruct((bsz, seq, n_pairs * LANES), BF16),
        grid=(bsz, n_pairs),
        in_specs=[pl.BlockSpec(memory_space=pltpu.SMEM)] + q_specs + k_specs + [v_spec],
        out_specs=pl.BlockSpec((None, seq, LANES), lambda b, p: (b, 0, p)),
        scratch_shapes=[
            pltpu.VMEM((groups - 1, seq, LANES), F32),
            pltpu.VMEM((groups - 1, seq, LANES), F32),
            pltpu.VMEM((seq, LANES), F32),
            pltpu.VMEM((2, 2 * WIN_BLOCK, 2 * WIN_BLOCK), F32),
            pltpu.VMEM((groups, seq, LANES), F32),
            pltpu.VMEM((groups, seq, LANES), F32),
            pltpu.VMEM((groups, seq, LANES), F32),
        ],
        compiler_params=_params(("parallel", "parallel")),
        name="dilated_attention",
    )(slopes, *([proj3] * 7))


def _conv_kernel(a_ref, gate_ref, w_ref, cb_ref, lng_ref, lnb_ref, o_ref, buf, win, *,
                 ts, halo, rc):
    @pl.when(pl.program_id(1) == 0)
    def _():
        buf[0:halo, :] = jnp.zeros((halo, buf.shape[1]), F32)

    buf[halo:halo + ts, :] = a_ref[...].astype(F32) * jax.nn.sigmoid(gate_ref[...].astype(F32))
    off = halo - (CONV_K - 1)
    sub = CONV_SUBLANES
    for c in range(ts // rc):
        acc = jnp.zeros((rc, buf.shape[1]), F32)
        for r in range(sub):
            span = rc + (CONV_K - 1 - r) // sub * sub
            win[r, 0:span, :] = buf[c * rc + off + r:c * rc + off + r + span, :]
            for k in range(r, CONV_K, sub):
                acc = acc + w_ref[k:k + 1, :] * win[r, k - r:k - r + rc, :]
        y = _layer_norm(acc + cb_ref[...], lng_ref[...], lnb_ref[...])
        o_ref[c * rc:(c + 1) * rc, :] = _silu(y).astype(o_ref.dtype)
    buf[0:halo, :] = buf[ts:ts + halo, :]


def _conv_module(proj3, conv_w, conv_b, ln_g, ln_b, col0, ts=256):
    bsz, seq, _ = proj3.shape
    width = conv_w.shape[1]
    halo = 32
    rc = 64
    vec = lambda a: a.reshape(1, width)
    small = pl.BlockSpec((1, width), lambda b, s: (0, 0))
    return pl.pallas_call(
        functools.partial(_conv_kernel, ts=ts, halo=halo, rc=rc),
        out_shape=jax.ShapeDtypeStruct((bsz, seq, width), BF16),
        grid=(bsz, seq // ts),
        in_specs=[
            pl.BlockSpec((None, ts, width), lambda b, s: (b, s, col0 // width)),
            pl.BlockSpec((None, ts, width), lambda b, s: (b, s, col0 // width + 1)),
            pl.BlockSpec((CONV_K, width), lambda b, s: (0, 0)),
            small, small, small,
        ],
        out_specs=pl.BlockSpec((None, ts, width), lambda b, s: (b, s, 0)),
        scratch_shapes=[pltpu.VMEM((ts + halo, width), F32),
                        pltpu.VMEM((CONV_SUBLANES, rc + halo, width), F32)],
        compiler_params=_params(("parallel", "arbitrary")),
        name="conv_module",
    )(proj3, proj3, conv_w, vec(conv_b), vec(ln_g), vec(ln_b))


def _route(logits_t, rb):
    mx = jnp.max(logits_t, axis=0, keepdims=True)
    ex = jnp.exp(logits_t - mx)
    probs = ex / jnp.sum(ex, axis=0, keepdims=True)
    sel = probs + rb
    srow = [sel[i:i + 1, :] for i in range(N_EXPERTS)]
    prow = [probs[i:i + 1, :] for i in range(N_EXPERTS)]
    epg = EXPERTS_PER_GROUP
    n_grp = N_EXPERTS // epg

    scores = []
    for g in range(n_grp):
        a, b, c, d = srow[g * epg:(g + 1) * epg]
        hi1, lo1 = jnp.maximum(a, b), jnp.minimum(a, b)
        hi2, lo2 = jnp.maximum(c, d), jnp.minimum(c, d)
        scores.append(jnp.maximum(hi1, hi2)
                      + jnp.maximum(jnp.minimum(hi1, hi2), jnp.maximum(lo1, lo2)))
    best = scores[0]
    grp = jnp.zeros(best.shape, jnp.int32)
    for g in range(1, n_grp):
        better = scores[g] > best
        grp = jnp.where(better, g, grp)
        best = jnp.where(better, scores[g], best)

    def pick(rows_, j):
        out = rows_[j]
        for g in range(1, n_grp):
            out = jnp.where(grp == g, rows_[g * epg + j], out)
        return out

    ing = [pick(srow, j) for j in range(epg)]
    ping = [pick(prow, j) for j in range(epg)]
    b0, i0, p0 = ing[0], jnp.zeros(best.shape, jnp.int32), ping[0]
    for j in range(1, epg):
        better = ing[j] > b0
        i0 = jnp.where(better, j, i0)
        p0 = jnp.where(better, ping[j], p0)
        b0 = jnp.where(better, ing[j], b0)
    b1 = jnp.full(best.shape, -jnp.inf, F32)
    i1 = jnp.zeros(best.shape, jnp.int32)
    p1 = jnp.zeros(best.shape, F32)
    for j in range(epg):
        better = jnp.logical_and(i0 != j, ing[j] > b1)
        i1 = jnp.where(better, j, i1)
        p1 = jnp.where(better, ping[j], p1)
        b1 = jnp.where(better, ing[j], b1)
    den = p0 + p1
    return grp * epg + i0, grp * epg + i1, p0 / den, p1 / den


def _sort_tile(e0, e1, hb):
    st = hb.shape[0]
    eid = lax.broadcasted_iota(jnp.int32, (N_EXPERTS, st), 0)
    sel0 = jnp.where(eid == e0, 1.0, 0.0)
    sel1 = jnp.where(eid == e1, 1.0, 0.0)
    sel = sel0 + sel1
    before = (lax.broadcasted_iota(jnp.int32, (st, st), 0)
              < lax.broadcasted_iota(jnp.int32, (st, st), 1))
    rank = jnp.dot(sel.astype(BF16), jnp.where(before, 1.0, 0.0).astype(BF16),
                   preferred_element_type=F32)
    cnt = jnp.sum(sel, axis=1, keepdims=True)
    padded = jnp.ceil(cnt * (1.0 / PAGE)) * PAGE
    ecol = lax.broadcasted_iota(jnp.int32, (N_EXPERTS, 1), 0)
    seg = jnp.zeros((N_EXPERTS, 1), F32)
    run = jnp.zeros((1, 1), F32)
    for e in range(N_EXPERTS):
        seg = jnp.where(ecol == e, run, seg)
        run = run + padded[e:e + 1, :]
    base = seg + rank
    pos0 = jnp.sum(sel0 * base, axis=0, keepdims=True)
    pos1 = jnp.sum(sel1 * base, axis=0, keepdims=True)
    slot = lax.broadcasted_iota(jnp.int32, (SLOTS, st), 0)
    perm = (jnp.where(slot == pos0.astype(jnp.int32), 1.0, 0.0)
            + jnp.where(slot == pos1.astype(jnp.int32), 1.0, 0.0)).astype(BF16)
    rows = jnp.dot(perm, hb, preferred_element_type=F32).astype(BF16)
    return pos0, pos1, cnt, rows


def _outproj_kernel(a_ref, b_ref, x_ref, mod_ref, g_ref, w_ref, rw_ref, rb_ref,
                    xo_ref, xs_ref, meta_ref, cnt_ref):
    ka = a_ref.shape[1]
    for c in range(x_ref.shape[0] // SORT_TILE):
        ts = slice(c * SORT_TILE, (c + 1) * SORT_TILE)
        y = (jnp.dot(a_ref[ts, :], w_ref[0:ka, :], preferred_element_type=F32)
             + jnp.dot(b_ref[ts, :], w_ref[ka:, :], preferred_element_type=F32))
        xn = x_ref[ts, :] + mod_ref[2:3, :] * y
        xo_ref[ts, :] = xn
        h = _rms(xn, g_ref[...]) * (1.0 + mod_ref[4:5, :]) + mod_ref[3:4, :]
        hb = h.astype(BF16)
        hl = (h - hb.astype(F32)).astype(BF16)
        parts = (jnp.dot(hb, rw_ref[...], preferred_element_type=F32)
                 + jnp.dot(hl, rw_ref[...], preferred_element_type=F32)).T
        lt = parts[0:N_EXPERTS, :] + parts[N_EXPERTS:2 * N_EXPERTS, :]
        e0, e1, g0, g1 = _route(lt, rb_ref[...])
        pos0, pos1, cnt, rows = _sort_tile(e0, e1, hb)
        xs_ref[c * SLOTS:(c + 1) * SLOTS, :] = rows
        meta_ref[c] = jnp.concatenate(
            [pos0, pos1, g0, g1, jnp.zeros((META_ROWS - 4, SORT_TILE), F32)], axis=0)
        cnt_ref[c] = jnp.broadcast_to(cnt, (N_EXPERTS, LANES))


def _out_projection(a, b, x2, mod_l, g, w_bf, rw_split, rb, seq, tm=1024):
    t, d = x2.shape
    ka, kb = a.shape[1], b.shape[1]
    per_b = seq // tm
    sub = tm // SORT_TILE
    n_tiles = t // SORT_TILE
    return pl.pallas_call(
        _outproj_kernel,
        out_shape=(jax.ShapeDtypeStruct((t, d), F32),
                   jax.ShapeDtypeStruct((n_tiles * SLOTS, d), BF16),
                   jax.ShapeDtypeStruct((n_tiles, META_ROWS, SORT_TILE), F32),
                   jax.ShapeDtypeStruct((n_tiles, N_EXPERTS, LANES), F32)),
        grid=(t // tm,),
        in_specs=[
            pl.BlockSpec((tm, ka), lambda i: (i, 0)),
            pl.BlockSpec((tm, kb), lambda i: (i, 0)),
            pl.BlockSpec((tm, d), lambda i: (i, 0)),
            pl.BlockSpec((None, 6, d), lambda i: (i // per_b, 0, 0)),
            pl.BlockSpec((1, d), lambda i: (0, 0)),
            pl.BlockSpec((ka + kb, d), lambda i: (0, 0)),
            pl.BlockSpec((d, LANES), lambda i: (0, 0)),
            pl.BlockSpec((N_EXPERTS, 1), lambda i: (0, 0)),
        ],
        out_specs=(pl.BlockSpec((tm, d), lambda i: (i, 0)),
                   pl.BlockSpec((sub * SLOTS, d), lambda i: (i, 0)),
                   pl.BlockSpec((sub, META_ROWS, SORT_TILE), lambda i: (i, 0, 0)),
                   pl.BlockSpec((sub, N_EXPERTS, LANES), lambda i: (i, 0, 0))),
        compiler_params=_params(("parallel",)),
        name="out_projection",
    )(a, b, x2, mod_l, g.reshape(1, d), w_bf, rw_split, rb.reshape(N_EXPERTS, 1))


def _expert_plan(counts, n_steps_max):
    n_tiles = counts.shape[0]
    n_groups = N_EXPERTS + 1
    i32 = jnp.int32
    npg = (counts + PAGE - 1) // PAGE
    npg = jnp.concatenate([npg, PAGES_PER_TILE - jnp.sum(npg, axis=1, keepdims=True)], axis=1)
    seg = jnp.cumsum(npg, axis=1) - npg
    page_base = jnp.arange(n_tiles, dtype=i32)[:, None] * PAGES_PER_TILE + seg
    cum_t = jnp.cumsum(npg, axis=0)
    tot = cum_t[-1]
    steps_e = (tot + STEP_PAGES - 1) // STEP_PAGES
    step_end = jnp.cumsum(steps_e)
    n_live = step_end[-1]
    g = jnp.arange(n_steps_max, dtype=i32)
    ex = jnp.minimum(jnp.sum((step_end[:, None] <= g[None, :]).astype(i32), axis=0),
                     n_groups - 1)
    sel = jnp.arange(n_groups, dtype=i32)[:, None] == ex[None, :]

    def of_step(per_group):
        return jnp.sum(jnp.where(sel, per_group[:, None], 0), axis=0)

    def rows_of_step(per_tile_group):
        return jnp.sum(jnp.where(sel[None], per_tile_group[:, :, None], 0), axis=1)

    q = ((g - of_step(step_end - steps_e))[:, None] * STEP_PAGES
         + jnp.arange(STEP_PAGES, dtype=i32)[None, :])
    valid = jnp.logical_and(q < of_step(tot)[:, None], (g < n_live)[:, None])
    cum_e = rows_of_step(cum_t)
    ti = jnp.minimum(jnp.sum((cum_e[:, :, None] <= q[None]).astype(i32), axis=0), n_tiles - 1)
    shift = rows_of_step(page_base - (cum_t - npg))
    hit = jnp.arange(n_tiles, dtype=i32)[:, None, None] == ti[None]
    pid = q + jnp.sum(jnp.where(hit, shift[:, :, None], 0), axis=0)
    spare = (n_tiles * PAGES_PER_TILE + (g % 2)[:, None] * STEP_PAGES
             + jnp.arange(STEP_PAGES, dtype=i32)[None, :])
    first = jnp.where(g < n_live, pid[:, 0], pid[0, 0])[:, None]
    in_pages = jnp.where(valid, pid, first).astype(i32).reshape(-1)
    out_pages = jnp.where(valid, pid, spare).astype(i32).reshape(-1)
    return in_pages, out_pages, ex.astype(i32), n_live.astype(i32).reshape(1)


def _expert_kernel(inp_ref, outp_ref, ex_ref, live_ref, xs_ref, wgu_ref, wd_ref, ys_ref,
                   wgu_bf, wd_bf, xbuf, ybuf, zbuf, in_sems, out_sems):
    g = pl.program_id(0)
    live = live_ref[0]
    slot = g % 2
    ff = wd_ref.shape[0]
    n_spare = 2 * STEP_PAGES
    half_rows = STEP_PAGES * PAGE // 2

    def is_expert_step(step):
        group = ex_ref[jnp.minimum(step, pl.num_programs(0) - 1)]
        return jnp.logical_and(step < live, group < N_EXPERTS)

    def in_copies(step, buf):
        return [pltpu.make_async_copy(xs_ref.at[inp_ref[step * STEP_PAGES + k]],
                                      xbuf.at[buf, pl.ds(k * PAGE, PAGE), :], in_sems.at[buf])
                for k in range(STEP_PAGES)]

    def out_copies(step, pages, zeros=False):
        return [pltpu.make_async_copy(
            zbuf.at[k] if zeros else ybuf.at[slot, pl.ds(k * PAGE, PAGE), :],
            ys_ref.at[outp_ref[step * STEP_PAGES + k]], out_sems.at[slot])
                for k in pages]

    @pl.when(g == 0)
    def _():
        zbuf[...] = jnp.zeros_like(zbuf)
        for j in range(n_spare // STEP_PAGES):
            first = ys_ref.shape[0] - n_spare + j * STEP_PAGES
            cp = pltpu.make_async_copy(zbuf, ys_ref.at[pl.ds(first, STEP_PAGES)], out_sems.at[1])
            cp.start()
            cp.wait()

        @pl.when(is_expert_step(0))
        def _():
            for cp in in_copies(0, 0):
                cp.start()

    @pl.when(jnp.logical_and(g >= 2, g - 2 < live))
    def _():
        for cp in out_copies(g - 2, range(STEP_PAGES)):
            cp.wait()

    group = ex_ref[g]
    is_expert = is_expert_step(g)
    new_expert = jnp.logical_or(g == 0, group != ex_ref[jnp.maximum(g - 1, 0)])

    @pl.when(jnp.logical_and(new_expert, is_expert))
    def _():
        wgu_bf[...] = wgu_ref[...].astype(BF16)
        wd_bf[...] = wd_ref[...].astype(BF16)

    @pl.when(is_expert)
    def _():
        for cp in in_copies(g, slot):
            cp.wait()

        @pl.when(is_expert_step(g + 1))
        def _():
            for cp in in_copies(g + 1, 1 - slot):
                cp.start()

        gu = jnp.dot(xbuf[slot], wgu_bf[...], preferred_element_type=F32)
        act = (_silu(gu[:, :ff]) * gu[:, ff:]).astype(BF16)
        for hlf in range(2):
            rows = slice(hlf * half_rows, (hlf + 1) * half_rows)
            ybuf[slot, rows, :] = jnp.dot(act[rows, :], wd_bf[...],
                                          preferred_element_type=F32).astype(BF16)
            for cp in out_copies(g, range(hlf * STEP_PAGES // 2, (hlf + 1) * STEP_PAGES // 2)):
                cp.start()

    @pl.when(jnp.logical_and(g < live, group == N_EXPERTS))
    def _():
        for cp in out_copies(g, range(STEP_PAGES), zeros=True):
            cp.start()


def _experts(xs_pages, plan, w_gu, w_down, layer, n_steps):
    n_pages, _, d = xs_pages.shape
    two_ff = w_gu.shape[-1]
    ff = two_ff // 2
    in_pages, out_pages, ex, live = plan
    grid_spec = pltpu.PrefetchScalarGridSpec(
        num_scalar_prefetch=4,
        grid=(n_steps,),
        in_specs=[
            pl.BlockSpec(memory_space=pl.ANY),
            pl.BlockSpec((None, None, d, two_ff),
                         lambda g, ip, op, e, nl: (layer, jnp.minimum(e[g], N_EXPERTS - 1), 0, 0)),
            pl.BlockSpec((None, None, ff, d),
                         lambda g, ip, op, e, nl: (layer, jnp.minimum(e[g], N_EXPERTS - 1), 0, 0)),
        ],
        out_specs=pl.BlockSpec(memory_space=pl.ANY),
        scratch_shapes=[pltpu.VMEM((d, two_ff), BF16), pltpu.VMEM((ff, d), BF16),
                        pltpu.VMEM((2, STEP_PAGES * PAGE, d), BF16),
                        pltpu.VMEM((2, STEP_PAGES * PAGE, d), BF16),
                        pltpu.VMEM((STEP_PAGES, PAGE, d), BF16),
                        pltpu.SemaphoreType.DMA((2,)),
                        pltpu.SemaphoreType.DMA((2,))],
    )
    return pl.pallas_call(
        _expert_kernel,
        out_shape=jax.ShapeDtypeStruct((n_pages + 2 * STEP_PAGES, PAGE, d), BF16),
        grid_spec=grid_spec,
        compiler_params=_params(("arbitrary",)),
        name="experts",
    )(in_pages, out_pages, ex, live, xs_pages, w_gu, w_down)


def _combine_kernel(ys_ref, meta_ref, x_ref, mod_ref, fg_ref, o_ref):
    y = _unsort(ys_ref[...], meta_ref[...])
    o_ref[...] = _rms(x_ref[...] + mod_ref[5:6, :] * y, fg_ref[...])


def _final_combine(ys_rows, meta, x2, mod_l, final_g, seq):
    t, d = x2.shape
    per_b = seq // SORT_TILE
    return pl.pallas_call(
        _combine_kernel,
        out_shape=jax.ShapeDtypeStruct((t, d), F32),
        grid=(t // SORT_TILE,),
        in_specs=[
            pl.BlockSpec((SLOTS, d), lambda i: (i, 0)),
            pl.BlockSpec((None, META_ROWS, SORT_TILE), lambda i: (i, 0, 0)),
            pl.BlockSpec((SORT_TILE, d), lambda i: (i, 0)),
            pl.BlockSpec((None, 6, d), lambda i: (i // per_b, 0, 0)),
            pl.BlockSpec((1, d), lambda i: (0, 0)),
        ],
        out_specs=pl.BlockSpec((SORT_TILE, d), lambda i: (i, 0)),
        compiler_params=_params(("parallel",)),
        name="moe_combine_final_norm",
    )(ys_rows, meta, x2, mod_l, final_g.reshape(1, d))


def _moe_experts(xs, counts, w_gu, w_down, layer):
    d = xs.shape[1]
    n_tiles = counts.shape[0]
    n_pages = n_tiles * PAGES_PER_TILE
    n_steps = n_pages // STEP_PAGES + N_EXPERTS + 1 + 2
    plan = _expert_plan(counts[:, :, 0].astype(jnp.int32), n_steps)
    ys = _experts(xs.reshape(n_pages, PAGE, d), plan, w_gu, w_down, layer, n_steps)
    return ys.reshape(-1, d)


def kernel(x, c, norm1_g, norm2_g, ada_w, ada_b, ev_w_in, ev_w_out, sgu_ln_g, sgu_ln_b, sgu_w, sgu_b, lam_q1, lam_k1, lam_q2, lam_k2, diff_subln_g, od_w_in, od_w_out, conv_w, conv_b, conv_ln_g, conv_ln_b, router_w, router_b, moe_w_gu, moe_w_down, final_g):
    bsz, seq, d = x.shape
    depth = ada_w.shape[0]
    half = d // 2
    diff_heads = half // (2 * HEAD_DIM)
    dil_heads = half // HEAD_DIM
    t = bsz * seq

    mod = _modulation(c, ada_w, ada_b)
    rw_hi = router_w.astype(BF16)
    rw_lo = (router_w - rw_hi.astype(F32)).astype(BF16)
    rw_split = jnp.pad(jnp.concatenate([rw_hi, rw_lo], axis=1),
                       ((0, 0), (0, LANES - 2 * N_EXPERTS)))

    x2 = x.reshape(t, d)
    moe = None
    for l in range(depth):
        i = l // 2
        w_in = (ev_w_in if l % 2 == 0 else od_w_in)[i].astype(BF16)
        x2, proj = _in_projection(x2, mod[l], norm1_g[l], w_in, seq, moe)
        if l % 2 == 0:
            lam_init = 0.8 - 0.6 * math.exp(-0.3 * l)
            part_a = _spatial_gating(proj, sgu_ln_g[i], sgu_ln_b[i], sgu_w[i], sgu_b[i], half)
            part_b = _diff_attention(proj.reshape(bsz, seq, -1), lam_q1[i], lam_k1[i], lam_q2[i],
                                     lam_k2[i], diff_subln_g[i], lam_init, diff_heads, 2 * half)
            w_out = ev_w_out[i]
        else:
            proj3 = proj.reshape(bsz, seq, -1)
            part_a = _dilated_attention(proj3, dil_heads).reshape(t, half)
            part_b = _conv_module(proj3, conv_w[i], conv_b[i], conv_ln_g[i], conv_ln_b[i],
                                  (2 * len(DIL_PAIRS) + 1) * half)
            w_out = od_w_out[i]
        x2, xs, meta, counts = _out_projection(part_a, part_b.reshape(t, half), x2, mod[l],
                                               norm2_g[l], w_out.astype(BF16), rw_split,
                                               router_b, seq)
        moe = (_moe_experts(xs, counts, moe_w_gu, moe_w_down, l), meta, mod[l])
    out = _final_combine(moe[0], moe[1], x2, mod[depth - 1], final_g, seq)
    return out.reshape(bsz, seq, d)
```

```python
import functools
import math

import jax
import jax.numpy as jnp
from jax import lax
from jax.experimental import pallas as pl
from jax.experimental.pallas import tpu as pltpu

F32 = jnp.float32
BF16 = jnp.bfloat16

HEAD_DIM = 64
LANES = 128
CHUNK = 128
SGU_GROUPS = 4
DIL_PAIRS = ((128, 1), (512, 4), (2048, 16))
WIN_BLOCK = 128
CONV_K = 31
CONV_SUBLANES = 8
N_EXPERTS = 16
EXPERTS_PER_GROUP = 4
RMS_EPS = 1e-6
LN_EPS = 1e-5
NEG = -1e30
LOG2E = 1.4426950408889634
VMEM_LIMIT = 56 * 1024 * 1024


def _params(sem):
    return pltpu.CompilerParams(dimension_semantics=sem, vmem_limit_bytes=VMEM_LIMIT)


def _gelu(x):
    return x * (0.5 * (1.0 + jnp.tanh(0.7978845608028654 * (x + 0.044715 * (x * x * x)))))


def _silu(x):
    return x * jax.nn.sigmoid(x)


def _rms(x, g):
    ms = jnp.mean(x * x, axis=-1, keepdims=True)
    return x * lax.rsqrt(ms + RMS_EPS) * g


def _layer_norm(x, g, b):
    mu = jnp.mean(x, axis=-1, keepdims=True)
    xc = x - mu
    var = jnp.mean(xc * xc, axis=-1, keepdims=True)
    return xc * lax.rsqrt(var + LN_EPS) * g + b


def _mod_kernel(c_ref, w_ref, b_ref, o_ref):
    ca = _silu(c_ref[...]).astype(BF16)
    o_ref[...] = jnp.dot(ca, w_ref[...].astype(BF16), preferred_element_type=F32) + b_ref[...]


def _modulation(c, ada_w, ada_b):
    depth, d, six_d = ada_w.shape
    bsz = c.shape[0]
    n = six_d // d
    out = pl.pallas_call(
        _mod_kernel,
        out_shape=jax.ShapeDtypeStruct((depth, bsz, six_d), F32),
        grid=(depth, n),
        in_specs=[
            pl.BlockSpec((bsz, d), lambda l, j: (0, 0)),
            pl.BlockSpec((None, d, d), lambda l, j: (l, 0, j)),
            pl.BlockSpec((None, 1, d), lambda l, j: (l, 0, j)),
        ],
        out_specs=pl.BlockSpec((None, bsz, d), lambda l, j: (l, 0, j)),
        compiler_params=_params(("parallel", "parallel")),
        name="modulation",
    )(c, ada_w, ada_b.reshape(depth, 1, six_d))
    return out.reshape(depth, bsz, n, d)


SORT_TILE = 256
PAGE = 16
TOP_K = 2
SLOTS = TOP_K * SORT_TILE + N_EXPERTS * PAGE
PAGES_PER_TILE = SLOTS // PAGE
STEP_PAGES = 32
META_ROWS = 8


def _unsort(ys, meta):
    pos0 = meta[0:1, :].astype(jnp.int32)
    pos1 = meta[1:2, :].astype(jnp.int32)
    slot = lax.broadcasted_iota(jnp.int32, (SLOTS, meta.shape[1]), 0)
    w = (jnp.where(slot == pos0, meta[2:3, :], 0.0)
         + jnp.where(slot == pos1, meta[3:4, :], 0.0)).astype(BF16)
    return lax.dot_general(w, ys, (((0,), (0,)), ((), ())), preferred_element_type=F32)


def _project(x, mod_ref, g_ref, w_ref, o_ref, n_chunk):
    h = _rms(x, g_ref[...]) * (1.0 + mod_ref[1:2, :]) + mod_ref[0:1, :]
    hb = h.astype(BF16)
    for j in range(0, o_ref.shape[-1], n_chunk):
        o_ref[:, j:j + n_chunk] = jnp.dot(
            hb, w_ref[:, j:j + n_chunk], preferred_element_type=F32).astype(o_ref.dtype)


def _inproj_kernel(x_ref, mod_ref, g_ref, w_ref, o_ref, *, n_chunk):
    _project(x_ref[...], mod_ref, g_ref, w_ref, o_ref, n_chunk)


def _inproj_moe_kernel(ys_ref, meta_ref, x_ref, pmod_ref, mod_ref, g_ref, w_ref, xo_ref, o_ref,
                       *, n_chunk):
    for c in range(meta_ref.shape[0]):
        ts = slice(c * SORT_TILE, (c + 1) * SORT_TILE)
        y = _unsort(ys_ref[c * SLOTS:(c + 1) * SLOTS, :], meta_ref[c])
        xo_ref[ts, :] = x_ref[ts, :] + pmod_ref[5:6, :] * y
    _project(xo_ref[...], mod_ref, g_ref, w_ref, o_ref, n_chunk)


def _in_projection(x2, mod_l, g, w_bf, seq, moe=None, tm=512):
    t, d = x2.shape
    n = w_bf.shape[1]
    per_b = seq // tm
    sub = tm // SORT_TILE
    row_spec = pl.BlockSpec((tm, d), lambda i: (i, 0))
    mod_spec = pl.BlockSpec((None, 6, d), lambda i: (i // per_b, 0, 0))
    tail_specs = [mod_spec, pl.BlockSpec((1, d), lambda i: (0, 0)),
                  pl.BlockSpec((d, n), lambda i: (0, 0))]
    proj_shape = jax.ShapeDtypeStruct((t, n), BF16)
    proj_spec = pl.BlockSpec((tm, n), lambda i: (i, 0))
    if moe is None:
        return x2, pl.pallas_call(
            functools.partial(_inproj_kernel, n_chunk=512),
            out_shape=proj_shape,
            grid=(t // tm,),
            in_specs=[row_spec] + tail_specs,
            out_specs=proj_spec,
            compiler_params=_params(("parallel",)),
            name="in_projection",
        )(x2, mod_l, g.reshape(1, d), w_bf)
    ys_rows, meta, prev_mod = moe
    return pl.pallas_call(
        functools.partial(_inproj_moe_kernel, n_chunk=512),
        out_shape=(jax.ShapeDtypeStruct((t, d), F32), proj_shape),
        grid=(t // tm,),
        in_specs=[pl.BlockSpec((sub * SLOTS, d), lambda i: (i, 0)),
                  pl.BlockSpec((sub, META_ROWS, SORT_TILE), lambda i: (i, 0, 0)),
                  row_spec, mod_spec] + tail_specs,
        out_specs=(row_spec, proj_spec),
        compiler_params=_params(("parallel",)),
        name="moe_combine_in_projection",
    )(ys_rows, meta, x2, prev_mod, mod_l, g.reshape(1, d), w_bf)


def _sgu_kernel(u_ref, v_ref, lng_ref, lnb_ref, w_ref, bias_ref, o_ref):
    u = _gelu(u_ref[...].astype(F32))
    v = _gelu(v_ref[...].astype(F32))
    vb = _layer_norm(v, lng_ref[...], lnb_ref[...]).astype(BF16)
    row = lax.broadcasted_iota(jnp.int32, (CHUNK, CHUNK), 0)
    col = lax.broadcasted_iota(jnp.int32, (CHUNK, CHUNK), 1)
    causal = col <= row
    gd = u.shape[1] // SGU_GROUPS
    n_chunks = u.shape[0] // CHUNK
    for g in range(SGU_GROUPS):
        w = jnp.where(causal, w_ref[g], 0.0).astype(BF16)
        cs = slice(g * gd, (g + 1) * gd)
        rhs = jnp.concatenate([vb[c * CHUNK:(c + 1) * CHUNK, cs] for c in range(n_chunks)], axis=1)
        mixed = jnp.dot(w, rhs, preferred_element_type=F32)
        for c in range(n_chunks):
            rs = slice(c * CHUNK, (c + 1) * CHUNK)
            o_ref[rs, cs] = (u[rs, cs] * (mixed[:, c * gd:(c + 1) * gd] + bias_ref[:, cs])
                             ).astype(o_ref.dtype)


def _spatial_gating(proj, ln_g, ln_b, w_s, b_s, width, tm=1024):
    t = proj.shape[0]
    gd = width // SGU_GROUPS
    bias = jnp.repeat(b_s.T, gd, axis=1)
    return pl.pallas_call(
        _sgu_kernel,
        out_shape=jax.ShapeDtypeStruct((t, width), BF16),
        grid=(t // tm,),
        in_specs=[
            pl.BlockSpec((tm, width), lambda i: (i, 0)),
            pl.BlockSpec((tm, width), lambda i: (i, 1)),
            pl.BlockSpec((1, width), lambda i: (0, 0)),
            pl.BlockSpec((1, width), lambda i: (0, 0)),
            pl.BlockSpec((SGU_GROUPS, CHUNK, CHUNK), lambda i: (0, 0, 0)),
            pl.BlockSpec((CHUNK, width), lambda i: (0, 0)),
        ],
        out_specs=pl.BlockSpec((tm, width), lambda i: (i, 0)),
        compiler_params=_params(("parallel",)),
        name="spatial_gating",
    )(proj, proj, ln_g.reshape(1, width), ln_b.reshape(1, width), w_s, bias)


ONES_ROWS = 16


def _split3(x):
    hi = x.astype(BF16).astype(F32)
    r1 = x - hi
    mid = r1.astype(BF16).astype(F32)
    lo = (r1 - mid).astype(BF16).astype(F32)
    return hi, mid, lo


def _diff_kernel(slopes_ref, lq1_ref, lk1_ref, lq2_ref, lk2_ref, q_ref, k_ref, v_ref, g_ref,
                 o_ref, kx, qx, vt, acc, *, tq, lam_init):
    h = pl.program_id(1)
    seq, vd = v_ref.shape
    n_q = seq // tq
    hd = HEAD_DIM
    lane = lax.broadcasted_iota(jnp.int32, (tq, LANES), 1)

    def place(pieces, base, fill):
        out = fill
        for n, piece in enumerate(pieces):
            out = jnp.where(lane == base + n, piece, out)
        return out

    slope = slopes_ref[h] * LOG2E
    one = jnp.ones((tq, LANES), F32)
    for c in range(n_q):
        rs = slice(c * tq, (c + 1) * tq)
        k = k_ref[rs, :].astype(F32)
        pos = (lax.broadcasted_iota(jnp.int32, (tq, LANES), 0) + c * tq).astype(F32) * slope
        pieces = _split3(pos)
        zero = jnp.zeros_like(k)
        kx[0, rs, :] = jnp.where(lane < hd, k, place(pieces, hd, zero)).astype(BF16)
        kx[1, rs, :] = jnp.where(lane >= hd, k, place(pieces, 0, zero)).astype(BF16)
        vt[0:vd, rs] = v_ref[rs, :].astype(F32).T.astype(BF16)
        vt[vd:, rs] = jnp.ones((ONES_ROWS, tq), BF16)
        q = q_ref[rs, :].astype(F32) * (hd ** -0.5 * LOG2E)
        qx[0, rs, :] = jnp.where(lane < hd, q, place((one, one, one), hd, zero)).astype(BF16)
        qx[1, rs, :] = jnp.where(lane >= hd, q, place((one, one, one), 0, zero)).astype(BF16)

    kc = tq
    chunks = {qi: [(k0, min(k0 + kc, (qi + 1) * tq)) for k0 in range(0, (qi + 1) * tq, kc)]
              for qi in range(n_q)}
    ms = {}
    for c in range(max(len(v) for v in chunks.values())):
        for qi in range(n_q):
            if c >= len(chunks[qi]):
                continue
            k0, k1 = chunks[qi][c]
            for m in range(2):
                s = lax.dot_general(kx[m, k0:k1, :], qx[m, qi * tq:(qi + 1) * tq, :],
                                    (((1,), (1,)), ((), ())),
                                    preferred_element_type=F32)
                if k1 > qi * tq + 1:
                    key_i = lax.broadcasted_iota(jnp.int32, s.shape, 0) + k0
                    qry_i = lax.broadcasted_iota(jnp.int32, s.shape, 1) + qi * tq
                    s = jnp.where(key_i <= qry_i, s, NEG)
                m_tile = jnp.max(s, axis=0, keepdims=True)
                if c == 0:
                    ms[qi, m] = m_tile
                    p = jnp.exp2(s - m_tile).astype(BF16)
                    acc[qi, m] = jnp.dot(vt[:, k0:k1], p, preferred_element_type=F32)
                else:
                    m_new = jnp.maximum(ms[qi, m], m_tile)
                    alpha = jnp.exp2(ms[qi, m] - m_new)
                    ms[qi, m] = m_new
                    p = jnp.exp2(s - m_new).astype(BF16)
                    acc[qi, m] = (alpha * acc[qi, m]
                                  + jnp.dot(vt[:, k0:k1], p, preferred_element_type=F32))

    lam = (jnp.exp(jnp.sum(lq1_ref[...] * lk1_ref[...], axis=-1, keepdims=True))
           - jnp.exp(jnp.sum(lq2_ref[...] * lk2_ref[...], axis=-1, keepdims=True)) + lam_init)
    for qi in range(n_q):
        o_t = (acc[qi, 0, 0:vd, :] / acc[qi, 0, vd:vd + 1, :]
               - lam * (acc[qi, 1, 0:vd, :] / acc[qi, 1, vd:vd + 1, :]))
        o_ref[qi * tq:(qi + 1) * tq, :] = (
            _rms(o_t.T, g_ref[...]) * (1.0 - lam_init)).astype(o_ref.dtype)


def _diff_attention(proj3, lq1, lk1, lq2, lk2, subln_g, lam_init, n_heads, col0, tq=1024):
    bsz, seq, _ = proj3.shape
    slopes = 2.0 ** (-8.0 * jnp.arange(1, n_heads + 1, dtype=F32) / n_heads)
    qb, kb, vb = col0 // LANES, col0 // LANES + n_heads, col0 // LANES + 2 * n_heads
    vec = lambda a: a.reshape(1, -1)
    small = lambda n: pl.BlockSpec((1, n), lambda b, h: (0, 0))
    return pl.pallas_call(
        functools.partial(_diff_kernel, tq=tq, lam_init=lam_init),
        out_shape=jax.ShapeDtypeStruct((bsz, seq, n_heads * LANES), BF16),
        grid=(bsz, n_heads),
        in_specs=[
            pl.BlockSpec(memory_space=pltpu.SMEM),
            small(HEAD_DIM), small(HEAD_DIM), small(HEAD_DIM), small(HEAD_DIM),
            pl.BlockSpec((None, seq, LANES), lambda b, h: (b, 0, qb + h)),
            pl.BlockSpec((None, seq, LANES), lambda b, h: (b, 0, kb + h)),
            pl.BlockSpec((None, seq, LANES), lambda b, h: (b, 0, vb + h)),
            small(LANES),
        ],
        out_specs=pl.BlockSpec((None, seq, LANES), lambda b, h: (b, 0, h)),
        scratch_shapes=[
            pltpu.VMEM((2, seq, LANES), BF16),
            pltpu.VMEM((2, seq, LANES), BF16),
            pltpu.VMEM((LANES + ONES_ROWS, seq), BF16),
            pltpu.VMEM((seq // tq, 2, LANES + ONES_ROWS, tq), F32),
        ],
        compiler_params=_params(("parallel", "parallel")),
        name="diff_attention",
    )(slopes, vec(lq1), vec(lk1), vec(lq2), vec(lk2), proj3, proj3, proj3, vec(subln_g))


def _dil_kernel(slopes_ref, q1_ref, q2_ref, q3_ref, k1_ref, k2_ref, k3_ref, v_ref, o_ref,
                qf, kf, vf, bias_s, acc_s, m_s, l_s, *, seq):
    hp = pl.program_id(1)
    wb = WIN_BLOCK
    hd = HEAD_DIM
    n_q_blocks = seq // wb

    qf[0] = q2_ref[...].astype(F32)
    qf[1] = q3_ref[...].astype(F32)
    kf[0] = k2_ref[...].astype(F32)
    kf[1] = k3_ref[...].astype(F32)
    vf[...] = v_ref[...].astype(F32)

    row = lax.broadcasted_iota(jnp.int32, (2 * wb, 2 * wb), 0)
    col = lax.broadcasted_iota(jnp.int32, (2 * wb, 2 * wb), 1)
    qi_ = jnp.where(row < wb, row, row - wb)
    dist = qi_ + wb - col
    slope = jnp.where(row < wb, slopes_ref[2 * hp], slopes_ref[2 * hp + 1])
    valid = jnp.logical_and(dist >= 0, dist <= wb)
    table = jnp.where(valid, -(slope * LOG2E) * dist.astype(F32), NEG)
    bias_s[1] = table
    bias_s[0] = jnp.where(col < wb, NEG, table)

    first = lax.broadcasted_iota(jnp.int32, (wb, LANES), 1) < hd
    q_scale = hd ** -0.5 * LOG2E

    for g, (window, dil) in enumerate(DIL_PAIRS):
        assert window // dil == wb
        nb = n_q_blocks // dil
        refs = {"q": (q1_ref, qf), "k": (k1_ref, kf), "v": (v_ref, vf)}

        def rows(kind, start, g=g, dil=dil, refs=refs):
            direct, copies = refs[kind]
            if dil == 1:
                return direct[pl.ds(pl.multiple_of(start, wb), wb), :]
            src = copies if kind == "v" else copies.at[g - 1]
            return src[pl.ds(start, wb, stride=dil), :]

        def body(idx, _, g=g, dil=dil, nb=nb, rows=rows):
            r = idx // nb
            n = idx % nb
            cur = n * (wb * dil) + r
            q = rows("q", cur).astype(F32) * q_scale
            zero = jnp.zeros_like(q)
            qx = jnp.concatenate([jnp.where(first, q, zero), jnp.where(first, zero, q)],
                                 axis=0).astype(BF16)
            if nb > 1:
                prev = jnp.maximum(n - 1, 0) * (wb * dil) + r
                keys = jnp.concatenate([rows("k", prev), rows("k", cur)], axis=0).astype(BF16)
                vals = jnp.concatenate([rows("v", prev), rows("v", cur)], axis=0).astype(BF16)
                bias = bias_s[jnp.minimum(n, 1)]
            else:
                keys, vals = rows("k", cur).astype(BF16), rows("v", cur).astype(BF16)
                bias = bias_s[1, :, wb:]
            vx = jnp.concatenate([vals, jnp.ones_like(vals)], axis=1)
            s = lax.dot_general(qx, keys, (((1,), (1,)), ((), ())),
                                preferred_element_type=F32) + bias
            m = jnp.max(s, axis=-1, keepdims=True)
            p = jnp.exp2(s - m).astype(BF16)
            o = jnp.dot(p, vx, preferred_element_type=F32)
            acc = jnp.where(first, o[0:wb, 0:LANES], o[wb:, 0:LANES])
            den = jnp.where(first, o[0:wb, LANES:], o[wb:, LANES:])
            mx = jnp.where(first, jnp.broadcast_to(m[0:wb], (wb, LANES)),
                           jnp.broadcast_to(m[wb:], (wb, LANES)))
            if dil == 1:
                dst = pl.ds(pl.multiple_of(cur, wb), wb)
            else:
                dst = pl.ds(cur, wb, stride=dil)
            acc_s[g, dst, :] = acc
            m_s[g, dst, :] = mx
            l_s[g, dst, :] = den
            return 0

        lax.fori_loop(0, n_q_blocks, body, 0, unroll=16)

    mr = 256
    for c in range(seq // mr):
        rs = slice(c * mr, (c + 1) * mr)
        m1, m2, m3 = m_s[0, rs, :], m_s[1, rs, :], m_s[2, rs, :]
        m = jnp.maximum(jnp.maximum(m1, m2), m3)
        w1, w2, w3 = jnp.exp2(m1 - m), jnp.exp2(m2 - m), jnp.exp2(m3 - m)
        num = w1 * acc_s[0, rs, :] + w2 * acc_s[1, rs, :] + w3 * acc_s[2, rs, :]
        den = w1 * l_s[0, rs, :] + w2 * l_s[1, rs, :] + w3 * l_s[2, rs, :]
        o_ref[rs, :] = (num / den).astype(o_ref.dtype)


def _dilated_attention(proj3, n_heads):
    bsz, seq, _ = proj3.shape
    n_pairs = n_heads * HEAD_DIM // LANES
    slopes = 2.0 ** (-8.0 * jnp.arange(1, n_heads + 1, dtype=F32) / n_heads)
    groups = len(DIL_PAIRS)

    def spec(blk0):
        return pl.BlockSpec((None, seq, LANES), lambda b, p: (b, 0, blk0 + p))

    q_specs = [spec(g * n_pairs) for g in range(groups)]
    k_specs = [spec((groups + g) * n_pairs) for g in range(groups)]
    v_spec = spec(2 * groups * n_pairs)
    return pl.pallas_call(
        functools.partial(_dil_kernel, seq=seq),
        out_shape=jax.ShapeDtypeStruct((bsz, seq, n_pairs * LANES), BF16),
        grid=(bsz, n_pairs),
        in_specs=[pl.BlockSpec(memory_space=pltpu.SMEM)] + q_specs + k_specs + [v_spec],
        out_specs=pl.BlockSpec((None, seq, LANES), lambda b, p: (b, 0, p)),
        scratch_shapes=[
            pltpu.VMEM((groups - 1, seq, LANES), F32),
            pltpu.VMEM((groups - 1, seq, LANES), F32),
            pltpu.VMEM((seq, LANES), F32),
            pltpu.VMEM((2, 2 * WIN_BLOCK, 2 * WIN_BLOCK), F32),
            pltpu.VMEM((groups, seq, LANES), F32),
            pltpu.VMEM((groups, seq, LANES), F32),
            pltpu.VMEM((groups, seq, LANES), F32),
        ],
        compiler_params=_params(("parallel", "parallel")),
        name="dilated_attention",
    )(slopes, *([proj3] * 7))


def _conv_kernel(a_ref, gate_ref, w_ref, cb_ref, lng_ref, lnb_ref, o_ref, buf, win, *,
                 ts, halo, rc):
    @pl.when(pl.program_id(1) == 0)
    def _():
        buf[0:halo, :] = jnp.zeros((halo, buf.shape[1]), F32)

    buf[halo:halo + ts, :] = a_ref[...].astype(F32) * jax.nn.sigmoid(gate_ref[...].astype(F32))
    off = halo - (CONV_K - 1)
    sub = CONV_SUBLANES
    for c in range(ts // rc):
        acc = jnp.zeros((rc, buf.shape[1]), F32)
        for r in range(sub):
            span = rc + (CONV_K - 1 - r) // sub * sub
            win[r, 0:span, :] = buf[c * rc + off + r:c * rc + off + r + span, :]
            for k in range(r, CONV_K, sub):
                acc = acc + w_ref[k:k + 1, :] * win[r, k - r:k - r + rc, :]
        y = _layer_norm(acc + cb_ref[...], lng_ref[...], lnb_ref[...])
        o_ref[c * rc:(c + 1) * rc, :] = _silu(y).astype(o_ref.dtype)
    buf[0:halo, :] = buf[ts:ts + halo, :]


def _conv_module(proj3, conv_w, conv_b, ln_g, ln_b, col0, ts=256):
    bsz, seq, _ = proj3.shape
    width = conv_w.shape[1]
    halo = 32
    rc = 64
    vec = lambda a: a.reshape(1, width)
    small = pl.BlockSpec((1, width), lambda b, s: (0, 0))
    return pl.pallas_call(
        functools.partial(_conv_kernel, ts=ts, halo=halo, rc=rc),
        out_shape=jax.ShapeDtypeStruct((bsz, seq, width), BF16),
        grid=(bsz, seq // ts),
        in_specs=[
            pl.BlockSpec((None, ts, width), lambda b, s: (b, s, col0 // width)),
            pl.BlockSpec((None, ts, width), lambda b, s: (b, s, col0 // width + 1)),
            pl.BlockSpec((CONV_K, width), lambda b, s: (0, 0)),
            small, small, small,
        ],
        out_specs=pl.BlockSpec((None, ts, width), lambda b, s: (b, s, 0)),
        scratch_shapes=[pltpu.VMEM((ts + halo, width), F32),
                        pltpu.VMEM((CONV_SUBLANES, rc + halo, width), F32)],
        compiler_params=_params(("parallel", "arbitrary")),
        name="conv_module",
    )(proj3, proj3, conv_w, vec(conv_b), vec(ln_g), vec(ln_b))


def _route(logits_t, rb):
    mx = jnp.max(logits_t, axis=0, keepdims=True)
    ex = jnp.exp(logits_t - mx)
    probs = ex / jnp.sum(ex, axis=0, keepdims=True)
    sel = probs + rb
    srow = [sel[i:i + 1, :] for i in range(N_EXPERTS)]
    prow = [probs[i:i + 1, :] for i in range(N_EXPERTS)]
    epg = EXPERTS_PER_GROUP
    n_grp = N_EXPERTS // epg

    scores = []
    for g in range(n_grp):
        a, b, c, d = srow[g * epg:(g + 1) * epg]
        hi1, lo1 = jnp.maximum(a, b), jnp.minimum(a, b)
        hi2, lo2 = jnp.maximum(c, d), jnp.minimum(c, d)
        scores.append(jnp.maximum(hi1, hi2)
                      + jnp.maximum(jnp.minimum(hi1, hi2), jnp.maximum(lo1, lo2)))
    best = scores[0]
    grp = jnp.zeros(best.shape, jnp.int32)
    for g in range(1, n_grp):
        better = scores[g] > best
        grp = jnp.where(better, g, grp)
        best = jnp.where(better, scores[g], best)

    def pick(rows_, j):
        out = rows_[j]
        for g in range(1, n_grp):
            out = jnp.where(grp == g, rows_[g * epg + j], out)
        return out

    ing = [pick(srow, j) for j in range(epg)]
    ping = [pick(prow, j) for j in range(epg)]
    b0, i0, p0 = ing[0], jnp.zeros(best.shape, jnp.int32), ping[0]
    for j in range(1, epg):
        better = ing[j] > b0
        i0 = jnp.where(better, j, i0)
        p0 = jnp.where(better, ping[j], p0)
        b0 = jnp.where(better, ing[j], b0)
    b1 = jnp.full(best.shape, -jnp.inf, F32)
    i1 = jnp.zeros(best.shape, jnp.int32)
    p1 = jnp.zeros(best.shape, F32)
    for j in range(epg):
        better = jnp.logical_and(i0 != j, ing[j] > b1)
        i1 = jnp.where(better, j, i1)
        p1 = jnp.where(better, ping[j], p1)
        b1 = jnp.where(better, ing[j], b1)
    den = p0 + p1
    return grp * epg + i0, grp * epg + i1, p0 / den, p1 / den


def _sort_tile(e0, e1, hb):
    st = hb.shape[0]
    eid = lax.broadcasted_iota(jnp.int32, (N_EXPERTS, st), 0)
    sel0 = jnp.where(eid == e0, 1.0, 0.0)
    sel1 = jnp.where(eid == e1, 1.0, 0.0)
    sel = sel0 + sel1
    before = (lax.broadcasted_iota(jnp.int32, (st, st), 0)
              < lax.broadcasted_iota(jnp.int32, (st, st), 1))
    rank = jnp.dot(sel.astype(BF16), jnp.where(before, 1.0, 0.0).astype(BF16),
                   preferred_element_type=F32)
    cnt = jnp.sum(sel, axis=1, keepdims=True)
    padded = jnp.ceil(cnt * (1.0 / PAGE)) * PAGE
    ecol = lax.broadcasted_iota(jnp.int32, (N_EXPERTS, 1), 0)
    seg = jnp.zeros((N_EXPERTS, 1), F32)
    run = jnp.zeros((1, 1), F32)
    for e in range(N_EXPERTS):
        seg = jnp.where(ecol == e, run, seg)
        run = run + padded[e:e + 1, :]
    base = seg + rank
    pos0 = jnp.sum(sel0 * base, axis=0, keepdims=True)
    pos1 = jnp.sum(sel1 * base, axis=0, keepdims=True)
    slot = lax.broadcasted_iota(jnp.int32, (SLOTS, st), 0)
    perm = (jnp.where(slot == pos0.astype(jnp.int32), 1.0, 0.0)
            + jnp.where(slot == pos1.astype(jnp.int32), 1.0, 0.0)).astype(BF16)
    rows = jnp.dot(perm, hb, preferred_element_type=F32).astype(BF16)
    return pos0, pos1, cnt, rows


def _outproj_kernel(a_ref, b_ref, x_ref, mod_ref, g_ref, w_ref, rw_ref, rb_ref,
                    xo_ref, xs_ref, meta_ref, cnt_ref):
    ka = a_ref.shape[1]
    for c in range(x_ref.shape[0] // SORT_TILE):
        ts = slice(c * SORT_TILE, (c + 1) * SORT_TILE)
        y = (jnp.dot(a_ref[ts, :], w_ref[0:ka, :], preferred_element_type=F32)
             + jnp.dot(b_ref[ts, :], w_ref[ka:, :], preferred_element_type=F32))
        xn = x_ref[ts, :] + mod_ref[2:3, :] * y
        xo_ref[ts, :] = xn
        h = _rms(xn, g_ref[...]) * (1.0 + mod_ref[4:5, :]) + mod_ref[3:4, :]
        hb = h.astype(BF16)
        hl = (h - hb.astype(F32)).astype(BF16)
        parts = (jnp.dot(hb, rw_ref[...], preferred_element_type=F32)
                 + jnp.dot(hl, rw_ref[...], preferred_element_type=F32)).T
        lt = parts[0:N_EXPERTS, :] + parts[N_EXPERTS:2 * N_EXPERTS, :]
        e0, e1, g0, g1 = _route(lt, rb_ref[...])
        pos0, pos1, cnt, rows = _sort_tile(e0, e1, hb)
        xs_ref[c * SLOTS:(c + 1) * SLOTS, :] = rows
        meta_ref[c] = jnp.concatenate(
            [pos0, pos1, g0, g1, jnp.zeros((META_ROWS - 4, SORT_TILE), F32)], axis=0)
        cnt_ref[c] = jnp.broadcast_to(cnt, (N_EXPERTS, LANES))


def _out_projection(a, b, x2, mod_l, g, w_bf, rw_split, rb, seq, tm=1024):
    t, d = x2.shape
    ka, kb = a.shape[1], b.shape[1]
    per_b = seq // tm
    sub = tm // SORT_TILE
    n_tiles = t // SORT_TILE
    return pl.pallas_call(
        _outproj_kernel,
        out_shape=(jax.ShapeDtypeStruct((t, d), F32),
                   jax.ShapeDtypeStruct((n_tiles * SLOTS, d), BF16),
                   jax.ShapeDtypeStruct((n_tiles, META_ROWS, SORT_TILE), F32),
                   jax.ShapeDtypeStruct((n_tiles, N_EXPERTS, LANES), F32)),
        grid=(t // tm,),
        in_specs=[
            pl.BlockSpec((tm, ka), lambda i: (i, 0)),
            pl.BlockSpec((tm, kb), lambda i: (i, 0)),
            pl.BlockSpec((tm, d), lambda i: (i, 0)),
            pl.BlockSpec((None, 6, d), lambda i: (i // per_b, 0, 0)),
            pl.BlockSpec((1, d), lambda i: (0, 0)),
            pl.BlockSpec((ka + kb, d), lambda i: (0, 0)),
            pl.BlockSpec((d, LANES), lambda i: (0, 0)),
            pl.BlockSpec((N_EXPERTS, 1), lambda i: (0, 0)),
        ],
        out_specs=(pl.BlockSpec((tm, d), lambda i: (i, 0)),
                   pl.BlockSpec((sub * SLOTS, d), lambda i: (i, 0)),
                   pl.BlockSpec((sub, META_ROWS, SORT_TILE), lambda i: (i, 0, 0)),
                   pl.BlockSpec((sub, N_EXPERTS, LANES), lambda i: (i, 0, 0))),
        compiler_params=_params(("parallel",)),
        name="out_projection",
    )(a, b, x2, mod_l, g.reshape(1, d), w_bf, rw_split, rb.reshape(N_EXPERTS, 1))


def _expert_plan(counts, n_steps_max):
    n_tiles = counts.shape[0]
    n_groups = N_EXPERTS + 1
    i32 = jnp.int32
    npg = (counts + PAGE - 1) // PAGE
    npg = jnp.concatenate([npg, PAGES_PER_TILE - jnp.sum(npg, axis=1, keepdims=True)], axis=1)
    seg = jnp.cumsum(npg, axis=1) - npg
    page_base = jnp.arange(n_tiles, dtype=i32)[:, None] * PAGES_PER_TILE + seg
    cum_t = jnp.cumsum(npg, axis=0)
    tot = cum_t[-1]
    steps_e = (tot + STEP_PAGES - 1) // STEP_PAGES
    step_end = jnp.cumsum(steps_e)
    n_live = step_end[-1]
    g = jnp.arange(n_steps_max, dtype=i32)
    ex = jnp.minimum(jnp.sum((step_end[:, None] <= g[None, :]).astype(i32), axis=0),
                     n_groups - 1)
    sel = jnp.arange(n_groups, dtype=i32)[:, None] == ex[None, :]

    def of_step(per_group):
        return jnp.sum(jnp.where(sel, per_group[:, None], 0), axis=0)

    def rows_of_step(per_tile_group):
        return jnp.sum(jnp.where(sel[None], per_tile_group[:, :, None], 0), axis=1)

    q = ((g - of_step(step_end - steps_e))[:, None] * STEP_PAGES
         + jnp.arange(STEP_PAGES, dtype=i32)[None, :])
    valid = jnp.logical_and(q < of_step(tot)[:, None], (g < n_live)[:, None])
    cum_e = rows_of_step(cum_t)
    ti = jnp.minimum(jnp.sum((cum_e[:, :, None] <= q[None]).astype(i32), axis=0), n_tiles - 1)
    shift = rows_of_step(page_base - (cum_t - npg))
    hit = jnp.arange(n_tiles, dtype=i32)[:, None, None] == ti[None]
    pid = q + jnp.sum(jnp.where(hit, shift[:, :, None], 0), axis=0)
    spare = (n_tiles * PAGES_PER_TILE + (g % 2)[:, None] * STEP_PAGES
             + jnp.arange(STEP_PAGES, dtype=i32)[None, :])
    first = jnp.where(g < n_live, pid[:, 0], pid[0, 0])[:, None]
    in_pages = jnp.where(valid, pid, first).astype(i32).reshape(-1)
    out_pages = jnp.where(valid, pid, spare).astype(i32).reshape(-1)
    return in_pages, out_pages, ex.astype(i32), n_live.astype(i32).reshape(1)


def _expert_kernel(inp_ref, outp_ref, ex_ref, live_ref, xs_ref, wgu_ref, wd_ref, ys_ref,
                   wgu_bf, wd_bf, xbuf, ybuf, zbuf, in_sems, out_sems):
    g = pl.program_id(0)
    live = live_ref[0]
    slot = g % 2
    ff = wd_ref.shape[0]
    n_spare = 2 * STEP_PAGES
    half_rows = STEP_PAGES * PAGE // 2

    def is_expert_step(step):
        group = ex_ref[jnp.minimum(step, pl.num_programs(0) - 1)]
        return jnp.logical_and(step < live, group < N_EXPERTS)

    def in_copies(step, buf):
        return [pltpu.make_async_copy(xs_ref.at[inp_ref[step * STEP_PAGES + k]],
                                      xbuf.at[buf, pl.ds(k * PAGE, PAGE), :], in_sems.at[buf])
                for k in range(STEP_PAGES)]

    def out_copies(step, pages, zeros=False):
        return [pltpu.make_async_copy(
            zbuf.at[k] if zeros else ybuf.at[slot, pl.ds(k * PAGE, PAGE), :],
            ys_ref.at[outp_ref[step * STEP_PAGES + k]], out_sems.at[slot])
                for k in pages]

    @pl.when(g == 0)
    def _():
        zbuf[...] = jnp.zeros_like(zbuf)
        for j in range(n_spare // STEP_PAGES):
            first = ys_ref.shape[0] - n_spare + j * STEP_PAGES
            cp = pltpu.make_async_copy(zbuf, ys_ref.at[pl.ds(first, STEP_PAGES)], out_sems.at[1])
            cp.start()
            cp.wait()

        @pl.when(is_expert_step(0))
        def _():
            for cp in in_copies(0, 0):
                cp.start()

    @pl.when(jnp.logical_and(g >= 2, g - 2 < live))
    def _():
        for cp in out_copies(g - 2, range(STEP_PAGES)):
            cp.wait()

    group = ex_ref[g]
    is_expert = is_expert_step(g)
    new_expert = jnp.logical_or(g == 0, group != ex_ref[jnp.maximum(g - 1, 0)])

    @pl.when(jnp.logical_and(new_expert, is_expert))
    def _():
        wgu_bf[...] = wgu_ref[...].astype(BF16)
        wd_bf[...] = wd_ref[...].astype(BF16)

    @pl.when(is_expert)
    def _():
        for cp in in_copies(g, slot):
            cp.wait()

        @pl.when(is_expert_step(g + 1))
        def _():
            for cp in in_copies(g + 1, 1 - slot):
                cp.start()

        gu = jnp.dot(xbuf[slot], wgu_bf[...], preferred_element_type=F32)
        act = (_silu(gu[:, :ff]) * gu[:, ff:]).astype(BF16)
        for hlf in range(2):
            rows = slice(hlf * half_rows, (hlf + 1) * half_rows)
            ybuf[slot, rows, :] = jnp.dot(act[rows, :], wd_bf[...],
                                          preferred_element_type=F32).astype(BF16)
            for cp in out_copies(g, range(hlf * STEP_PAGES // 2, (hlf + 1) * STEP_PAGES // 2)):
                cp.start()

    @pl.when(jnp.logical_and(g < live, group == N_EXPERTS))
    def _():
        for cp in out_copies(g, range(STEP_PAGES), zeros=True):
            cp.start()


def _experts(xs_pages, plan, w_gu, w_down, layer, n_steps):
    n_pages, _, d = xs_pages.shape
    two_ff = w_gu.shape[-1]
    ff = two_ff // 2
    in_pages, out_pages, ex, live = plan
    grid_spec = pltpu.PrefetchScalarGridSpec(
        num_scalar_prefetch=4,
        grid=(n_steps,),
        in_specs=[
            pl.BlockSpec(memory_space=pl.ANY),
            pl.BlockSpec((None, None, d, two_ff),
                         lambda g, ip, op, e, nl: (layer, jnp.minimum(e[g], N_EXPERTS - 1), 0, 0)),
            pl.BlockSpec((None, None, ff, d),
                         lambda g, ip, op, e, nl: (layer, jnp.minimum(e[g], N_EXPERTS - 1), 0, 0)),
        ],
        out_specs=pl.BlockSpec(memory_space=pl.ANY),
        scratch_shapes=[pltpu.VMEM((d, two_ff), BF16), pltpu.VMEM((ff, d), BF16),
                        pltpu.VMEM((2, STEP_PAGES * PAGE, d), BF16),
                        pltpu.VMEM((2, STEP_PAGES * PAGE, d), BF16),
                        pltpu.VMEM((STEP_PAGES, PAGE, d), BF16),
                        pltpu.SemaphoreType.DMA((2,)),
                        pltpu.SemaphoreType.DMA((2,))],
    )
    return pl.pallas_call(
        _expert_kernel,
        out_shape=jax.ShapeDtypeStruct((n_pages + 2 * STEP_PAGES, PAGE, d), BF16),
        grid_spec=grid_spec,
        compiler_params=_params(("arbitrary",)),
        name="experts",
    )(in_pages, out_pages, ex, live, xs_pages, w_gu, w_down)


def _combine_kernel(ys_ref, meta_ref, x_ref, mod_ref, fg_ref, o_ref):
    y = _unsort(ys_ref[...], meta_ref[...])
    o_ref[...] = _rms(x_ref[...] + mod_ref[5:6, :] * y, fg_ref[...])


def _final_combine(ys_rows, meta, x2, mod_l, final_g, seq):
    t, d = x2.shape
    per_b = seq // SORT_TILE
    return pl.pallas_call(
        _combine_kernel,
        out_shape=jax.ShapeDtypeStruct((t, d), F32),
        grid=(t // SORT_TILE,),
        in_specs=[
            pl.BlockSpec((SLOTS, d), lambda i: (i, 0)),
            pl.BlockSpec((None, META_ROWS, SORT_TILE), lambda i: (i, 0, 0)),
            pl.BlockSpec((SORT_TILE, d), lambda i: (i, 0)),
            pl.BlockSpec((None, 6, d), lambda i: (i // per_b, 0, 0)),
            pl.BlockSpec((1, d), lambda i: (0, 0)),
        ],
        out_specs=pl.BlockSpec((SORT_TILE, d), lambda i: (i, 0)),
        compiler_params=_params(("parallel",)),
        name="moe_combine_final_norm",
    )(ys_rows, meta, x2, mod_l, final_g.reshape(1, d))


def _moe_experts(xs, counts, w_gu, w_down, layer):
    d = xs.shape[1]
    n_tiles = counts.shape[0]
    n_pages = n_tiles * PAGES_PER_TILE
    n_steps = n_pages // STEP_PAGES + N_EXPERTS + 1 + 2
    plan = _expert_plan(counts[:, :, 0].astype(jnp.int32), n_steps)
    ys = _experts(xs.reshape(n_pages, PAGE, d), plan, w_gu, w_down, layer, n_steps)
    return ys.reshape(-1, d)


def kernel(x, c, norm1_g, norm2_g, ada_w, ada_b, ev_w_in, ev_w_out, sgu_ln_g, sgu_ln_b, sgu_w, sgu_b, lam_q1, lam_k1, lam_q2, lam_k2, diff_subln_g, od_w_in, od_w_out, conv_w, conv_b, conv_ln_g, conv_ln_b, router_w, router_b, moe_w_gu, moe_w_down, final_g):
    bsz, seq, d = x.shape
    depth = ada_w.shape[0]
    half = d // 2
    diff_heads = half // (2 * HEAD_DIM)
    dil_heads = half // HEAD_DIM
    t = bsz * seq

    mod = _modulation(c, ada_w, ada_b)
    rw_hi = router_w.astype(BF16)
    rw_lo = (router_w - rw_hi.astype(F32)).astype(BF16)
    rw_split = jnp.pad(jnp.concatenate([rw_hi, rw_lo], axis=1),
                       ((0, 0), (0, LANES - 2 * N_EXPERTS)))

    x2 = x.reshape(t, d)
    moe = None
    for l in range(depth):
        i = l // 2
        w_in = (ev_w_in if l % 2 == 0 else od_w_in)[i].astype(BF16)
        x2, proj = _in_projection(x2, mod[l], norm1_g[l], w_in, seq, moe)
        if l % 2 == 0:
            lam_init = 0.8 - 0.6 * math.exp(-0.3 * l)
            part_a = _spatial_gating(proj, sgu_ln_g[i], sgu_ln_b[i], sgu_w[i], sgu_b[i], half)
            part_b = _diff_attention(proj.reshape(bsz, seq, -1), lam_q1[i], lam_k1[i], lam_q2[i],
                                     lam_k2[i], diff_subln_g[i], lam_init, diff_heads, 2 * half)
            w_out = ev_w_out[i]
        else:
            proj3 = proj.reshape(bsz, seq, -1)
            part_a = _dilated_attention(proj3, dil_heads).reshape(t, half)
            part_b = _conv_module(proj3, conv_w[i], conv_b[i], conv_ln_g[i], conv_ln_b[i],
                                  (2 * len(DIL_PAIRS) + 1) * half)
            w_out = od_w_out[i]
        x2, xs, meta, counts = _out_projection(part_a, part_b.reshape(t, half), x2, mod[l],
                                               norm2_g[l], w_out.astype(BF16), rw_split,
                                               router_b, seq)
        moe = (_moe_experts(xs, counts, moe_w_gu, moe_w_down, l), meta, mod[l])
    out = _final_combine(moe[0], moe[1], x2, mod[depth - 1], final_g, seq)
    return out.reshape(bsz, seq, d)
```

```python
import functools
import math

import jax
import jax.numpy as jnp
from jax import lax
from jax.experimental import pallas as pl
from jax.experimental.pallas import tpu as pltpu

F32 = jnp.float32
BF16 = jnp.bfloat16

HEAD_DIM = 64
LANES = 128
CHUNK = 128
SGU_GROUPS = 4
DIL_PAIRS = ((128, 1), (512, 4), (2048, 16))
WIN_BLOCK = 128
CONV_K = 31
CONV_SUBLANES = 8
N_EXPERTS = 16
EXPERTS_PER_GROUP = 4
RMS_EPS = 1e-6
LN_EPS = 1e-5
NEG = -1e30
LOG2E = 1.4426950408889634
VMEM_LIMIT = 56 * 1024 * 1024


def _params(sem):
    return pltpu.CompilerParams(dimension_semantics=sem, vmem_limit_bytes=VMEM_LIMIT)


def _gelu(x):
    return x * (0.5 * (1.0 + jnp.tanh(0.7978845608028654 * (x + 0.044715 * (x * x * x)))))


def _silu(x):
    return x * jax.nn.sigmoid(x)


def _rms(x, g):
    ms = jnp.mean(x * x, axis=-1, keepdims=True)
    return x * lax.rsqrt(ms + RMS_EPS) * g


def _layer_norm(x, g, b):
    mu = jnp.mean(x, axis=-1, keepdims=True)
    xc = x - mu
    var = jnp.mean(xc * xc, axis=-1, keepdims=True)
    return xc * lax.rsqrt(var + LN_EPS) * g + b


def _mod_kernel(c_ref, w_ref, b_ref, o_ref):
    ca = _silu(c_ref[...]).astype(BF16)
    o_ref[...] = jnp.dot(ca, w_ref[...].astype(BF16), preferred_element_type=F32) + b_ref[...]


def _modulation(c, ada_w, ada_b):
    depth, d, six_d = ada_w.shape
    bsz = c.shape[0]
    n = six_d // d
    out = pl.pallas_call(
        _mod_kernel,
        out_shape=jax.ShapeDtypeStruct((depth, bsz, six_d), F32),
        grid=(depth, n),
        in_specs=[
            pl.BlockSpec((bsz, d), lambda l, j: (0, 0)),
            pl.BlockSpec((None, d, d), lambda l, j: (l, 0, j)),
            pl.BlockSpec((None, 1, d), lambda l, j: (l, 0, j)),
        ],
        out_specs=pl.BlockSpec((None, bsz, d), lambda l, j: (l, 0, j)),
        compiler_params=_params(("parallel", "parallel")),
        name="modulation",
    )(c, ada_w, ada_b.reshape(depth, 1, six_d))
    return out.reshape(depth, bsz, n, d)


SORT_TILE = 256
PAGE = 16
TOP_K = 2
SLOTS = TOP_K * SORT_TILE + N_EXPERTS * PAGE
PAGES_PER_TILE = SLOTS // PAGE
STEP_PAGES = 32
META_ROWS = 8


def _unsort(ys, meta):
    pos0 = meta[0:1, :].astype(jnp.int32)
    pos1 = meta[1:2, :].astype(jnp.int32)
    slot = lax.broadcasted_iota(jnp.int32, (SLOTS, meta.shape[1]), 0)
    w = (jnp.where(slot == pos0, meta[2:3, :], 0.0)
         + jnp.where(slot == pos1, meta[3:4, :], 0.0)).astype(BF16)
    return lax.dot_general(w, ys, (((0,), (0,)), ((), ())), preferred_element_type=F32)


def _project(x, mod_ref, g_ref, w_ref, o_ref, n_chunk):
    h = _rms(x, g_ref[...]) * (1.0 + mod_ref[1:2, :]) + mod_ref[0:1, :]
    hb = h.astype(BF16)
    for j in range(0, o_ref.shape[-1], n_chunk):
        o_ref[:, j:j + n_chunk] = jnp.dot(
            hb, w_ref[:, j:j + n_chunk], preferred_element_type=F32).astype(o_ref.dtype)


def _inproj_kernel(x_ref, mod_ref, g_ref, w_ref, o_ref, *, n_chunk):
    _project(x_ref[...], mod_ref, g_ref, w_ref, o_ref, n_chunk)


def _inproj_moe_kernel(ys_ref, meta_ref, x_ref, pmod_ref, mod_ref, g_ref, w_ref, xo_ref, o_ref,
                       *, n_chunk):
    for c in range(meta_ref.shape[0]):
        ts = slice(c * SORT_TILE, (c + 1) * SORT_TILE)
        y = _unsort(ys_ref[c * SLOTS:(c + 1) * SLOTS, :], meta_ref[c])
        xo_ref[ts, :] = x_ref[ts, :] + pmod_ref[5:6, :] * y
    _project(xo_ref[...], mod_ref, g_ref, w_ref, o_ref, n_chunk)


def _in_projection(x2, mod_l, g, w_bf, seq, moe=None, tm=512):
    t, d = x2.shape
    n = w_bf.shape[1]
    per_b = seq // tm
    sub = tm // SORT_TILE
    row_spec = pl.BlockSpec((tm, d), lambda i: (i, 0))
    mod_spec = pl.BlockSpec((None, 6, d), lambda i: (i // per_b, 0, 0))
    tail_specs = [mod_spec, pl.BlockSpec((1, d), lambda i: (0, 0)),
                  pl.BlockSpec((d, n), lambda i: (0, 0))]
    proj_shape = jax.ShapeDtypeStruct((t, n), BF16)
    proj_spec = pl.BlockSpec((tm, n), lambda i: (i, 0))
    if moe is None:
        return x2, pl.pallas_call(
            functools.partial(_inproj_kernel, n_chunk=512),
            out_shape=proj_shape,
            grid=(t // tm,),
            in_specs=[row_spec] + tail_specs,
            out_specs=proj_spec,
            compiler_params=_params(("parallel",)),
            name="in_projection",
        )(x2, mod_l, g.reshape(1, d), w_bf)
    ys_rows, meta, prev_mod = moe
    return pl.pallas_call(
        functools.partial(_inproj_moe_kernel, n_chunk=512),
        out_shape=(jax.ShapeDtypeStruct((t, d), F32), proj_shape),
        grid=(t // tm,),
        in_specs=[pl.BlockSpec((sub * SLOTS, d), lambda i: (i, 0)),
                  pl.BlockSpec((sub, META_ROWS, SORT_TILE), lambda i: (i, 0, 0)),
                  row_spec, mod_spec] + tail_specs,
        out_specs=(row_spec, proj_spec),
        compiler_params=_params(("parallel",)),
        name="moe_combine_in_projection",
    )(ys_rows, meta, x2, prev_mod, mod_l, g.reshape(1, d), w_bf)


def _sgu_kernel(u_ref, v_ref, lng_ref, lnb_ref, w_ref, bias_ref, o_ref):
    u = _gelu(u_ref[...].astype(F32))
    v = _gelu(v_ref[...].astype(F32))
    vb = _layer_norm(v, lng_ref[...], lnb_ref[...]).astype(BF16)
    row = lax.broadcasted_iota(jnp.int32, (CHUNK, CHUNK), 0)
    col = lax.broadcasted_iota(jnp.int32, (CHUNK, CHUNK), 1)
    causal = col <= row
    gd = u.shape[1] // SGU_GROUPS
    n_chunks = u.shape[0] // CHUNK
    for g in range(SGU_GROUPS):
        w = jnp.where(causal, w_ref[g], 0.0).astype(BF16)
        cs = slice(g * gd, (g + 1) * gd)
        rhs = jnp.concatenate([vb[c * CHUNK:(c + 1) * CHUNK, cs] for c in range(n_chunks)], axis=1)
        mixed = jnp.dot(w, rhs, preferred_element_type=F32)
        for c in range(n_chunks):
            rs = slice(c * CHUNK, (c + 1) * CHUNK)
            o_ref[rs, cs] = (u[rs, cs] * (mixed[:, c * gd:(c + 1) * gd] + bias_ref[:, cs])
                             ).astype(o_ref.dtype)


def _spatial_gating(proj, ln_g, ln_b, w_s, b_s, width, tm=1024):
    t = proj.shape[0]
    gd = width // SGU_GROUPS
    bias = jnp.repeat(b_s.T, gd, axis=1)
    return pl.pallas_call(
        _sgu_kernel,
        out_shape=jax.ShapeDtypeStruct((t, width), BF16),
        grid=(t // tm,),
        in_specs=[
            pl.BlockSpec((tm, width), lambda i: (i, 0)),
            pl.BlockSpec((tm, width), lambda i: (i, 1)),
            pl.BlockSpec((1, width), lambda i: (0, 0)),
            pl.BlockSpec((1, width), lambda i: (0, 0)),
            pl.BlockSpec((SGU_GROUPS, CHUNK, CHUNK), lambda i: (0, 0, 0)),
            pl.BlockSpec((CHUNK, width), lambda i: (0, 0)),
        ],
        out_specs=pl.BlockSpec((tm, width), lambda i: (i, 0)),
        compiler_params=_params(("parallel",)),
        name="spatial_gating",
    )(proj, proj, ln_g.reshape(1, width), ln_b.reshape(1, width), w_s, bias)


ONES_ROWS = 16


def _split3(x):
    hi = x.astype(BF16).astype(F32)
    r1 = x - hi
    mid = r1.astype(BF16).astype(F32)
    lo = (r1 - mid).astype(BF16).astype(F32)
    return hi, mid, lo


def _diff_kernel(slopes_ref, lq1_ref, lk1_ref, lq2_ref, lk2_ref, q_ref, k_ref, v_ref, g_ref,
                 o_ref, kx, qx, vt, acc, *, tq, lam_init):
    h = pl.program_id(1)
    seq, vd = v_ref.shape
    n_q = seq // tq
    hd = HEAD_DIM
    lane = lax.broadcasted_iota(jnp.int32, (tq, LANES), 1)

    def place(pieces, base, fill):
        out = fill
        for n, piece in enumerate(pieces):
            out = jnp.where(lane == base + n, piece, out)
        return out

    slope = slopes_ref[h] * LOG2E
    one = jnp.ones((tq, LANES), F32)
    for c in range(n_q):
        rs = slice(c * tq, (c + 1) * tq)
        k = k_ref[rs, :].astype(F32)
        pos = (lax.broadcasted_iota(jnp.int32, (tq, LANES), 0) + c * tq).astype(F32) * slope
        pieces = _split3(pos)
        zero = jnp.zeros_like(k)
        kx[0, rs, :] = jnp.where(lane < hd, k, place(pieces, hd, zero)).astype(BF16)
        kx[1, rs, :] = jnp.where(lane >= hd, k, place(pieces, 0, zero)).astype(BF16)
        vt[0:vd, rs] = v_ref[rs, :].astype(F32).T.astype(BF16)
        vt[vd:, rs] = jnp.ones((ONES_ROWS, tq), BF16)
        q = q_ref[rs, :].astype(F32) * (hd ** -0.5 * LOG2E)
        qx[0, rs, :] = jnp.where(lane < hd, q, place((one, one, one), hd, zero)).astype(BF16)
        qx[1, rs, :] = jnp.where(lane >= hd, q, place((one, one, one), 0, zero)).astype(BF16)

    kc = tq
    chunks = {qi: [(k0, min(k0 + kc, (qi + 1) * tq)) for k0 in range(0, (qi + 1) * tq, kc)]
              for qi in range(n_q)}
    ms = {}
    for c in range(max(len(v) for v in chunks.values())):
        for qi in range(n_q):
            if c >= len(chunks[qi]):
                continue
            k0, k1 = chunks[qi][c]
            for m in range(2):
                s = lax.dot_general(kx[m, k0:k1, :], qx[m, qi * tq:(qi + 1) * tq, :],
                                    (((1,), (1,)), ((), ())),
                                    preferred_element_type=F32)
                if k1 > qi * tq + 1:
                    key_i = lax.broadcasted_iota(jnp.int32, s.shape, 0) + k0
                    qry_i = lax.broadcasted_iota(jnp.int32, s.shape, 1) + qi * tq
                    s = jnp.where(key_i <= qry_i, s, NEG)
                m_tile = jnp.max(s, axis=0, keepdims=True)
                if c == 0:
                    ms[qi, m] = m_tile
                    p = jnp.exp2(s - m_tile).astype(BF16)
                    acc[qi, m] = jnp.dot(vt[:, k0:k1], p, preferred_element_type=F32)
                else:
                    m_new = jnp.maximum(ms[qi, m], m_tile)
                    alpha = jnp.exp2(ms[qi, m] - m_new)
                    ms[qi, m] = m_new
                    p = jnp.exp2(s - m_new).astype(BF16)
                    acc[qi, m] = (alpha * acc[qi, m]
                                  + jnp.dot(vt[:, k0:k1], p, preferred_element_type=F32))

    lam = (jnp.exp(jnp.sum(lq1_ref[...] * lk1_ref[...], axis=-1, keepdims=True))
           - jnp.exp(jnp.sum(lq2_ref[...] * lk2_ref[...], axis=-1, keepdims=True)) + lam_init)
    for qi in range(n_q):
        o_t = (acc[qi, 0, 0:vd, :] / acc[qi, 0, vd:vd + 1, :]
               - lam * (acc[qi, 1, 0:vd, :] / acc[qi, 1, vd:vd + 1, :]))
        o_ref[qi * tq:(qi + 1) * tq, :] = (
            _rms(o_t.T, g_ref[...]) * (1.0 - lam_init)).astype(o_ref.dtype)


def _diff_attention(proj3, lq1, lk1, lq2, lk2, subln_g, lam_init, n_heads, col0, tq=1024):
    bsz, seq, _ = proj3.shape
    slopes = 2.0 ** (-8.0 * jnp.arange(1, n_heads + 1, dtype=F32) / n_heads)
    qb, kb, vb = col0 // LANES, col0 // LANES + n_heads, col0 // LANES + 2 * n_heads
    vec = lambda a: a.reshape(1, -1)
    small = lambda n: pl.BlockSpec((1, n), lambda b, h: (0, 0))
    return pl.pallas_call(
        functools.partial(_diff_kernel, tq=tq, lam_init=lam_init),
        out_shape=jax.ShapeDtypeStruct((bsz, seq, n_heads * LANES), BF16),
        grid=(bsz, n_heads),
        in_specs=[
            pl.BlockSpec(memory_space=pltpu.SMEM),
            small(HEAD_DIM), small(HEAD_DIM), small(HEAD_DIM), small(HEAD_DIM),
            pl.BlockSpec((None, seq, LANES), lambda b, h: (b, 0, qb + h)),
            pl.BlockSpec((None, seq, LANES), lambda b, h: (b, 0, kb + h)),
            pl.BlockSpec((None, seq, LANES), lambda b, h: (b, 0, vb + h)),
            small(LANES),
        ],
        out_specs=pl.BlockSpec((None, seq, LANES), lambda b, h: (b, 0, h)),
        scratch_shapes=[
            pltpu.VMEM((2, seq, LANES), BF16),
            pltpu.VMEM((2, seq, LANES), BF16),
            pltpu.VMEM((LANES + ONES_ROWS, seq), BF16),
            pltpu.VMEM((seq // tq, 2, LANES + ONES_ROWS, tq), F32),
        ],
        compiler_params=_params(("parallel", "parallel")),
        name="diff_attention",
    )(slopes, vec(lq1), vec(lk1), vec(lq2), vec(lk2), proj3, proj3, proj3, vec(subln_g))


def _dil_kernel(slopes_ref, q1_ref, q2_ref, q3_ref, k1_ref, k2_ref, k3_ref, v_ref, o_ref,
                qf, kf, vf, bias_s, acc_s, m_s, l_s, *, seq):
    hp = pl.program_id(1)
    wb = WIN_BLOCK
    hd = HEAD_DIM
    n_q_blocks = seq // wb

    qf[0] = q2_ref[...].astype(F32)
    qf[1] = q3_ref[...].astype(F32)
    kf[0] = k2_ref[...].astype(F32)
    kf[1] = k3_ref[...].astype(F32)
    vf[...] = v_ref[...].astype(F32)

    row = lax.broadcasted_iota(jnp.int32, (2 * wb, 2 * wb), 0)
    col = lax.broadcasted_iota(jnp.int32, (2 * wb, 2 * wb), 1)
    qi_ = jnp.where(row < wb, row, row - wb)
    dist = qi_ + wb - col
    slope = jnp.where(row < wb, slopes_ref[2 * hp], slopes_ref[2 * hp + 1])
    valid = jnp.logical_and(dist >= 0, dist <= wb)
    table = jnp.where(valid, -(slope * LOG2E) * dist.astype(F32), NEG)
    bias_s[1] = table
    bias_s[0] = jnp.where(col < wb, NEG, table)

    first = lax.broadcasted_iota(jnp.int32, (wb, LANES), 1) < hd
    q_scale = hd ** -0.5 * LOG2E

    for g, (window, dil) in enumerate(DIL_PAIRS):
        assert window // dil == wb
        nb = n_q_blocks // dil
        refs = {"q": (q1_ref, qf), "k": (k1_ref, kf), "v": (v_ref, vf)}

        def rows(kind, start, g=g, dil=dil, refs=refs):
            direct, copies = refs[kind]
            if dil == 1:
                return direct[pl.ds(pl.multiple_of(start, wb), wb), :]
            src = copies if kind == "v" else copies.at[g - 1]
            return src[pl.ds(start, wb, stride=dil), :]

        def body(idx, _, g=g, dil=dil, nb=nb, rows=rows):
            r = idx // nb
            n = idx % nb
            cur = n * (wb * dil) + r
            q = rows("q", cur).astype(F32) * q_scale
            zero = jnp.zeros_like(q)
            qx = jnp.concatenate([jnp.where(first, q, zero), jnp.where(first, zero, q)],
                                 axis=0).astype(BF16)
            if nb > 1:
                prev = jnp.maximum(n - 1, 0) * (wb * dil) + r
                keys = jnp.concatenate([rows("k", prev), rows("k", cur)], axis=0).astype(BF16)
                vals = jnp.concatenate([rows("v", prev), rows("v", cur)], axis=0).astype(BF16)
                bias = bias_s[jnp.minimum(n, 1)]
            else:
                keys, vals = rows("k", cur).astype(BF16), rows("v", cur).astype(BF16)
                bias = bias_s[1, :, wb:]
            vx = jnp.concatenate([vals, jnp.ones_like(vals)], axis=1)
            s = lax.dot_general(qx, keys, (((1,), (1,)), ((), ())),
                                preferred_element_type=F32) + bias
            m = jnp.max(s, axis=-1, keepdims=True)
            p = jnp.exp2(s - m).astype(BF16)
            o = jnp.dot(p, vx, preferred_element_type=F32)
            acc = jnp.where(first, o[0:wb, 0:LANES], o[wb:, 0:LANES])
            den = jnp.where(first, o[0:wb, LANES:], o[wb:, LANES:])
            mx = jnp.where(first, jnp.broadcast_to(m[0:wb], (wb, LANES)),
                           jnp.broadcast_to(m[wb:], (wb, LANES)))
            if dil == 1:
                dst = pl.ds(pl.multiple_of(cur, wb), wb)
            else:
                dst = pl.ds(cur, wb, stride=dil)
            acc_s[g, dst, :] = acc
            m_s[g, dst, :] = mx
            l_s[g, dst, :] = den
            return 0

        lax.fori_loop(0, n_q_blocks, body, 0, unroll=16)

    mr = 256
    for c in range(seq // mr):
        rs = slice(c * mr, (c + 1) * mr)
        m1, m2, m3 = m_s[0, rs, :], m_s[1, rs, :], m_s[2, rs, :]
        m = jnp.maximum(jnp.maximum(m1, m2), m3)
        w1, w2, w3 = jnp.exp2(m1 - m), jnp.exp2(m2 - m), jnp.exp2(m3 - m)
        num = w1 * acc_s[0, rs, :] + w2 * acc_s[1, rs, :] + w3 * acc_s[2, rs, :]
        den = w1 * l_s[0, rs, :] + w2 * l_s[1, rs, :] + w3 * l_s[2, rs, :]
        o_ref[rs, :] = (num / den).astype(o_ref.dtype)


def _dilated_attention(proj3, n_heads):
    bsz, seq, _ = proj3.shape
    n_pairs = n_heads * HEAD_DIM // LANES
    slopes = 2.0 ** (-8.0 * jnp.arange(1, n_heads + 1, dtype=F32) / n_heads)
    groups = len(DIL_PAIRS)

    def spec(blk0):
        return pl.BlockSpec((None, seq, LANES), lambda b, p: (b, 0, blk0 + p))

    q_specs = [spec(g * n_pairs) for g in range(groups)]
    k_specs = [spec((groups + g) * n_pairs) for g in range(groups)]
    v_spec = spec(2 * groups * n_pairs)
    return pl.pallas_call(
        functools.partial(_dil_kernel, seq=seq),
        out_shape=jax.ShapeDtypeStruct((bsz, seq, n_pairs * LANES), BF16),
        grid=(bsz, n_pairs),
        in_specs=[pl.BlockSpec(memory_space=pltpu.SMEM)] + q_specs + k_specs + [v_spec],
        out_specs=pl.BlockSpec((None, seq, LANES), lambda b, p: (b, 0, p)),
        scratch_shapes=[
            pltpu.VMEM((groups - 1, seq, LANES), F32),
            pltpu.VMEM((groups - 1, seq, LANES), F32),
            pltpu.VMEM((seq, LANES), F32),
            pltpu.VMEM((2, 2 * WIN_BLOCK, 2 * WIN_BLOCK), F32),
            pltpu.VMEM((groups, seq, LANES), F32),
            pltpu.VMEM((groups, seq, LANES), F32),
            pltpu.VMEM((groups, seq, LANES), F32),
        ],
        compiler_params=_params(("parallel", "parallel")),
        name="dilated_attention",
    )(slopes, *([proj3] * 7))


def _conv_kernel(a_ref, gate_ref, w_ref, cb_ref, lng_ref, lnb_ref, o_ref, buf, win, *,
                 ts, halo, rc):
    @pl.when(pl.program_id(1) == 0)
    def _():
        buf[0:halo, :] = jnp.zeros((halo, buf.shape[1]), F32)

    buf[halo:halo + ts, :] = a_ref[...].astype(F32) * jax.nn.sigmoid(gate_ref[...].astype(F32))
    off = halo - (CONV_K - 1)
    sub = CONV_SUBLANES
    for c in range(ts // rc):
        acc = jnp.zeros((rc, buf.shape[1]), F32)
        for r in range(sub):
            span = rc + (CONV_K - 1 - r) // sub * sub
            win[r, 0:span, :] = buf[c * rc + off + r:c * rc + off + r + span, :]
            for k in range(r, CONV_K, sub):
                acc = acc + w_ref[k:k + 1, :] * win[r, k - r:k - r + rc, :]
        y = _layer_norm(acc + cb_ref[...], lng_ref[...], lnb_ref[...])
        o_ref[c * rc:(c + 1) * rc, :] = _silu(y).astype(o_ref.dtype)
    buf[0:halo, :] = buf[ts:ts + halo, :]


def _conv_module(proj3, conv_w, conv_b, ln_g, ln_b, col0, ts=512):
    bsz, seq, _ = proj3.shape
    width = conv_w.shape[1]
    halo = 32
    rc = 64
    vec = lambda a: a.reshape(1, width)
    small = pl.BlockSpec((1, width), lambda b, s: (0, 0))
    return pl.pallas_call(
        functools.partial(_conv_kernel, ts=ts, halo=halo, rc=rc),
        out_shape=jax.ShapeDtypeStruct((bsz, seq, width), BF16),
        grid=(bsz, seq // ts),
        in_specs=[
            pl.BlockSpec((None, ts, width), lambda b, s: (b, s, col0 // width)),
            pl.BlockSpec((None, ts, width), lambda b, s: (b, s, col0 // width + 1)),
            pl.BlockSpec((CONV_K, width), lambda b, s: (0, 0)),
            small, small, small,
        ],
        out_specs=pl.BlockSpec((None, ts, width), lambda b, s: (b, s, 0)),
        scratch_shapes=[pltpu.VMEM((ts + halo, width), F32),
                        pltpu.VMEM((CONV_SUBLANES, rc + halo, width), F32)],
        compiler_params=_params(("parallel", "arbitrary")),
        name="conv_module",
    )(proj3, proj3, conv_w, vec(conv_b), vec(ln_g), vec(ln_b))


def _route(logits_t, rb):
    mx = jnp.max(logits_t, axis=0, keepdims=True)
    ex = jnp.exp(logits_t - mx)
    probs = ex / jnp.sum(ex, axis=0, keepdims=True)
    sel = probs + rb
    srow = [sel[i:i + 1, :] for i in range(N_EXPERTS)]
    prow = [probs[i:i + 1, :] for i in range(N_EXPERTS)]
    epg = EXPERTS_PER_GROUP
    n_grp = N_EXPERTS // epg

    scores = []
    for g in range(n_grp):
        a, b, c, d = srow[g * epg:(g + 1) * epg]
        hi1, lo1 = jnp.maximum(a, b), jnp.minimum(a, b)
        hi2, lo2 = jnp.maximum(c, d), jnp.minimum(c, d)
        scores.append(jnp.maximum(hi1, hi2)
                      + jnp.maximum(jnp.minimum(hi1, hi2), jnp.maximum(lo1, lo2)))
    best = scores[0]
    grp = jnp.zeros(best.shape, jnp.int32)
    for g in range(1, n_grp):
        better = scores[g] > best
        grp = jnp.where(better, g, grp)
        best = jnp.where(better, scores[g], best)

    def pick(rows_, j):
        out = rows_[j]
        for g in range(1, n_grp):
            out = jnp.where(grp == g, rows_[g * epg + j], out)
        return out

    ing = [pick(srow, j) for j in range(epg)]
    ping = [pick(prow, j) for j in range(epg)]
    b0, i0, p0 = ing[0], jnp.zeros(best.shape, jnp.int32), ping[0]
    for j in range(1, epg):
        better = ing[j] > b0
        i0 = jnp.where(better, j, i0)
        p0 = jnp.where(better, ping[j], p0)
        b0 = jnp.where(better, ing[j], b0)
    b1 = jnp.full(best.shape, -jnp.inf, F32)
    i1 = jnp.zeros(best.shape, jnp.int32)
    p1 = jnp.zeros(best.shape, F32)
    for j in range(epg):
        better = jnp.logical_and(i0 != j, ing[j] > b1)
        i1 = jnp.where(better, j, i1)
        p1 = jnp.where(better, ping[j], p1)
        b1 = jnp.where(better, ing[j], b1)
    den = p0 + p1
    return grp * epg + i0, grp * epg + i1, p0 / den, p1 / den


def _sort_tile(e0, e1, hb):
    st = hb.shape[0]
    eid = lax.broadcasted_iota(jnp.int32, (N_EXPERTS, st), 0)
    sel0 = jnp.where(eid == e0, 1.0, 0.0)
    sel1 = jnp.where(eid == e1, 1.0, 0.0)
    sel = sel0 + sel1
    before = (lax.broadcasted_iota(jnp.int32, (st, st), 0)
              < lax.broadcasted_iota(jnp.int32, (st, st), 1))
    rank = jnp.dot(sel.astype(BF16), jnp.where(before, 1.0, 0.0).astype(BF16),
                   preferred_element_type=F32)
    cnt = jnp.sum(sel, axis=1, keepdims=True)
    padded = jnp.ceil(cnt * (1.0 / PAGE)) * PAGE
    ecol = lax.broadcasted_iota(jnp.int32, (N_EXPERTS, 1), 0)
    seg = jnp.zeros((N_EXPERTS, 1), F32)
    run = jnp.zeros((1, 1), F32)
    for e in range(N_EXPERTS):
        seg = jnp.where(ecol == e, run, seg)
        run = run + padded[e:e + 1, :]
    base = seg + rank
    pos0 = jnp.sum(sel0 * base, axis=0, keepdims=True)
    pos1 = jnp.sum(sel1 * base, axis=0, keepdims=True)
    slot = lax.broadcasted_iota(jnp.int32, (SLOTS, st), 0)
    perm = (jnp.where(slot == pos0.astype(jnp.int32), 1.0, 0.0)
            + jnp.where(slot == pos1.astype(jnp.int32), 1.0, 0.0)).astype(BF16)
    rows = jnp.dot(perm, hb, preferred_element_type=F32).astype(BF16)
    return pos0, pos1, cnt, rows


def _outproj_kernel(a_ref, b_ref, x_ref, mod_ref, g_ref, w_ref, rw_ref, rb_ref,
                    xo_ref, xs_ref, meta_ref, cnt_ref):
    ka = a_ref.shape[1]
    for c in range(x_ref.shape[0] // SORT_TILE):
        ts = slice(c * SORT_TILE, (c + 1) * SORT_TILE)
        y = (jnp.dot(a_ref[ts, :], w_ref[0:ka, :], preferred_element_type=F32)
             + jnp.dot(b_ref[ts, :], w_ref[ka:, :], preferred_element_type=F32))
        xn = x_ref[ts, :] + mod_ref[2:3, :] * y
        xo_ref[ts, :] = xn
        h = _rms(xn, g_ref[...]) * (1.0 + mod_ref[4:5, :]) + mod_ref[3:4, :]
        hb = h.astype(BF16)
        hl = (h - hb.astype(F32)).astype(BF16)
        parts = (jnp.dot(hb, rw_ref[...], preferred_element_type=F32)
                 + jnp.dot(hl, rw_ref[...], preferred_element_type=F32)).T
        lt = parts[0:N_EXPERTS, :] + parts[N_EXPERTS:2 * N_EXPERTS, :]
        e0, e1, g0, g1 = _route(lt, rb_ref[...])
        pos0, pos1, cnt, rows = _sort_tile(e0, e1, hb)
        xs_ref[c * SLOTS:(c + 1) * SLOTS, :] = rows
        meta_ref[c] = jnp.concatenate(
            [pos0, pos1, g0, g1, jnp.zeros((META_ROWS - 4, SORT_TILE), F32)], axis=0)
        cnt_ref[c] = jnp.broadcast_to(cnt, (N_EXPERTS, LANES))


def _out_projection(a, b, x2, mod_l, g, w_bf, rw_split, rb, seq, tm=1024):
    t, d = x2.shape
    ka, kb = a.shape[1], b.shape[1]
    per_b = seq // tm
    sub = tm // SORT_TILE
    n_tiles = t // SORT_TILE
    return pl.pallas_call(
        _outproj_kernel,
        out_shape=(jax.ShapeDtypeStruct((t, d), F32),
                   jax.ShapeDtypeStruct((n_tiles * SLOTS, d), BF16),
                   jax.ShapeDtypeStruct((n_tiles, META_ROWS, SORT_TILE), F32),
                   jax.ShapeDtypeStruct((n_tiles, N_EXPERTS, LANES), F32)),
        grid=(t // tm,),
        in_specs=[
            pl.BlockSpec((tm, ka), lambda i: (i, 0)),
            pl.BlockSpec((tm, kb), lambda i: (i, 0)),
            pl.BlockSpec((tm, d), lambda i: (i, 0)),
            pl.BlockSpec((None, 6, d), lambda i: (i // per_b, 0, 0)),
            pl.BlockSpec((1, d), lambda i: (0, 0)),
            pl.BlockSpec((ka + kb, d), lambda i: (0, 0)),
            pl.BlockSpec((d, LANES), lambda i: (0, 0)),
            pl.BlockSpec((N_EXPERTS, 1), lambda i: (0, 0)),
        ],
        out_specs=(pl.BlockSpec((tm, d), lambda i: (i, 0)),
                   pl.BlockSpec((sub * SLOTS, d), lambda i: (i, 0)),
                   pl.BlockSpec((sub, META_ROWS, SORT_TILE), lambda i: (i, 0, 0)),
                   pl.BlockSpec((sub, N_EXPERTS, LANES), lambda i: (i, 0, 0))),
        compiler_params=_params(("parallel",)),
        name="out_projection",
    )(a, b, x2, mod_l, g.reshape(1, d), w_bf, rw_split, rb.reshape(N_EXPERTS, 1))


def _expert_plan(counts, n_steps_max):
    n_tiles = counts.shape[0]
    n_groups = N_EXPERTS + 1
    i32 = jnp.int32
    npg = (counts + PAGE - 1) // PAGE
    npg = jnp.concatenate([npg, PAGES_PER_TILE - jnp.sum(npg, axis=1, keepdims=True)], axis=1)
    seg = jnp.cumsum(npg, axis=1) - npg
    page_base = jnp.arange(n_tiles, dtype=i32)[:, None] * PAGES_PER_TILE + seg
    cum_t = jnp.cumsum(npg, axis=0)
    tot = cum_t[-1]
    steps_e = (tot + STEP_PAGES - 1) // STEP_PAGES
    step_end = jnp.cumsum(steps_e)
    n_live = step_end[-1]
    g = jnp.arange(n_steps_max, dtype=i32)
    ex = jnp.minimum(jnp.sum((step_end[:, None] <= g[None, :]).astype(i32), axis=0),
                     n_groups - 1)
    sel = jnp.arange(n_groups, dtype=i32)[:, None] == ex[None, :]

    def of_step(per_group):
        return jnp.sum(jnp.where(sel, per_group[:, None], 0), axis=0)

    def rows_of_step(per_tile_group):
        return jnp.sum(jnp.where(sel[None], per_tile_group[:, :, None], 0), axis=1)

    q = ((g - of_step(step_end - steps_e))[:, None] * STEP_PAGES
         + jnp.arange(STEP_PAGES, dtype=i32)[None, :])
    valid = jnp.logical_and(q < of_step(tot)[:, None], (g < n_live)[:, None])
    cum_e = rows_of_step(cum_t)
    ti = jnp.minimum(jnp.sum((cum_e[:, :, None] <= q[None]).astype(i32), axis=0), n_tiles - 1)
    shift = rows_of_step(page_base - (cum_t - npg))
    hit = jnp.arange(n_tiles, dtype=i32)[:, None, None] == ti[None]
    pid = q + jnp.sum(jnp.where(hit, shift[:, :, None], 0), axis=0)
    spare = (n_tiles * PAGES_PER_TILE + (g % 2)[:, None] * STEP_PAGES
             + jnp.arange(STEP_PAGES, dtype=i32)[None, :])
    first = jnp.where(g < n_live, pid[:, 0], pid[0, 0])[:, None]
    in_pages = jnp.where(valid, pid, first).astype(i32).reshape(-1)
    out_pages = jnp.where(valid, pid, spare).astype(i32).reshape(-1)
    return in_pages, out_pages, ex.astype(i32), n_live.astype(i32).reshape(1)


def _expert_kernel(inp_ref, outp_ref, ex_ref, live_ref, xs_ref, wgu_ref, wd_ref, ys_ref,
                   wgu_bf, wd_bf, xbuf, ybuf, zbuf, in_sems, out_sems):
    g = pl.program_id(0)
    live = live_ref[0]
    slot = g % 2
    ff = wd_ref.shape[0]
    n_spare = 2 * STEP_PAGES
    half_rows = STEP_PAGES * PAGE // 2

    def is_expert_step(step):
        group = ex_ref[jnp.minimum(step, pl.num_programs(0) - 1)]
        return jnp.logical_and(step < live, group < N_EXPERTS)

    def in_copies(step, buf):
        return [pltpu.make_async_copy(xs_ref.at[inp_ref[step * STEP_PAGES + k]],
                                      xbuf.at[buf, pl.ds(k * PAGE, PAGE), :], in_sems.at[buf])
                for k in range(STEP_PAGES)]

    def out_copies(step, pages, zeros=False):
        return [pltpu.make_async_copy(
            zbuf.at[k] if zeros else ybuf.at[slot, pl.ds(k * PAGE, PAGE), :],
            ys_ref.at[outp_ref[step * STEP_PAGES + k]], out_sems.at[slot])
                for k in pages]

    @pl.when(g == 0)
    def _():
        zbuf[...] = jnp.zeros_like(zbuf)
        for j in range(n_spare // STEP_PAGES):
            first = ys_ref.shape[0] - n_spare + j * STEP_PAGES
            cp = pltpu.make_async_copy(zbuf, ys_ref.at[pl.ds(first, STEP_PAGES)], out_sems.at[1])
            cp.start()
            cp.wait()

        @pl.when(is_expert_step(0))
        def _():
            for cp in in_copies(0, 0):
                cp.start()

    @pl.when(jnp.logical_and(g >= 2, g - 2 < live))
    def _():
        for cp in out_copies(g - 2, range(STEP_PAGES)):
            cp.wait()

    group = ex_ref[g]
    is_expert = is_expert_step(g)
    new_expert = jnp.logical_or(g == 0, group != ex_ref[jnp.maximum(g - 1, 0)])

    @pl.when(jnp.logical_and(new_expert, is_expert))
    def _():
        wgu_bf[...] = wgu_ref[...].astype(BF16)
        wd_bf[...] = wd_ref[...].astype(BF16)

    @pl.when(is_expert)
    def _():
        for cp in in_copies(g, slot):
            cp.wait()

        @pl.when(is_expert_step(g + 1))
        def _():
            for cp in in_copies(g + 1, 1 - slot):
                cp.start()

        gu = jnp.dot(xbuf[slot], wgu_bf[...], preferred_element_type=F32)
        act = (_silu(gu[:, :ff]) * gu[:, ff:]).astype(BF16)
        for hlf in range(2):
            rows = slice(hlf * half_rows, (hlf + 1) * half_rows)
            ybuf[slot, rows, :] = jnp.dot(act[rows, :], wd_bf[...],
                                          preferred_element_type=F32).astype(BF16)
            for cp in out_copies(g, range(hlf * STEP_PAGES // 2, (hlf + 1) * STEP_PAGES // 2)):
                cp.start()

    @pl.when(jnp.logical_and(g < live, group == N_EXPERTS))
    def _():
        for cp in out_copies(g, range(STEP_PAGES), zeros=True):
            cp.start()


def _experts(xs_pages, plan, w_gu, w_down, layer, n_steps):
    n_pages, _, d = xs_pages.shape
    two_ff = w_gu.shape[-1]
    ff = two_ff // 2
    in_pages, out_pages, ex, live = plan
    grid_spec = pltpu.PrefetchScalarGridSpec(
        num_scalar_prefetch=4,
        grid=(n_steps,),
        in_specs=[
            pl.BlockSpec(memory_space=pl.ANY),
            pl.BlockSpec((None, None, d, two_ff),
                         lambda g, ip, op, e, nl: (layer, jnp.minimum(e[g], N_EXPERTS - 1), 0, 0)),
            pl.BlockSpec((None, None, ff, d),
                         lambda g, ip, op, e, nl: (layer, jnp.minimum(e[g], N_EXPERTS - 1), 0, 0)),
        ],
        out_specs=pl.BlockSpec(memory_space=pl.ANY),
        scratch_shapes=[pltpu.VMEM((d, two_ff), BF16), pltpu.VMEM((ff, d), BF16),
                        pltpu.VMEM((2, STEP_PAGES * PAGE, d), BF16),
                        pltpu.VMEM((2, STEP_PAGES * PAGE, d), BF16),
                        pltpu.VMEM((STEP_PAGES, PAGE, d), BF16),
                        pltpu.SemaphoreType.DMA((2,)),
                        pltpu.SemaphoreType.DMA((2,))],
    )
    return pl.pallas_call(
        _expert_kernel,
        out_shape=jax.ShapeDtypeStruct((n_pages + 2 * STEP_PAGES, PAGE, d), BF16),
        grid_spec=grid_spec,
        compiler_params=_params(("arbitrary",)),
        name="experts",
    )(in_pages, out_pages, ex, live, xs_pages, w_gu, w_down)


def _combine_kernel(ys_ref, meta_ref, x_ref, mod_ref, fg_ref, o_ref):
    for c in range(meta_ref.shape[0]):
        ts = slice(c * SORT_TILE, (c + 1) * SORT_TILE)
        y = _unsort(ys_ref[c * SLOTS:(c + 1) * SLOTS, :], meta_ref[c])
        o_ref[ts, :] = _rms(x_ref[ts, :] + mod_ref[5:6, :] * y, fg_ref[...])


def _final_combine(ys_rows, meta, x2, mod_l, final_g, seq, tm=1024):
    t, d = x2.shape
    per_b = seq // tm
    sub = tm // SORT_TILE
    return pl.pallas_call(
        _combine_kernel,
        out_shape=jax.ShapeDtypeStruct((t, d), F32),
        grid=(t // tm,),
        in_specs=[
            pl.BlockSpec((sub * SLOTS, d), lambda i: (i, 0)),
            pl.BlockSpec((sub, META_ROWS, SORT_TILE), lambda i: (i, 0, 0)),
            pl.BlockSpec((tm, d), lambda i: (i, 0)),
            pl.BlockSpec((None, 6, d), lambda i: (i // per_b, 0, 0)),
            pl.BlockSpec((1, d), lambda i: (0, 0)),
        ],
        out_specs=pl.BlockSpec((tm, d), lambda i: (i, 0)),
        compiler_params=_params(("parallel",)),
        name="moe_combine_final_norm",
    )(ys_rows, meta, x2, mod_l, final_g.reshape(1, d))


def _moe_experts(xs, counts, w_gu, w_down, layer):
    d = xs.shape[1]
    n_tiles = counts.shape[0]
    n_pages = n_tiles * PAGES_PER_TILE
    n_steps = n_pages // STEP_PAGES + N_EXPERTS + 1 + 2
    plan = _expert_plan(counts[:, :, 0].astype(jnp.int32), n_steps)
    ys = _experts(xs.reshape(n_pages, PAGE, d), plan, w_gu, w_down, layer, n_steps)
    return ys.reshape(-1, d)


def kernel(x, c, norm1_g, norm2_g, ada_w, ada_b, ev_w_in, ev_w_out, sgu_ln_g, sgu_ln_b, sgu_w, sgu_b, lam_q1, lam_k1, lam_q2, lam_k2, diff_subln_g, od_w_in, od_w_out, conv_w, conv_b, conv_ln_g, conv_ln_b, router_w, router_b, moe_w_gu, moe_w_down, final_g):
    bsz, seq, d = x.shape
    depth = ada_w.shape[0]
    half = d // 2
    diff_heads = half // (2 * HEAD_DIM)
    dil_heads = half // HEAD_DIM
    t = bsz * seq

    mod = _modulation(c, ada_w, ada_b)
    rw_hi = router_w.astype(BF16)
    rw_lo = (router_w - rw_hi.astype(F32)).astype(BF16)
    rw_split = jnp.pad(jnp.concatenate([rw_hi, rw_lo], axis=1),
                       ((0, 0), (0, LANES - 2 * N_EXPERTS)))

    x2 = x.reshape(t, d)
    moe = None
    for l in range(depth):
        i = l // 2
        w_in = (ev_w_in if l % 2 == 0 else od_w_in)[i].astype(BF16)
        x2, proj = _in_projection(x2, mod[l], norm1_g[l], w_in, seq, moe)
        if l % 2 == 0:
            lam_init = 0.8 - 0.6 * math.exp(-0.3 * l)
            part_a = _spatial_gating(proj, sgu_ln_g[i], sgu_ln_b[i], sgu_w[i], sgu_b[i], half)
            part_b = _diff_attention(proj.reshape(bsz, seq, -1), lam_q1[i], lam_k1[i], lam_q2[i],
                                     lam_k2[i], diff_subln_g[i], lam_init, diff_heads, 2 * half)
            w_out = ev_w_out[i]
        else:
            proj3 = proj.reshape(bsz, seq, -1)
            part_a = _dilated_attention(proj3, dil_heads).reshape(t, half)
            part_b = _conv_module(proj3, conv_w[i], conv_b[i], conv_ln_g[i], conv_ln_b[i],
                                  (2 * len(DIL_PAIRS) + 1) * half)
            w_out = od_w_out[i]
        x2, xs, meta, counts = _out_projection(part_a, part_b.reshape(t, half), x2, mod[l],
                                               norm2_g[l], w_out.astype(BF16), rw_split,
                                               router_b, seq)
        moe = (_moe_experts(xs, counts, moe_w_gu, moe_w_down, l), meta, mod[l])
    out = _final_combine(moe[0], moe[1], x2, mod[depth - 1], final_g, seq)
    return out.reshape(bsz, seq, d)
```

```python
import functools
import math

import jax
import jax.numpy as jnp
from jax import lax
from jax.experimental import pallas as pl
from jax.experimental.pallas import tpu as pltpu

F32 = jnp.float32
BF16 = jnp.bfloat16

HEAD_DIM = 64
LANES = 128
CHUNK = 128
SGU_GROUPS = 4
DIL_PAIRS = ((128, 1), (512, 4), (2048, 16))
WIN_BLOCK = 128
CONV_K = 31
CONV_SUBLANES = 8
N_EXPERTS = 16
EXPERTS_PER_GROUP = 4
RMS_EPS = 1e-6
LN_EPS = 1e-5
NEG = -1e30
LOG2E = 1.4426950408889634
VMEM_LIMIT = 56 * 1024 * 1024


def _params(sem):
    return pltpu.CompilerParams(dimension_semantics=sem, vmem_limit_bytes=VMEM_LIMIT)


def _gelu(x):
    return x * (0.5 * (1.0 + jnp.tanh(0.7978845608028654 * (x + 0.044715 * (x * x * x)))))


def _silu(x):
    return x * jax.nn.sigmoid(x)


def _rms(x, g):
    ms = jnp.mean(x * x, axis=-1, keepdims=True)
    return x * lax.rsqrt(ms + RMS_EPS) * g


def _layer_norm(x, g, b):
    mu = jnp.mean(x, axis=-1, keepdims=True)
    xc = x - mu
    var = jnp.mean(xc * xc, axis=-1, keepdims=True)
    return xc * lax.rsqrt(var + LN_EPS) * g + b


def _mod_kernel(c_ref, w_ref, b_ref, o_ref):
    ca = _silu(c_ref[...]).astype(BF16)
    o_ref[...] = jnp.dot(ca, w_ref[...].astype(BF16), preferred_element_type=F32) + b_ref[...]


def _modulation(c, ada_w, ada_b):
    depth, d, six_d = ada_w.shape
    bsz = c.shape[0]
    n = six_d // d
    out = pl.pallas_call(
        _mod_kernel,
        out_shape=jax.ShapeDtypeStruct((depth, bsz, six_d), F32),
        grid=(depth, n),
        in_specs=[
            pl.BlockSpec((bsz, d), lambda l, j: (0, 0)),
            pl.BlockSpec((None, d, d), lambda l, j: (l, 0, j)),
            pl.BlockSpec((None, 1, d), lambda l, j: (l, 0, j)),
        ],
        out_specs=pl.BlockSpec((None, bsz, d), lambda l, j: (l, 0, j)),
        compiler_params=_params(("parallel", "parallel")),
        name="modulation",
    )(c, ada_w, ada_b.reshape(depth, 1, six_d))
    return out.reshape(depth, bsz, n, d)


SORT_TILE = 256
PAGE = 16
TOP_K = 2
SLOTS = TOP_K * SORT_TILE + N_EXPERTS * PAGE
PAGES_PER_TILE = SLOTS // PAGE
STEP_PAGES = 32
META_ROWS = 8


def _unsort(ys, meta):
    pos0 = meta[0:1, :].astype(jnp.int32)
    pos1 = meta[1:2, :].astype(jnp.int32)
    slot = lax.broadcasted_iota(jnp.int32, (SLOTS, meta.shape[1]), 0)
    w = (jnp.where(slot == pos0, meta[2:3, :], 0.0)
         + jnp.where(slot == pos1, meta[3:4, :], 0.0)).astype(BF16)
    return lax.dot_general(w, ys, (((0,), (0,)), ((), ())), preferred_element_type=F32)


def _project(x, mod_ref, g_ref, w_ref, o_ref, n_chunk):
    h = _rms(x, g_ref[...]) * (1.0 + mod_ref[1:2, :]) + mod_ref[0:1, :]
    hb = h.astype(BF16)
    for j in range(0, o_ref.shape[-1], n_chunk):
        o_ref[:, j:j + n_chunk] = jnp.dot(
            hb, w_ref[:, j:j + n_chunk], preferred_element_type=F32).astype(o_ref.dtype)


def _inproj_kernel(x_ref, mod_ref, g_ref, w_ref, o_ref, *, n_chunk):
    _project(x_ref[...], mod_ref, g_ref, w_ref, o_ref, n_chunk)


def _inproj_moe_kernel(ys_ref, meta_ref, x_ref, pmod_ref, mod_ref, g_ref, w_ref, xo_ref, o_ref,
                       *, n_chunk):
    for c in range(meta_ref.shape[0]):
        ts = slice(c * SORT_TILE, (c + 1) * SORT_TILE)
        y = _unsort(ys_ref[c * SLOTS:(c + 1) * SLOTS, :], meta_ref[c])
        xo_ref[ts, :] = x_ref[ts, :] + pmod_ref[5:6, :] * y
    _project(xo_ref[...], mod_ref, g_ref, w_ref, o_ref, n_chunk)


def _in_projection(x2, mod_l, g, w_bf, seq, moe=None, tm=512):
    t, d = x2.shape
    n = w_bf.shape[1]
    per_b = seq // tm
    sub = tm // SORT_TILE
    row_spec = pl.BlockSpec((tm, d), lambda i: (i, 0))
    mod_spec = pl.BlockSpec((None, 6, d), lambda i: (i // per_b, 0, 0))
    tail_specs = [mod_spec, pl.BlockSpec((1, d), lambda i: (0, 0)),
                  pl.BlockSpec((d, n), lambda i: (0, 0))]
    proj_shape = jax.ShapeDtypeStruct((t, n), BF16)
    proj_spec = pl.BlockSpec((tm, n), lambda i: (i, 0))
    if moe is None:
        return x2, pl.pallas_call(
            functools.partial(_inproj_kernel, n_chunk=512),
            out_shape=proj_shape,
            grid=(t // tm,),
            in_specs=[row_spec] + tail_specs,
            out_specs=proj_spec,
            compiler_params=_params(("parallel",)),
            name="in_projection",
        )(x2, mod_l, g.reshape(1, d), w_bf)
    ys_rows, meta, prev_mod = moe
    return pl.pallas_call(
        functools.partial(_inproj_moe_kernel, n_chunk=512),
        out_shape=(jax.ShapeDtypeStruct((t, d), F32), proj_shape),
        grid=(t // tm,),
        in_specs=[pl.BlockSpec((sub * SLOTS, d), lambda i: (i, 0)),
                  pl.BlockSpec((sub, META_ROWS, SORT_TILE), lambda i: (i, 0, 0)),
                  row_spec, mod_spec] + tail_specs,
        out_specs=(row_spec, proj_spec),
        compiler_params=_params(("parallel",)),
        name="moe_combine_in_projection",
    )(ys_rows, meta, x2, prev_mod, mod_l, g.reshape(1, d), w_bf)


def _sgu_kernel(u_ref, v_ref, lng_ref, lnb_ref, w_ref, bias_ref, o_ref):
    u = _gelu(u_ref[...].astype(F32))
    v = _gelu(v_ref[...].astype(F32))
    vb = _layer_norm(v, lng_ref[...], lnb_ref[...]).astype(BF16)
    row = lax.broadcasted_iota(jnp.int32, (CHUNK, CHUNK), 0)
    col = lax.broadcasted_iota(jnp.int32, (CHUNK, CHUNK), 1)
    causal = col <= row
    gd = u.shape[1] // SGU_GROUPS
    n_chunks = u.shape[0] // CHUNK
    for g in range(SGU_GROUPS):
        w = jnp.where(causal, w_ref[g], 0.0).astype(BF16)
        cs = slice(g * gd, (g + 1) * gd)
        rhs = jnp.concatenate([vb[c * CHUNK:(c + 1) * CHUNK, cs] for c in range(n_chunks)], axis=1)
        mixed = jnp.dot(w, rhs, preferred_element_type=F32)
        for c in range(n_chunks):
            rs = slice(c * CHUNK, (c + 1) * CHUNK)
            o_ref[rs, cs] = (u[rs, cs] * (mixed[:, c * gd:(c + 1) * gd] + bias_ref[:, cs])
                             ).astype(o_ref.dtype)


def _spatial_gating(proj, ln_g, ln_b, w_s, b_s, width, tm=1024):
    t = proj.shape[0]
    gd = width // SGU_GROUPS
    bias = jnp.repeat(b_s.T, gd, axis=1)
    return pl.pallas_call(
        _sgu_kernel,
        out_shape=jax.ShapeDtypeStruct((t, width), BF16),
        grid=(t // tm,),
        in_specs=[
            pl.BlockSpec((tm, width), lambda i: (i, 0)),
            pl.BlockSpec((tm, width), lambda i: (i, 1)),
            pl.BlockSpec((1, width), lambda i: (0, 0)),
            pl.BlockSpec((1, width), lambda i: (0, 0)),
            pl.BlockSpec((SGU_GROUPS, CHUNK, CHUNK), lambda i: (0, 0, 0)),
            pl.BlockSpec((CHUNK, width), lambda i: (0, 0)),
        ],
        out_specs=pl.BlockSpec((tm, width), lambda i: (i, 0)),
        compiler_params=_params(("parallel",)),
        name="spatial_gating",
    )(proj, proj, ln_g.reshape(1, width), ln_b.reshape(1, width), w_s, bias)


ONES_ROWS = 16


def _split3(x):
    hi = x.astype(BF16).astype(F32)
    r1 = x - hi
    mid = r1.astype(BF16).astype(F32)
    lo = (r1 - mid).astype(BF16).astype(F32)
    return hi, mid, lo


def _diff_kernel(slopes_ref, lq1_ref, lk1_ref, lq2_ref, lk2_ref, q_ref, k_ref, v_ref, g_ref,
                 o_ref, kx, qx, vt, acc, *, tq, lam_init):
    h = pl.program_id(1)
    seq, vd = v_ref.shape
    n_q = seq // tq
    hd = HEAD_DIM
    lane = lax.broadcasted_iota(jnp.int32, (tq, LANES), 1)

    def place(pieces, base, fill):
        out = fill
        for n, piece in enumerate(pieces):
            out = jnp.where(lane == base + n, piece, out)
        return out

    slope = slopes_ref[h] * LOG2E
    one = jnp.ones((tq, LANES), F32)
    for c in range(n_q):
        rs = slice(c * tq, (c + 1) * tq)
        k = k_ref[rs, :].astype(F32)
        pos = (lax.broadcasted_iota(jnp.int32, (tq, LANES), 0) + c * tq).astype(F32) * slope
        pieces = _split3(pos)
        zero = jnp.zeros_like(k)
        kx[0, rs, :] = jnp.where(lane < hd, k, place(pieces, hd, zero)).astype(BF16)
        kx[1, rs, :] = jnp.where(lane >= hd, k, place(pieces, 0, zero)).astype(BF16)
        vt[0:vd, rs] = v_ref[rs, :].astype(F32).T.astype(BF16)
        vt[vd:, rs] = jnp.ones((ONES_ROWS, tq), BF16)
        q = q_ref[rs, :].astype(F32) * (hd ** -0.5 * LOG2E)
        qx[0, rs, :] = jnp.where(lane < hd, q, place((one, one, one), hd, zero)).astype(BF16)
        qx[1, rs, :] = jnp.where(lane >= hd, q, place((one, one, one), 0, zero)).astype(BF16)

    kc = tq
    chunks = {qi: [(k0, min(k0 + kc, (qi + 1) * tq)) for k0 in range(0, (qi + 1) * tq, kc)]
              for qi in range(n_q)}
    ms = {}
    for c in range(max(len(v) for v in chunks.values())):
        for qi in range(n_q):
            if c >= len(chunks[qi]):
                continue
            k0, k1 = chunks[qi][c]
            for m in range(2):
                s = lax.dot_general(kx[m, k0:k1, :], qx[m, qi * tq:(qi + 1) * tq, :],
                                    (((1,), (1,)), ((), ())),
                                    preferred_element_type=F32)
                if k1 > qi * tq + 1:
                    key_i = lax.broadcasted_iota(jnp.int32, s.shape, 0) + k0
                    qry_i = lax.broadcasted_iota(jnp.int32, s.shape, 1) + qi * tq
                    s = jnp.where(key_i <= qry_i, s, NEG)
                m_tile = jnp.max(s, axis=0, keepdims=True)
                if c == 0:
                    ms[qi, m] = m_tile
                    p = jnp.exp2(s - m_tile).astype(BF16)
                    acc[qi, m] = jnp.dot(vt[:, k0:k1], p, preferred_element_type=F32)
                else:
                    m_new = jnp.maximum(ms[qi, m], m_tile)
                    alpha = jnp.exp2(ms[qi, m] - m_new)
                    ms[qi, m] = m_new
                    p = jnp.exp2(s - m_new).astype(BF16)
                    acc[qi, m] = (alpha * acc[qi, m]
                                  + jnp.dot(vt[:, k0:k1], p, preferred_element_type=F32))

    lam = (jnp.exp(jnp.sum(lq1_ref[...] * lk1_ref[...], axis=-1, keepdims=True))
           - jnp.exp(jnp.sum(lq2_ref[...] * lk2_ref[...], axis=-1, keepdims=True)) + lam_init)
    for qi in range(n_q):
        o_t = (acc[qi, 0, 0:vd, :] / acc[qi, 0, vd:vd + 1, :]
               - lam * (acc[qi, 1, 0:vd, :] / acc[qi, 1, vd:vd + 1, :]))
        o_ref[qi * tq:(qi + 1) * tq, :] = (
            _rms(o_t.T, g_ref[...]) * (1.0 - lam_init)).astype(o_ref.dtype)


def _diff_attention(proj3, lq1, lk1, lq2, lk2, subln_g, lam_init, n_heads, col0, tq=1024):
    bsz, seq, _ = proj3.shape
    slopes = 2.0 ** (-8.0 * jnp.arange(1, n_heads + 1, dtype=F32) / n_heads)
    qb, kb, vb = col0 // LANES, col0 // LANES + n_heads, col0 // LANES + 2 * n_heads
    vec = lambda a: a.reshape(1, -1)
    small = lambda n: pl.BlockSpec((1, n), lambda b, h: (0, 0))
    return pl.pallas_call(
        functools.partial(_diff_kernel, tq=tq, lam_init=lam_init),
        out_shape=jax.ShapeDtypeStruct((bsz, seq, n_heads * LANES), BF16),
        grid=(bsz, n_heads),
        in_specs=[
            pl.BlockSpec(memory_space=pltpu.SMEM),
            small(HEAD_DIM), small(HEAD_DIM), small(HEAD_DIM), small(HEAD_DIM),
            pl.BlockSpec((None, seq, LANES), lambda b, h: (b, 0, qb + h)),
            pl.BlockSpec((None, seq, LANES), lambda b, h: (b, 0, kb + h)),
            pl.BlockSpec((None, seq, LANES), lambda b, h: (b, 0, vb + h)),
            small(LANES),
        ],
        out_specs=pl.BlockSpec((None, seq, LANES), lambda b, h: (b, 0, h)),
        scratch_shapes=[
            pltpu.VMEM((2, seq, LANES), BF16),
            pltpu.VMEM((2, seq, LANES), BF16),
            pltpu.VMEM((LANES + ONES_ROWS, seq), BF16),
            pltpu.VMEM((seq // tq, 2, LANES + ONES_ROWS, tq), F32),
        ],
        compiler_params=_params(("parallel", "parallel")),
        name="diff_attention",
    )(slopes, vec(lq1), vec(lk1), vec(lq2), vec(lk2), proj3, proj3, proj3, vec(subln_g))


def _dil_kernel(slopes_ref, q1_ref, q2_ref, q3_ref, k1_ref, k2_ref, k3_ref, v_ref, o_ref,
                qf, kf, vf, bias_s, acc_s, m_s, l_s, *, seq):
    hp = pl.program_id(1)
    wb = WIN_BLOCK
    hd = HEAD_DIM
    n_q_blocks = seq // wb

    qf[0] = q2_ref[...].astype(F32)
    qf[1] = q3_ref[...].astype(F32)
    kf[0] = k2_ref[...].astype(F32)
    kf[1] = k3_ref[...].astype(F32)
    vf[...] = v_ref[...].astype(F32)

    row = lax.broadcasted_iota(jnp.int32, (2 * wb, 2 * wb), 0)
    col = lax.broadcasted_iota(jnp.int32, (2 * wb, 2 * wb), 1)
    qi_ = jnp.where(row < wb, row, row - wb)
    dist = qi_ + wb - col
    slope = jnp.where(row < wb, slopes_ref[2 * hp], slopes_ref[2 * hp + 1])
    valid = jnp.logical_and(dist >= 0, dist <= wb)
    table = jnp.where(valid, -(slope * LOG2E) * dist.astype(F32), NEG)
    bias_s[1] = table
    bias_s[0] = jnp.where(col < wb, NEG, table)

    first = lax.broadcasted_iota(jnp.int32, (wb, LANES), 1) < hd
    q_scale = hd ** -0.5 * LOG2E

    for g, (window, dil) in enumerate(DIL_PAIRS):
        assert window // dil == wb
        nb = n_q_blocks // dil
        refs = {"q": (q1_ref, qf), "k": (k1_ref, kf), "v": (v_ref, vf)}

        def rows(kind, start, g=g, dil=dil, refs=refs):
            direct, copies = refs[kind]
            if dil == 1:
                return direct[pl.ds(pl.multiple_of(start, wb), wb), :]
            src = copies if kind == "v" else copies.at[g - 1]
            return src[pl.ds(start, wb, stride=dil), :]

        def body(idx, _, g=g, dil=dil, nb=nb, rows=rows):
            r = idx // nb
            n = idx % nb
            cur = n * (wb * dil) + r
            q = rows("q", cur).astype(F32) * q_scale
            zero = jnp.zeros_like(q)
            qx = jnp.concatenate([jnp.where(first, q, zero), jnp.where(first, zero, q)],
                                 axis=0).astype(BF16)
            if nb > 1:
                prev = jnp.maximum(n - 1, 0) * (wb * dil) + r
                keys = jnp.concatenate([rows("k", prev), rows("k", cur)], axis=0).astype(BF16)
                vals = jnp.concatenate([rows("v", prev), rows("v", cur)], axis=0).astype(BF16)
                bias = bias_s[jnp.minimum(n, 1)]
            else:
                keys, vals = rows("k", cur).astype(BF16), rows("v", cur).astype(BF16)
                bias = bias_s[1, :, wb:]
            vx = jnp.concatenate([vals, jnp.ones_like(vals)], axis=1)
            s = lax.dot_general(qx, keys, (((1,), (1,)), ((), ())),
                                preferred_element_type=F32) + bias
            m = jnp.max(s, axis=-1, keepdims=True)
            p = jnp.exp2(s - m).astype(BF16)
            o = jnp.dot(p, vx, preferred_element_type=F32)
            acc = jnp.where(first, o[0:wb, 0:LANES], o[wb:, 0:LANES])
            den = jnp.where(first, o[0:wb, LANES:], o[wb:, LANES:])
            mx = jnp.where(first, jnp.broadcast_to(m[0:wb], (wb, LANES)),
                           jnp.broadcast_to(m[wb:], (wb, LANES)))
            if dil == 1:
                dst = pl.ds(pl.multiple_of(cur, wb), wb)
            else:
                dst = pl.ds(cur, wb, stride=dil)
            acc_s[g, dst, :] = acc
            m_s[g, dst, :] = mx
            l_s[g, dst, :] = den
            return 0

        lax.fori_loop(0, n_q_blocks, body, 0, unroll=16)

    mr = 256
    for c in range(seq // mr):
        rs = slice(c * mr, (c + 1) * mr)
        m1, m2, m3 = m_s[0, rs, :], m_s[1, rs, :], m_s[2, rs, :]
        m = jnp.maximum(jnp.maximum(m1, m2), m3)
        w1, w2, w3 = jnp.exp2(m1 - m), jnp.exp2(m2 - m), jnp.exp2(m3 - m)
        num = w1 * acc_s[0, rs, :] + w2 * acc_s[1, rs, :] + w3 * acc_s[2, rs, :]
        den = w1 * l_s[0, rs, :] + w2 * l_s[1, rs, :] + w3 * l_s[2, rs, :]
        o_ref[rs, :] = (num / den).astype(o_ref.dtype)


def _dilated_attention(proj3, n_heads):
    bsz, seq, _ = proj3.shape
    n_pairs = n_heads * HEAD_DIM // LANES
    slopes = 2.0 ** (-8.0 * jnp.arange(1, n_heads + 1, dtype=F32) / n_heads)
    groups = len(DIL_PAIRS)

    def spec(blk0):
        return pl.BlockSpec((None, seq, LANES), lambda b, p: (b, 0, blk0 + p))

    q_specs = [spec(g * n_pairs) for g in range(groups)]
    k_specs = [spec((groups + g) * n_pairs) for g in range(groups)]
    v_spec = spec(2 * groups * n_pairs)
    return pl.pallas_call(
        functools.partial(_dil_kernel, seq=seq),
        out_shape=jax.ShapeDtypeStruct((bsz, seq, n_pairs * LANES), BF16),
        grid=(bsz, n_pairs),
        in_specs=[pl.BlockSpec(memory_space=pltpu.SMEM)] + q_specs + k_specs + [v_spec],
        out_specs=pl.BlockSpec((None, seq, LANES), lambda b, p: (b, 0, p)),
        scratch_shapes=[
            pltpu.VMEM((groups - 1, seq, LANES), F32),
            pltpu.VMEM((groups - 1, seq, LANES), F32),
            pltpu.VMEM((seq, LANES), F32),
            pltpu.VMEM((2, 2 * WIN_BLOCK, 2 * WIN_BLOCK), F32),
            pltpu.VMEM((groups, seq, LANES), F32),
            pltpu.VMEM((groups, seq, LANES), F32),
            pltpu.VMEM((groups, seq, LANES), F32),
        ],
        compiler_params=_params(("parallel", "parallel")),
        name="dilated_attention",
    )(slopes, *([proj3] * 7))


CONV_HALO = 32
CONV_ROWS = 64


def _conv_rows(buf, win, w_ref, cb_ref, lng_ref, lnb_ref, row0, rows):
    off = CONV_HALO - (CONV_K - 1)
    sub = CONV_SUBLANES
    rc = CONV_ROWS
    outs = []
    for c in range(rows // rc):
        base = row0 + c * rc + off
        acc = jnp.zeros((rc, buf.shape[1]), F32)
        for r in range(sub):
            span = rc + (CONV_K - 1 - r) // sub * sub
            win[c % 2, r, 0:span, :] = buf[base + r:base + r + span, :]
            for k in range(r, CONV_K, sub):
                acc = acc + w_ref[k:k + 1, :] * win[c % 2, r, k - r:k - r + rc, :]
        y = _layer_norm(acc + cb_ref[...], lng_ref[...], lnb_ref[...])
        outs.append(_silu(y).astype(BF16))
    return jnp.concatenate(outs, axis=0)


def _route(logits_t, rb):
    mx = jnp.max(logits_t, axis=0, keepdims=True)
    ex = jnp.exp(logits_t - mx)
    probs = ex / jnp.sum(ex, axis=0, keepdims=True)
    sel = probs + rb
    srow = [sel[i:i + 1, :] for i in range(N_EXPERTS)]
    prow = [probs[i:i + 1, :] for i in range(N_EXPERTS)]
    epg = EXPERTS_PER_GROUP
    n_grp = N_EXPERTS // epg

    scores = []
    for g in range(n_grp):
        a, b, c, d = srow[g * epg:(g + 1) * epg]
        hi1, lo1 = jnp.maximum(a, b), jnp.minimum(a, b)
        hi2, lo2 = jnp.maximum(c, d), jnp.minimum(c, d)
        scores.append(jnp.maximum(hi1, hi2)
                      + jnp.maximum(jnp.minimum(hi1, hi2), jnp.maximum(lo1, lo2)))
    best = scores[0]
    grp = jnp.zeros(best.shape, jnp.int32)
    for g in range(1, n_grp):
        better = scores[g] > best
        grp = jnp.where(better, g, grp)
        best = jnp.where(better, scores[g], best)

    def pick(rows_, j):
        out = rows_[j]
        for g in range(1, n_grp):
            out = jnp.where(grp == g, rows_[g * epg + j], out)
        return out

    ing = [pick(srow, j) for j in range(epg)]
    ping = [pick(prow, j) for j in range(epg)]
    b0, i0, p0 = ing[0], jnp.zeros(best.shape, jnp.int32), ping[0]
    for j in range(1, epg):
        better = ing[j] > b0
        i0 = jnp.where(better, j, i0)
        p0 = jnp.where(better, ping[j], p0)
        b0 = jnp.where(better, ing[j], b0)
    b1 = jnp.full(best.shape, -jnp.inf, F32)
    i1 = jnp.zeros(best.shape, jnp.int32)
    p1 = jnp.zeros(best.shape, F32)
    for j in range(epg):
        better = jnp.logical_and(i0 != j, ing[j] > b1)
        i1 = jnp.where(better, j, i1)
        p1 = jnp.where(better, ping[j], p1)
        b1 = jnp.where(better, ing[j], b1)
    den = p0 + p1
    return grp * epg + i0, grp * epg + i1, p0 / den, p1 / den


def _sort_tile(e0, e1, hb):
    st = hb.shape[0]
    eid = lax.broadcasted_iota(jnp.int32, (N_EXPERTS, st), 0)
    sel0 = jnp.where(eid == e0, 1.0, 0.0)
    sel1 = jnp.where(eid == e1, 1.0, 0.0)
    sel = sel0 + sel1
    before = (lax.broadcasted_iota(jnp.int32, (st, st), 0)
              < lax.broadcasted_iota(jnp.int32, (st, st), 1))
    rank = jnp.dot(sel.astype(BF16), jnp.where(before, 1.0, 0.0).astype(BF16),
                   preferred_element_type=F32)
    cnt = jnp.sum(sel, axis=1, keepdims=True)
    padded = jnp.ceil(cnt * (1.0 / PAGE)) * PAGE
    ecol = lax.broadcasted_iota(jnp.int32, (N_EXPERTS, 1), 0)
    seg = jnp.zeros((N_EXPERTS, 1), F32)
    run = jnp.zeros((1, 1), F32)
    for e in range(N_EXPERTS):
        seg = jnp.where(ecol == e, run, seg)
        run = run + padded[e:e + 1, :]
    base = seg + rank
    pos0 = jnp.sum(sel0 * base, axis=0, keepdims=True)
    pos1 = jnp.sum(sel1 * base, axis=0, keepdims=True)
    slot = lax.broadcasted_iota(jnp.int32, (SLOTS, st), 0)
    perm = (jnp.where(slot == pos0.astype(jnp.int32), 1.0, 0.0)
            + jnp.where(slot == pos1.astype(jnp.int32), 1.0, 0.0)).astype(BF16)
    rows = jnp.dot(perm, hb, preferred_element_type=F32).astype(BF16)
    return pos0, pos1, cnt, rows


def _outproj_tiles(a_ref, b_tile, x_ref, mod_ref, g_ref, w_ref, rw_ref, rb_ref,
                   xo_ref, xs_ref, meta_ref, cnt_ref):
    ka = a_ref.shape[1]
    for c in range(x_ref.shape[0] // SORT_TILE):
        ts = slice(c * SORT_TILE, (c + 1) * SORT_TILE)
        y = (jnp.dot(a_ref[ts, :], w_ref[0:ka, :], preferred_element_type=F32)
             + jnp.dot(b_tile(c), w_ref[ka:, :], preferred_element_type=F32))
        xn = x_ref[ts, :] + mod_ref[2:3, :] * y
        xo_ref[ts, :] = xn
        h = _rms(xn, g_ref[...]) * (1.0 + mod_ref[4:5, :]) + mod_ref[3:4, :]
        hb = h.astype(BF16)
        hl = (h - hb.astype(F32)).astype(BF16)
        parts = (jnp.dot(hb, rw_ref[...], preferred_element_type=F32)
                 + jnp.dot(hl, rw_ref[...], preferred_element_type=F32)).T
        lt = parts[0:N_EXPERTS, :] + parts[N_EXPERTS:2 * N_EXPERTS, :]
        e0, e1, g0, g1 = _route(lt, rb_ref[...])
        pos0, pos1, cnt, rows = _sort_tile(e0, e1, hb)
        xs_ref[c * SLOTS:(c + 1) * SLOTS, :] = rows
        meta_ref[c] = jnp.concatenate(
            [pos0, pos1, g0, g1, jnp.zeros((META_ROWS - 4, SORT_TILE), F32)], axis=0)
        cnt_ref[c] = jnp.broadcast_to(cnt, (N_EXPERTS, LANES))


def _outproj_kernel(a_ref, b_ref, *rest):
    _outproj_tiles(a_ref, lambda c: b_ref[c * SORT_TILE:(c + 1) * SORT_TILE, :], *rest)


def _outproj_conv_kernel(a_ref, ca_ref, cg_ref, cw_ref, cb_ref, clg_ref, clb_ref, *rest,
                         per_b):
    *rest, buf, win = rest
    tm = ca_ref.shape[0]

    @pl.when(pl.program_id(0) % per_b == 0)
    def _():
        buf[0:CONV_HALO, :] = jnp.zeros((CONV_HALO, buf.shape[1]), F32)

    buf[CONV_HALO:, :] = ca_ref[...].astype(F32) * jax.nn.sigmoid(cg_ref[...].astype(F32))
    _outproj_tiles(
        a_ref,
        lambda c: _conv_rows(buf, win, cw_ref, cb_ref, clg_ref, clb_ref, c * SORT_TILE, SORT_TILE),
        *rest)
    buf[0:CONV_HALO, :] = buf[tm:tm + CONV_HALO, :]


def _out_projection(a, b, x2, mod_l, g, w_bf, rw_split, rb, seq, conv=None, tm=1024):
    t, d = x2.shape
    ka = a.shape[1]
    kb = w_bf.shape[0] - ka
    per_b = seq // tm
    sub = tm // SORT_TILE
    n_tiles = t // SORT_TILE
    if conv is None:
        kernel_fn, sem, scratch = _outproj_kernel, "parallel", []
        b_args = [b]
        b_specs = [pl.BlockSpec((tm, kb), lambda i: (i, 0))]
    else:
        proj, col0, cw, cb, clg, clb = conv
        vec = lambda v: v.reshape(1, kb)
        small = pl.BlockSpec((1, kb), lambda i: (0, 0))
        kernel_fn = functools.partial(_outproj_conv_kernel, per_b=per_b)
        sem = "arbitrary"
        scratch = [pltpu.VMEM((tm + CONV_HALO, kb), F32),
                   pltpu.VMEM((2, CONV_SUBLANES, CONV_ROWS + CONV_HALO, kb), F32)]
        b_args = [proj, proj, cw, vec(cb), vec(clg), vec(clb)]
        b_specs = [pl.BlockSpec((tm, kb), lambda i: (i, col0 // kb)),
                   pl.BlockSpec((tm, kb), lambda i: (i, col0 // kb + 1)),
                   pl.BlockSpec((CONV_K, kb), lambda i: (0, 0)), small, small, small]
    return pl.pallas_call(
        kernel_fn,
        out_shape=(jax.ShapeDtypeStruct((t, d), F32),
                   jax.ShapeDtypeStruct((n_tiles * SLOTS, d), BF16),
                   jax.ShapeDtypeStruct((n_tiles, META_ROWS, SORT_TILE), F32),
                   jax.ShapeDtypeStruct((n_tiles, N_EXPERTS, LANES), F32)),
        grid=(t // tm,),
        in_specs=[pl.BlockSpec((tm, ka), lambda i: (i, 0))] + b_specs + [
            pl.BlockSpec((tm, d), lambda i: (i, 0)),
            pl.BlockSpec((None, 6, d), lambda i: (i // per_b, 0, 0)),
            pl.BlockSpec((1, d), lambda i: (0, 0)),
            pl.BlockSpec((ka + kb, d), lambda i: (0, 0)),
            pl.BlockSpec((d, LANES), lambda i: (0, 0)),
            pl.BlockSpec((N_EXPERTS, 1), lambda i: (0, 0)),
        ],
        out_specs=(pl.BlockSpec((tm, d), lambda i: (i, 0)),
                   pl.BlockSpec((sub * SLOTS, d), lambda i: (i, 0)),
                   pl.BlockSpec((sub, META_ROWS, SORT_TILE), lambda i: (i, 0, 0)),
                   pl.BlockSpec((sub, N_EXPERTS, LANES), lambda i: (i, 0, 0))),
        scratch_shapes=scratch,
        compiler_params=_params((sem,)),
        name="out_projection" if conv is None else "conv_out_projection",
    )(a, *b_args, x2, mod_l, g.reshape(1, d), w_bf, rw_split, rb.reshape(N_EXPERTS, 1))


def _expert_plan(counts, n_steps_max):
    n_tiles = counts.shape[0]
    n_groups = N_EXPERTS + 1
    i32 = jnp.int32
    npg = (counts + PAGE - 1) // PAGE
    npg = jnp.concatenate([npg, PAGES_PER_TILE - jnp.sum(npg, axis=1, keepdims=True)], axis=1)
    seg = jnp.cumsum(npg, axis=1) - npg
    page_base = jnp.arange(n_tiles, dtype=i32)[:, None] * PAGES_PER_TILE + seg
    cum_t = jnp.cumsum(npg, axis=0)
    tot = cum_t[-1]
    steps_e = (tot + STEP_PAGES - 1) // STEP_PAGES
    step_end = jnp.cumsum(steps_e)
    n_live = step_end[-1]
    g = jnp.arange(n_steps_max, dtype=i32)
    ex = jnp.minimum(jnp.sum((step_end[:, None] <= g[None, :]).astype(i32), axis=0),
                     n_groups - 1)
    sel = jnp.arange(n_groups, dtype=i32)[:, None] == ex[None, :]

    def of_step(per_group):
        return jnp.sum(jnp.where(sel, per_group[:, None], 0), axis=0)

    def rows_of_step(per_tile_group):
        return jnp.sum(jnp.where(sel[None], per_tile_group[:, :, None], 0), axis=1)

    q = ((g - of_step(step_end - steps_e))[:, None] * STEP_PAGES
         + jnp.arange(STEP_PAGES, dtype=i32)[None, :])
    valid = jnp.logical_and(q < of_step(tot)[:, None], (g < n_live)[:, None])
    cum_e = rows_of_step(cum_t)
    ti = jnp.minimum(jnp.sum((cum_e[:, :, None] <= q[None]).astype(i32), axis=0), n_tiles - 1)
    shift = rows_of_step(page_base - (cum_t - npg))
    hit = jnp.arange(n_tiles, dtype=i32)[:, None, None] == ti[None]
    pid = q + jnp.sum(jnp.where(hit, shift[:, :, None], 0), axis=0)
    spare = (n_tiles * PAGES_PER_TILE + (g % 2)[:, None] * STEP_PAGES
             + jnp.arange(STEP_PAGES, dtype=i32)[None, :])
    first = jnp.where(g < n_live, pid[:, 0], pid[0, 0])[:, None]
    in_pages = jnp.where(valid, pid, first).astype(i32).reshape(-1)
    out_pages = jnp.where(valid, pid, spare).astype(i32).reshape(-1)
    return in_pages, out_pages, ex.astype(i32), n_live.astype(i32).reshape(1)


def _expert_kernel(inp_ref, outp_ref, ex_ref, live_ref, xs_ref, wgu_ref, wd_ref, ys_ref,
                   wgu_bf, wd_bf, xbuf, ybuf, zbuf, in_sems, out_sems):
    g = pl.program_id(0)
    live = live_ref[0]
    slot = g % 2
    ff = wd_ref.shape[0]
    n_spare = 2 * STEP_PAGES
    half_rows = STEP_PAGES * PAGE // 2

    def is_expert_step(step):
        group = ex_ref[jnp.minimum(step, pl.num_programs(0) - 1)]
        return jnp.logical_and(step < live, group < N_EXPERTS)

    def in_copies(step, buf):
        return [pltpu.make_async_copy(xs_ref.at[inp_ref[step * STEP_PAGES + k]],
                                      xbuf.at[buf, pl.ds(k * PAGE, PAGE), :], in_sems.at[buf])
                for k in range(STEP_PAGES)]

    def out_copies(step, pages, zeros=False):
        return [pltpu.make_async_copy(
            zbuf.at[k] if zeros else ybuf.at[slot, pl.ds(k * PAGE, PAGE), :],
            ys_ref.at[outp_ref[step * STEP_PAGES + k]], out_sems.at[slot])
                for k in pages]

    @pl.when(g == 0)
    def _():
        zbuf[...] = jnp.zeros_like(zbuf)
        for j in range(n_spare // STEP_PAGES):
            first = ys_ref.shape[0] - n_spare + j * STEP_PAGES
            cp = pltpu.make_async_copy(zbuf, ys_ref.at[pl.ds(first, STEP_PAGES)], out_sems.at[1])
            cp.start()
            cp.wait()

        @pl.when(is_expert_step(0))
        def _():
            for cp in in_copies(0, 0):
                cp.start()

    @pl.when(jnp.logical_and(g >= 2, g - 2 < live))
    def _():
        for cp in out_copies(g - 2, range(STEP_PAGES)):
            cp.wait()

    group = ex_ref[g]
    is_expert = is_expert_step(g)
    new_expert = jnp.logical_or(g == 0, group != ex_ref[jnp.maximum(g - 1, 0)])

    @pl.when(jnp.logical_and(new_expert, is_expert))
    def _():
        wgu_bf[...] = wgu_ref[...].astype(BF16)
        wd_bf[...] = wd_ref[...].astype(BF16)

    @pl.when(is_expert)
    def _():
        for cp in in_copies(g, slot):
            cp.wait()

        @pl.when(is_expert_step(g + 1))
        def _():
            for cp in in_copies(g + 1, 1 - slot):
                cp.start()

        gu = jnp.dot(xbuf[slot], wgu_bf[...], preferred_element_type=F32)
        act = (_silu(gu[:, :ff]) * gu[:, ff:]).astype(BF16)
        for hlf in range(2):
            rows = slice(hlf * half_rows, (hlf + 1) * half_rows)
            ybuf[slot, rows, :] = jnp.dot(act[rows, :], wd_bf[...],
                                          preferred_element_type=F32).astype(BF16)
            for cp in out_copies(g, range(hlf * STEP_PAGES // 2, (hlf + 1) * STEP_PAGES // 2)):
                cp.start()

    @pl.when(jnp.logical_and(g < live, group == N_EXPERTS))
    def _():
        for cp in out_copies(g, range(STEP_PAGES), zeros=True):
            cp.start()


def _experts(xs_pages, plan, w_gu, w_down, layer, n_steps):
    n_pages, _, d = xs_pages.shape
    two_ff = w_gu.shape[-1]
    ff = two_ff // 2
    in_pages, out_pages, ex, live = plan
    grid_spec = pltpu.PrefetchScalarGridSpec(
        num_scalar_prefetch=4,
        grid=(n_steps,),
        in_specs=[
            pl.BlockSpec(memory_space=pl.ANY),
            pl.BlockSpec((None, None, d, two_ff),
                         lambda g, ip, op, e, nl: (layer, jnp.minimum(e[g], N_EXPERTS - 1), 0, 0)),
            pl.BlockSpec((None, None, ff, d),
                         lambda g, ip, op, e, nl: (layer, jnp.minimum(e[g], N_EXPERTS - 1), 0, 0)),
        ],
        out_specs=pl.BlockSpec(memory_space=pl.ANY),
        scratch_shapes=[pltpu.VMEM((d, two_ff), BF16), pltpu.VMEM((ff, d), BF16),
                        pltpu.VMEM((2, STEP_PAGES * PAGE, d), BF16),
                        pltpu.VMEM((2, STEP_PAGES * PAGE, d), BF16),
                        pltpu.VMEM((STEP_PAGES, PAGE, d), BF16),
                        pltpu.SemaphoreType.DMA((2,)),
                        pltpu.SemaphoreType.DMA((2,))],
    )
    return pl.pallas_call(
        _expert_kernel,
        out_shape=jax.ShapeDtypeStruct((n_pages + 2 * STEP_PAGES, PAGE, d), BF16),
        grid_spec=grid_spec,
        compiler_params=_params(("arbitrary",)),
        name="experts",
    )(in_pages, out_pages, ex, live, xs_pages, w_gu, w_down)


def _combine_kernel(ys_ref, meta_ref, x_ref, mod_ref, fg_ref, o_ref):
    for c in range(meta_ref.shape[0]):
        ts = slice(c * SORT_TILE, (c + 1) * SORT_TILE)
        y = _unsort(ys_ref[c * SLOTS:(c + 1) * SLOTS, :], meta_ref[c])
        o_ref[ts, :] = _rms(x_ref[ts, :] + mod_ref[5:6, :] * y, fg_ref[...])


def _final_combine(ys_rows, meta, x2, mod_l, final_g, seq, tm=1024):
    t, d = x2.shape
    per_b = seq // tm
    sub = tm // SORT_TILE
    return pl.pallas_call(
        _combine_kernel,
        out_shape=jax.ShapeDtypeStruct((t, d), F32),
        grid=(t // tm,),
        in_specs=[
            pl.BlockSpec((sub * SLOTS, d), lambda i: (i, 0)),
            pl.BlockSpec((sub, META_ROWS, SORT_TILE), lambda i: (i, 0, 0)),
            pl.BlockSpec((tm, d), lambda i: (i, 0)),
            pl.BlockSpec((None, 6, d), lambda i: (i // per_b, 0, 0)),
            pl.BlockSpec((1, d), lambda i: (0, 0)),
        ],
        out_specs=pl.BlockSpec((tm, d), lambda i: (i, 0)),
        compiler_params=_params(("parallel",)),
        name="moe_combine_final_norm",
    )(ys_rows, meta, x2, mod_l, final_g.reshape(1, d))


def _moe_experts(xs, counts, w_gu, w_down, layer):
    d = xs.shape[1]
    n_tiles = counts.shape[0]
    n_pages = n_tiles * PAGES_PER_TILE
    n_steps = n_pages // STEP_PAGES + N_EXPERTS + 1 + 2
    plan = _expert_plan(counts[:, :, 0].astype(jnp.int32), n_steps)
    ys = _experts(xs.reshape(n_pages, PAGE, d), plan, w_gu, w_down, layer, n_steps)
    return ys.reshape(-1, d)


def kernel(x, c, norm1_g, norm2_g, ada_w, ada_b, ev_w_in, ev_w_out, sgu_ln_g, sgu_ln_b, sgu_w, sgu_b, lam_q1, lam_k1, lam_q2, lam_k2, diff_subln_g, od_w_in, od_w_out, conv_w, conv_b, conv_ln_g, conv_ln_b, router_w, router_b, moe_w_gu, moe_w_down, final_g):
    bsz, seq, d = x.shape
    depth = ada_w.shape[0]
    half = d // 2
    diff_heads = half // (2 * HEAD_DIM)
    dil_heads = half // HEAD_DIM
    t = bsz * seq

    mod = _modulation(c, ada_w, ada_b)
    rw_hi = router_w.astype(BF16)
    rw_lo = (router_w - rw_hi.astype(F32)).astype(BF16)
    rw_split = jnp.pad(jnp.concatenate([rw_hi, rw_lo], axis=1),
                       ((0, 0), (0, LANES - 2 * N_EXPERTS)))

    x2 = x.reshape(t, d)
    moe = None
    for l in range(depth):
        i = l // 2
        w_in = (ev_w_in if l % 2 == 0 else od_w_in)[i].astype(BF16)
        x2, proj = _in_projection(x2, mod[l], norm1_g[l], w_in, seq, moe)
        if l % 2 == 0:
            lam_init = 0.8 - 0.6 * math.exp(-0.3 * l)
            part_a = _spatial_gating(proj, sgu_ln_g[i], sgu_ln_b[i], sgu_w[i], sgu_b[i], half)
            conv = None
            part_b = _diff_attention(proj.reshape(bsz, seq, -1), lam_q1[i], lam_k1[i], lam_q2[i],
                                     lam_k2[i], diff_subln_g[i], lam_init, diff_heads,
                                     2 * half).reshape(t, half)
            w_out = ev_w_out[i]
        else:
            proj3 = proj.reshape(bsz, seq, -1)
            part_a = _dilated_attention(proj3, dil_heads).reshape(t, half)
            part_b = None
            conv = (proj, (2 * len(DIL_PAIRS) + 1) * half, conv_w[i], conv_b[i],
                    conv_ln_g[i], conv_ln_b[i])
            w_out = od_w_out[i]
        x2, xs, meta, counts = _out_projection(part_a, part_b, x2, mod[l], norm2_g[l],
                                               w_out.astype(BF16), rw_split, router_b, seq, conv)
        moe = (_moe_experts(xs, counts, moe_w_gu, moe_w_down, l), meta, mod[l])
    out = _final_combine(moe[0], moe[1], x2, mod[depth - 1], final_g, seq)
    return out.reshape(bsz, seq, d)
```

```python
import functools
import math

import jax
import jax.numpy as jnp
from jax import lax
from jax.experimental import pallas as pl
from jax.experimental.pallas import tpu as pltpu

F32 = jnp.float32
BF16 = jnp.bfloat16

HEAD_DIM = 64
LANES = 128
CHUNK = 128
SGU_GROUPS = 4
DIL_PAIRS = ((128, 1), (512, 4), (2048, 16))
WIN_BLOCK = 128
CONV_K = 31
CONV_SUBLANES = 8
N_EXPERTS = 16
EXPERTS_PER_GROUP = 4
RMS_EPS = 1e-6
LN_EPS = 1e-5
NEG = -1e30
LOG2E = 1.4426950408889634
VMEM_LIMIT = 56 * 1024 * 1024


def _params(sem):
    return pltpu.CompilerParams(dimension_semantics=sem, vmem_limit_bytes=VMEM_LIMIT)


def _gelu(x):
    return x * (0.5 * (1.0 + jnp.tanh(0.7978845608028654 * (x + 0.044715 * (x * x * x)))))


def _silu(x):
    return x * jax.nn.sigmoid(x)


def _rms(x, g):
    ms = jnp.mean(x * x, axis=-1, keepdims=True)
    return x * lax.rsqrt(ms + RMS_EPS) * g


def _layer_norm(x, g, b):
    mu = jnp.mean(x, axis=-1, keepdims=True)
    xc = x - mu
    var = jnp.mean(xc * xc, axis=-1, keepdims=True)
    return xc * lax.rsqrt(var + LN_EPS) * g + b


def _mod_kernel(c_ref, w_ref, b_ref, o_ref):
    ca = _silu(c_ref[...]).astype(BF16)
    o_ref[...] = jnp.dot(ca, w_ref[...].astype(BF16), preferred_element_type=F32) + b_ref[...]


def _modulation(c, ada_w, ada_b):
    depth, d, six_d = ada_w.shape
    bsz = c.shape[0]
    n = six_d // d
    out = pl.pallas_call(
        _mod_kernel,
        out_shape=jax.ShapeDtypeStruct((depth, bsz, six_d), F32),
        grid=(depth, n),
        in_specs=[
            pl.BlockSpec((bsz, d), lambda l, j: (0, 0)),
            pl.BlockSpec((None, d, d), lambda l, j: (l, 0, j)),
            pl.BlockSpec((None, 1, d), lambda l, j: (l, 0, j)),
        ],
        out_specs=pl.BlockSpec((None, bsz, d), lambda l, j: (l, 0, j)),
        compiler_params=_params(("parallel", "parallel")),
        name="modulation",
    )(c, ada_w, ada_b.reshape(depth, 1, six_d))
    return out.reshape(depth, bsz, n, d)


SORT_TILE = 256
PAGE = 16
TOP_K = 2
SLOTS = TOP_K * SORT_TILE + N_EXPERTS * PAGE
PAGES_PER_TILE = SLOTS // PAGE
STEP_PAGES = 32
META_ROWS = 8


def _unsort(ys, meta):
    pos0 = meta[0:1, :].astype(jnp.int32)
    pos1 = meta[1:2, :].astype(jnp.int32)
    slot = lax.broadcasted_iota(jnp.int32, (SLOTS, meta.shape[1]), 0)
    w = (jnp.where(slot == pos0, meta[2:3, :], 0.0)
         + jnp.where(slot == pos1, meta[3:4, :], 0.0)).astype(BF16)
    return lax.dot_general(w, ys, (((0,), (0,)), ((), ())), preferred_element_type=F32)


def _project(x, mod_ref, g_ref, w_ref, o_ref, n_chunk):
    h = _rms(x, g_ref[...]) * (1.0 + mod_ref[1:2, :]) + mod_ref[0:1, :]
    hb = h.astype(BF16)
    for j in range(0, o_ref.shape[-1], n_chunk):
        o_ref[:, j:j + n_chunk] = jnp.dot(
            hb, w_ref[:, j:j + n_chunk], preferred_element_type=F32).astype(o_ref.dtype)


def _inproj_kernel(x_ref, mod_ref, g_ref, w_ref, o_ref, *, n_chunk):
    _project(x_ref[...], mod_ref, g_ref, w_ref, o_ref, n_chunk)


def _inproj_moe_kernel(ys_ref, meta_ref, x_ref, pmod_ref, mod_ref, g_ref, w_ref, xo_ref, o_ref,
                       *, n_chunk):
    for c in range(meta_ref.shape[0]):
        ts = slice(c * SORT_TILE, (c + 1) * SORT_TILE)
        y = _unsort(ys_ref[c * SLOTS:(c + 1) * SLOTS, :], meta_ref[c])
        xo_ref[ts, :] = x_ref[ts, :] + pmod_ref[5:6, :] * y
    _project(xo_ref[...], mod_ref, g_ref, w_ref, o_ref, n_chunk)


def _in_projection(x2, mod_l, g, w_bf, seq, moe=None, tm=512):
    t, d = x2.shape
    n = w_bf.shape[1]
    per_b = seq // tm
    sub = tm // SORT_TILE
    row_spec = pl.BlockSpec((tm, d), lambda i: (i, 0))
    mod_spec = pl.BlockSpec((None, 6, d), lambda i: (i // per_b, 0, 0))
    tail_specs = [mod_spec, pl.BlockSpec((1, d), lambda i: (0, 0)),
                  pl.BlockSpec((d, n), lambda i: (0, 0))]
    proj_shape = jax.ShapeDtypeStruct((t, n), BF16)
    proj_spec = pl.BlockSpec((tm, n), lambda i: (i, 0))
    if moe is None:
        return x2, pl.pallas_call(
            functools.partial(_inproj_kernel, n_chunk=512),
            out_shape=proj_shape,
            grid=(t // tm,),
            in_specs=[row_spec] + tail_specs,
            out_specs=proj_spec,
            compiler_params=_params(("parallel",)),
            name="in_projection",
        )(x2, mod_l, g.reshape(1, d), w_bf)
    ys_rows, meta, prev_mod = moe
    return pl.pallas_call(
        functools.partial(_inproj_moe_kernel, n_chunk=512),
        out_shape=(jax.ShapeDtypeStruct((t, d), F32), proj_shape),
        grid=(t // tm,),
        in_specs=[pl.BlockSpec((sub * SLOTS, d), lambda i: (i, 0)),
                  pl.BlockSpec((sub, META_ROWS, SORT_TILE), lambda i: (i, 0, 0)),
                  row_spec, mod_spec] + tail_specs,
        out_specs=(row_spec, proj_spec),
        compiler_params=_params(("parallel",)),
        name="moe_combine_in_projection",
    )(ys_rows, meta, x2, prev_mod, mod_l, g.reshape(1, d), w_bf)


def _sgu_rows(u, v, lng_ref, lnb_ref, w_ref, bias_ref):
    u = _gelu(u.astype(F32))
    v = _gelu(v.astype(F32))
    vb = _layer_norm(v, lng_ref[...], lnb_ref[...]).astype(BF16)
    row = lax.broadcasted_iota(jnp.int32, (CHUNK, CHUNK), 0)
    col = lax.broadcasted_iota(jnp.int32, (CHUNK, CHUNK), 1)
    causal = col <= row
    gd = u.shape[1] // SGU_GROUPS
    n_chunks = u.shape[0] // CHUNK
    cols = []
    for g in range(SGU_GROUPS):
        w = jnp.where(causal, w_ref[g], 0.0).astype(BF16)
        cs = slice(g * gd, (g + 1) * gd)
        rhs = jnp.concatenate([vb[c * CHUNK:(c + 1) * CHUNK, cs] for c in range(n_chunks)], axis=1)
        mixed = jnp.dot(w, rhs, preferred_element_type=F32)
        cols.append(jnp.concatenate(
            [u[c * CHUNK:(c + 1) * CHUNK, cs] * (mixed[:, c * gd:(c + 1) * gd] + bias_ref[:, cs])
             for c in range(n_chunks)], axis=0))
    return jnp.concatenate(cols, axis=1).astype(BF16)


ONES_ROWS = 16


def _split3(x):
    hi = x.astype(BF16).astype(F32)
    r1 = x - hi
    mid = r1.astype(BF16).astype(F32)
    lo = (r1 - mid).astype(BF16).astype(F32)
    return hi, mid, lo


def _diff_kernel(slopes_ref, lq1_ref, lk1_ref, lq2_ref, lk2_ref, q_ref, k_ref, v_ref, g_ref,
                 o_ref, kx, qx, vt, acc, *, tq, lam_init):
    h = pl.program_id(1)
    seq, vd = v_ref.shape
    n_q = seq // tq
    hd = HEAD_DIM
    lane = lax.broadcasted_iota(jnp.int32, (tq, LANES), 1)

    def place(pieces, base, fill):
        out = fill
        for n, piece in enumerate(pieces):
            out = jnp.where(lane == base + n, piece, out)
        return out

    slope = slopes_ref[h] * LOG2E
    one = jnp.ones((tq, LANES), F32)
    for c in range(n_q):
        rs = slice(c * tq, (c + 1) * tq)
        k = k_ref[rs, :].astype(F32)
        pos = (lax.broadcasted_iota(jnp.int32, (tq, LANES), 0) + c * tq).astype(F32) * slope
        pieces = _split3(pos)
        zero = jnp.zeros_like(k)
        kx[0, rs, :] = jnp.where(lane < hd, k, place(pieces, hd, zero)).astype(BF16)
        kx[1, rs, :] = jnp.where(lane >= hd, k, place(pieces, 0, zero)).astype(BF16)
        vt[0:vd, rs] = v_ref[rs, :].astype(F32).T.astype(BF16)
        vt[vd:, rs] = jnp.ones((ONES_ROWS, tq), BF16)
        q = q_ref[rs, :].astype(F32) * (hd ** -0.5 * LOG2E)
        qx[0, rs, :] = jnp.where(lane < hd, q, place((one, one, one), hd, zero)).astype(BF16)
        qx[1, rs, :] = jnp.where(lane >= hd, q, place((one, one, one), 0, zero)).astype(BF16)

    kc = tq
    chunks = {qi: [(k0, min(k0 + kc, (qi + 1) * tq)) for k0 in range(0, (qi + 1) * tq, kc)]
              for qi in range(n_q)}
    ms = {}
    for c in range(max(len(v) for v in chunks.values())):
        for qi in range(n_q):
            if c >= len(chunks[qi]):
                continue
            k0, k1 = chunks[qi][c]
            for m in range(2):
                s = lax.dot_general(kx[m, k0:k1, :], qx[m, qi * tq:(qi + 1) * tq, :],
                                    (((1,), (1,)), ((), ())),
                                    preferred_element_type=F32)
                if k1 > qi * tq + 1:
                    key_i = lax.broadcasted_iota(jnp.int32, s.shape, 0) + k0
                    qry_i = lax.broadcasted_iota(jnp.int32, s.shape, 1) + qi * tq
                    s = jnp.where(key_i <= qry_i, s, NEG)
                m_tile = jnp.max(s, axis=0, keepdims=True)
                if c == 0:
                    ms[qi, m] = m_tile
                    p = jnp.exp2(s - m_tile).astype(BF16)
                    acc[qi, m] = jnp.dot(vt[:, k0:k1], p, preferred_element_type=F32)
                else:
                    m_new = jnp.maximum(ms[qi, m], m_tile)
                    alpha = jnp.exp2(ms[qi, m] - m_new)
                    ms[qi, m] = m_new
                    p = jnp.exp2(s - m_new).astype(BF16)
                    acc[qi, m] = (alpha * acc[qi, m]
                                  + jnp.dot(vt[:, k0:k1], p, preferred_element_type=F32))

    lam = (jnp.exp(jnp.sum(lq1_ref[...] * lk1_ref[...], axis=-1, keepdims=True))
           - jnp.exp(jnp.sum(lq2_ref[...] * lk2_ref[...], axis=-1, keepdims=True)) + lam_init)
    for qi in range(n_q):
        o_t = (acc[qi, 0, 0:vd, :] / acc[qi, 0, vd:vd + 1, :]
               - lam * (acc[qi, 1, 0:vd, :] / acc[qi, 1, vd:vd + 1, :]))
        o_ref[qi * tq:(qi + 1) * tq, :] = (
            _rms(o_t.T, g_ref[...]) * (1.0 - lam_init)).astype(o_ref.dtype)


def _diff_attention(proj3, lq1, lk1, lq2, lk2, subln_g, lam_init, n_heads, col0, tq=1024):
    bsz, seq, _ = proj3.shape
    slopes = 2.0 ** (-8.0 * jnp.arange(1, n_heads + 1, dtype=F32) / n_heads)
    qb, kb, vb = col0 // LANES, col0 // LANES + n_heads, col0 // LANES + 2 * n_heads
    vec = lambda a: a.reshape(1, -1)
    small = lambda n: pl.BlockSpec((1, n), lambda b, h: (0, 0))
    return pl.pallas_call(
        functools.partial(_diff_kernel, tq=tq, lam_init=lam_init),
        out_shape=jax.ShapeDtypeStruct((bsz, seq, n_heads * LANES), BF16),
        grid=(bsz, n_heads),
        in_specs=[
            pl.BlockSpec(memory_space=pltpu.SMEM),
            small(HEAD_DIM), small(HEAD_DIM), small(HEAD_DIM), small(HEAD_DIM),
            pl.BlockSpec((None, seq, LANES), lambda b, h: (b, 0, qb + h)),
            pl.BlockSpec((None, seq, LANES), lambda b, h: (b, 0, kb + h)),
            pl.BlockSpec((None, seq, LANES), lambda b, h: (b, 0, vb + h)),
            small(LANES),
        ],
        out_specs=pl.BlockSpec((None, seq, LANES), lambda b, h: (b, 0, h)),
        scratch_shapes=[
            pltpu.VMEM((2, seq, LANES), BF16),
            pltpu.VMEM((2, seq, LANES), BF16),
            pltpu.VMEM((LANES + ONES_ROWS, seq), BF16),
            pltpu.VMEM((seq // tq, 2, LANES + ONES_ROWS, tq), F32),
        ],
        compiler_params=_params(("parallel", "parallel")),
        name="diff_attention",
    )(slopes, vec(lq1), vec(lk1), vec(lq2), vec(lk2), proj3, proj3, proj3, vec(subln_g))


def _dil_kernel(slopes_ref, q1_ref, q2_ref, q3_ref, k1_ref, k2_ref, k3_ref, v_ref, o_ref,
                qf, kf, vf, bias_s, acc_s, m_s, l_s, *, seq):
    hp = pl.program_id(1)
    wb = WIN_BLOCK
    hd = HEAD_DIM
    n_q_blocks = seq // wb

    qf[0] = q2_ref[...].astype(F32)
    qf[1] = q3_ref[...].astype(F32)
    kf[0] = k2_ref[...].astype(F32)
    kf[1] = k3_ref[...].astype(F32)
    vf[...] = v_ref[...].astype(F32)

    row = lax.broadcasted_iota(jnp.int32, (2 * wb, 2 * wb), 0)
    col = lax.broadcasted_iota(jnp.int32, (2 * wb, 2 * wb), 1)
    qi_ = jnp.where(row < wb, row, row - wb)
    dist = qi_ + wb - col
    slope = jnp.where(row < wb, slopes_ref[2 * hp], slopes_ref[2 * hp + 1])
    valid = jnp.logical_and(dist >= 0, dist <= wb)
    table = jnp.where(valid, -(slope * LOG2E) * dist.astype(F32), NEG)
    bias_s[1] = table
    bias_s[0] = jnp.where(col < wb, NEG, table)

    first = lax.broadcasted_iota(jnp.int32, (wb, LANES), 1) < hd
    q_scale = hd ** -0.5 * LOG2E

    for g, (window, dil) in enumerate(DIL_PAIRS):
        assert window // dil == wb
        nb = n_q_blocks // dil
        refs = {"q": (q1_ref, qf), "k": (k1_ref, kf), "v": (v_ref, vf)}

        def rows(kind, start, g=g, dil=dil, refs=refs):
            direct, copies = refs[kind]
            if dil == 1:
                return direct[pl.ds(pl.multiple_of(start, wb), wb), :]
            src = copies if kind == "v" else copies.at[g - 1]
            return src[pl.ds(start, wb, stride=dil), :]

        def body(idx, _, g=g, dil=dil, nb=nb, rows=rows):
            r = idx // nb
            n = idx % nb
            cur = n * (wb * dil) + r
            q = rows("q", cur).astype(F32) * q_scale
            zero = jnp.zeros_like(q)
            qx = jnp.concatenate([jnp.where(first, q, zero), jnp.where(first, zero, q)],
                                 axis=0).astype(BF16)
            if nb > 1:
                prev = jnp.maximum(n - 1, 0) * (wb * dil) + r
                keys = jnp.concatenate([rows("k", prev), rows("k", cur)], axis=0).astype(BF16)
                vals = jnp.concatenate([rows("v", prev), rows("v", cur)], axis=0).astype(BF16)
                bias = bias_s[jnp.minimum(n, 1)]
            else:
                keys, vals = rows("k", cur).astype(BF16), rows("v", cur).astype(BF16)
                bias = bias_s[1, :, wb:]
            vx = jnp.concatenate([vals, jnp.ones_like(vals)], axis=1)
            s = lax.dot_general(qx, keys, (((1,), (1,)), ((), ())),
                                preferred_element_type=F32) + bias
            m = jnp.max(s, axis=-1, keepdims=True)
            p = jnp.exp2(s - m).astype(BF16)
            o = jnp.dot(p, vx, preferred_element_type=F32)
            acc = jnp.where(first, o[0:wb, 0:LANES], o[wb:, 0:LANES])
            den = jnp.where(first, o[0:wb, LANES:], o[wb:, LANES:])
            mx = jnp.where(first, jnp.broadcast_to(m[0:wb], (wb, LANES)),
                           jnp.broadcast_to(m[wb:], (wb, LANES)))
            if dil == 1:
                dst = pl.ds(pl.multiple_of(cur, wb), wb)
            else:
                dst = pl.ds(cur, wb, stride=dil)
            acc_s[g, dst, :] = acc
            m_s[g, dst, :] = mx
            l_s[g, dst, :] = den
            return 0

        lax.fori_loop(0, n_q_blocks, body, 0, unroll=16)

    mr = 256
    for c in range(seq // mr):
        rs = slice(c * mr, (c + 1) * mr)
        m1, m2, m3 = m_s[0, rs, :], m_s[1, rs, :], m_s[2, rs, :]
        m = jnp.maximum(jnp.maximum(m1, m2), m3)
        w1, w2, w3 = jnp.exp2(m1 - m), jnp.exp2(m2 - m), jnp.exp2(m3 - m)
        num = w1 * acc_s[0, rs, :] + w2 * acc_s[1, rs, :] + w3 * acc_s[2, rs, :]
        den = w1 * l_s[0, rs, :] + w2 * l_s[1, rs, :] + w3 * l_s[2, rs, :]
        o_ref[rs, :] = (num / den).astype(o_ref.dtype)


def _dilated_attention(proj3, n_heads):
    bsz, seq, _ = proj3.shape
    n_pairs = n_heads * HEAD_DIM // LANES
    slopes = 2.0 ** (-8.0 * jnp.arange(1, n_heads + 1, dtype=F32) / n_heads)
    groups = len(DIL_PAIRS)

    def spec(blk0):
        return pl.BlockSpec((None, seq, LANES), lambda b, p: (b, 0, blk0 + p))

    q_specs = [spec(g * n_pairs) for g in range(groups)]
    k_specs = [spec((groups + g) * n_pairs) for g in range(groups)]
    v_spec = spec(2 * groups * n_pairs)
    return pl.pallas_call(
        functools.partial(_dil_kernel, seq=seq),
        out_shape=jax.ShapeDtypeStruct((bsz, seq, n_pairs * LANES), BF16),
        grid=(bsz, n_pairs),
        in_specs=[pl.BlockSpec(memory_space=pltpu.SMEM)] + q_specs + k_specs + [v_spec],
        out_specs=pl.BlockSpec((None, seq, LANES), lambda b, p: (b, 0, p)),
        scratch_shapes=[
            pltpu.VMEM((groups - 1, seq, LANES), F32),
            pltpu.VMEM((groups - 1, seq, LANES), F32),
            pltpu.VMEM((seq, LANES), F32),
            pltpu.VMEM((2, 2 * WIN_BLOCK, 2 * WIN_BLOCK), F32),
            pltpu.VMEM((groups, seq, LANES), F32),
            pltpu.VMEM((groups, seq, LANES), F32),
            pltpu.VMEM((groups, seq, LANES), F32),
        ],
        compiler_params=_params(("parallel", "parallel")),
        name="dilated_attention",
    )(slopes, *([proj3] * 7))


CONV_HALO = 32
CONV_ROWS = 64


def _conv_rows(buf, win, w_ref, cb_ref, lng_ref, lnb_ref, row0, rows):
    off = CONV_HALO - (CONV_K - 1)
    sub = CONV_SUBLANES
    rc = CONV_ROWS
    outs = []
    for c in range(rows // rc):
        base = row0 + c * rc + off
        acc = jnp.zeros((rc, buf.shape[1]), F32)
        for r in range(sub):
            span = rc + (CONV_K - 1 - r) // sub * sub
            win[c % 2, r, 0:span, :] = buf[base + r:base + r + span, :]
            for k in range(r, CONV_K, sub):
                acc = acc + w_ref[k:k + 1, :] * win[c % 2, r, k - r:k - r + rc, :]
        y = _layer_norm(acc + cb_ref[...], lng_ref[...], lnb_ref[...])
        outs.append(_silu(y).astype(BF16))
    return jnp.concatenate(outs, axis=0)


def _route(logits_t, rb):
    mx = jnp.max(logits_t, axis=0, keepdims=True)
    ex = jnp.exp(logits_t - mx)
    probs = ex / jnp.sum(ex, axis=0, keepdims=True)
    sel = probs + rb
    srow = [sel[i:i + 1, :] for i in range(N_EXPERTS)]
    prow = [probs[i:i + 1, :] for i in range(N_EXPERTS)]
    epg = EXPERTS_PER_GROUP
    n_grp = N_EXPERTS // epg

    scores = []
    for g in range(n_grp):
        a, b, c, d = srow[g * epg:(g + 1) * epg]
        hi1, lo1 = jnp.maximum(a, b), jnp.minimum(a, b)
        hi2, lo2 = jnp.maximum(c, d), jnp.minimum(c, d)
        scores.append(jnp.maximum(hi1, hi2)
                      + jnp.maximum(jnp.minimum(hi1, hi2), jnp.maximum(lo1, lo2)))
    best = scores[0]
    grp = jnp.zeros(best.shape, jnp.int32)
    for g in range(1, n_grp):
        better = scores[g] > best
        grp = jnp.where(better, g, grp)
        best = jnp.where(better, scores[g], best)

    def pick(rows_, j):
        out = rows_[j]
        for g in range(1, n_grp):
            out = jnp.where(grp == g, rows_[g * epg + j], out)
        return out

    ing = [pick(srow, j) for j in range(epg)]
    ping = [pick(prow, j) for j in range(epg)]
    b0, i0, p0 = ing[0], jnp.zeros(best.shape, jnp.int32), ping[0]
    for j in range(1, epg):
        better = ing[j] > b0
        i0 = jnp.where(better, j, i0)
        p0 = jnp.where(better, ping[j], p0)
        b0 = jnp.where(better, ing[j], b0)
    b1 = jnp.full(best.shape, -jnp.inf, F32)
    i1 = jnp.zeros(best.shape, jnp.int32)
    p1 = jnp.zeros(best.shape, F32)
    for j in range(epg):
        better = jnp.logical_and(i0 != j, ing[j] > b1)
        i1 = jnp.where(better, j, i1)
        p1 = jnp.where(better, ping[j], p1)
        b1 = jnp.where(better, ing[j], b1)
    den = p0 + p1
    return grp * epg + i0, grp * epg + i1, p0 / den, p1 / den


def _sort_tile(e0, e1, hb):
    st = hb.shape[0]
    eid = lax.broadcasted_iota(jnp.int32, (N_EXPERTS, st), 0)
    sel0 = jnp.where(eid == e0, 1.0, 0.0)
    sel1 = jnp.where(eid == e1, 1.0, 0.0)
    sel = sel0 + sel1
    before = (lax.broadcasted_iota(jnp.int32, (st, st), 0)
              < lax.broadcasted_iota(jnp.int32, (st, st), 1))
    rank = jnp.dot(sel.astype(BF16), jnp.where(before, 1.0, 0.0).astype(BF16),
                   preferred_element_type=F32)
    cnt = jnp.sum(sel, axis=1, keepdims=True)
    padded = jnp.ceil(cnt * (1.0 / PAGE)) * PAGE
    ecol = lax.broadcasted_iota(jnp.int32, (N_EXPERTS, 1), 0)
    seg = jnp.zeros((N_EXPERTS, 1), F32)
    run = jnp.zeros((1, 1), F32)
    for e in range(N_EXPERTS):
        seg = jnp.where(ecol == e, run, seg)
        run = run + padded[e:e + 1, :]
    base = seg + rank
    pos0 = jnp.sum(sel0 * base, axis=0, keepdims=True)
    pos1 = jnp.sum(sel1 * base, axis=0, keepdims=True)
    slot = lax.broadcasted_iota(jnp.int32, (SLOTS, st), 0)
    perm = (jnp.where(slot == pos0.astype(jnp.int32), 1.0, 0.0)
            + jnp.where(slot == pos1.astype(jnp.int32), 1.0, 0.0)).astype(BF16)
    rows = jnp.dot(perm, hb, preferred_element_type=F32).astype(BF16)
    return pos0, pos1, cnt, rows


def _outproj_tiles(a_tile, b_tile, x_ref, mod_ref, g_ref, w_ref, rw_ref, rb_ref,
                   xo_ref, xs_ref, meta_ref, cnt_ref):
    ka = w_ref.shape[0] // 2
    for c in range(x_ref.shape[0] // SORT_TILE):
        ts = slice(c * SORT_TILE, (c + 1) * SORT_TILE)
        y = (jnp.dot(a_tile(c), w_ref[0:ka, :], preferred_element_type=F32)
             + jnp.dot(b_tile(c), w_ref[ka:, :], preferred_element_type=F32))
        xn = x_ref[ts, :] + mod_ref[2:3, :] * y
        xo_ref[ts, :] = xn
        h = _rms(xn, g_ref[...]) * (1.0 + mod_ref[4:5, :]) + mod_ref[3:4, :]
        hb = h.astype(BF16)
        hl = (h - hb.astype(F32)).astype(BF16)
        parts = (jnp.dot(hb, rw_ref[...], preferred_element_type=F32)
                 + jnp.dot(hl, rw_ref[...], preferred_element_type=F32)).T
        lt = parts[0:N_EXPERTS, :] + parts[N_EXPERTS:2 * N_EXPERTS, :]
        e0, e1, g0, g1 = _route(lt, rb_ref[...])
        pos0, pos1, cnt, rows = _sort_tile(e0, e1, hb)
        xs_ref[c * SLOTS:(c + 1) * SLOTS, :] = rows
        meta_ref[c] = jnp.concatenate(
            [pos0, pos1, g0, g1, jnp.zeros((META_ROWS - 4, SORT_TILE), F32)], axis=0)
        cnt_ref[c] = jnp.broadcast_to(cnt, (N_EXPERTS, LANES))


def _tile_rows(ref):
    return lambda c: ref[c * SORT_TILE:(c + 1) * SORT_TILE, :]


def _outproj_sgu_kernel(u_ref, v_ref, lng_ref, lnb_ref, ws_ref, bias_ref, b_ref, *rest):
    u, v = _tile_rows(u_ref), _tile_rows(v_ref)
    _outproj_tiles(lambda c: _sgu_rows(u(c), v(c), lng_ref, lnb_ref, ws_ref, bias_ref),
                   _tile_rows(b_ref), *rest)


def _outproj_conv_kernel(a_ref, ca_ref, cg_ref, cw_ref, cb_ref, clg_ref, clb_ref, *rest,
                         per_b):
    *rest, buf, win = rest
    tm = ca_ref.shape[0]

    @pl.when(pl.program_id(0) % per_b == 0)
    def _():
        buf[0:CONV_HALO, :] = jnp.zeros((CONV_HALO, buf.shape[1]), F32)

    buf[CONV_HALO:, :] = ca_ref[...].astype(F32) * jax.nn.sigmoid(cg_ref[...].astype(F32))
    _outproj_tiles(
        _tile_rows(a_ref),
        lambda c: _conv_rows(buf, win, cw_ref, cb_ref, clg_ref, clb_ref, c * SORT_TILE, SORT_TILE),
        *rest)
    buf[0:CONV_HALO, :] = buf[tm:tm + CONV_HALO, :]


def _out_projection(a, b, x2, mod_l, g, w_bf, rw_split, rb, seq, sgu=None, conv=None, tm=1024):
    t, d = x2.shape
    ka = kb = w_bf.shape[0] // 2
    per_b = seq // tm
    sub = tm // SORT_TILE
    n_tiles = t // SORT_TILE
    vec = lambda v: v.reshape(1, -1)
    if conv is None:
        proj, ln_g, ln_b, w_s, b_s = sgu
        gd = ka // SGU_GROUPS
        small = pl.BlockSpec((1, ka), lambda i: (0, 0))
        kernel_fn, sem, scratch = _outproj_sgu_kernel, "parallel", []
        a_args = [proj, proj, vec(ln_g), vec(ln_b), w_s, jnp.repeat(b_s.T, gd, axis=1)]
        a_specs = [pl.BlockSpec((tm, ka), lambda i: (i, 0)),
                   pl.BlockSpec((tm, ka), lambda i: (i, 1)), small, small,
                   pl.BlockSpec((SGU_GROUPS, CHUNK, CHUNK), lambda i: (0, 0, 0)),
                   pl.BlockSpec((CHUNK, ka), lambda i: (0, 0))]
        b_args = [b]
        b_specs = [pl.BlockSpec((tm, kb), lambda i: (i, 0))]
    else:
        a_args = [a]
        a_specs = [pl.BlockSpec((tm, ka), lambda i: (i, 0))]
        proj, col0, cw, cb, clg, clb = conv
        small = pl.BlockSpec((1, kb), lambda i: (0, 0))
        kernel_fn = functools.partial(_outproj_conv_kernel, per_b=per_b)
        sem = "arbitrary"
        scratch = [pltpu.VMEM((tm + CONV_HALO, kb), F32),
                   pltpu.VMEM((2, CONV_SUBLANES, CONV_ROWS + CONV_HALO, kb), F32)]
        b_args = [proj, proj, cw, vec(cb), vec(clg), vec(clb)]
        b_specs = [pl.BlockSpec((tm, kb), lambda i: (i, col0 // kb)),
                   pl.BlockSpec((tm, kb), lambda i: (i, col0 // kb + 1)),
                   pl.BlockSpec((CONV_K, kb), lambda i: (0, 0)), small, small, small]
    return pl.pallas_call(
        kernel_fn,
        out_shape=(jax.ShapeDtypeStruct((t, d), F32),
                   jax.ShapeDtypeStruct((n_tiles * SLOTS, d), BF16),
                   jax.ShapeDtypeStruct((n_tiles, META_ROWS, SORT_TILE), F32),
                   jax.ShapeDtypeStruct((n_tiles, N_EXPERTS, LANES), F32)),
        grid=(t // tm,),
        in_specs=a_specs + b_specs + [
            pl.BlockSpec((tm, d), lambda i: (i, 0)),
            pl.BlockSpec((None, 6, d), lambda i: (i // per_b, 0, 0)),
            pl.BlockSpec((1, d), lambda i: (0, 0)),
            pl.BlockSpec((ka + kb, d), lambda i: (0, 0)),
            pl.BlockSpec((d, LANES), lambda i: (0, 0)),
            pl.BlockSpec((N_EXPERTS, 1), lambda i: (0, 0)),
        ],
        out_specs=(pl.BlockSpec((tm, d), lambda i: (i, 0)),
                   pl.BlockSpec((sub * SLOTS, d), lambda i: (i, 0)),
                   pl.BlockSpec((sub, META_ROWS, SORT_TILE), lambda i: (i, 0, 0)),
                   pl.BlockSpec((sub, N_EXPERTS, LANES), lambda i: (i, 0, 0))),
        scratch_shapes=scratch,
        compiler_params=_params((sem,)),
        name="sgu_out_projection" if conv is None else "conv_out_projection",
    )(*a_args, *b_args, x2, mod_l, g.reshape(1, d), w_bf, rw_split, rb.reshape(N_EXPERTS, 1))


def _expert_plan(counts, n_steps_max):
    n_tiles = counts.shape[0]
    n_groups = N_EXPERTS + 1
    i32 = jnp.int32
    npg = (counts + PAGE - 1) // PAGE
    npg = jnp.concatenate([npg, PAGES_PER_TILE - jnp.sum(npg, axis=1, keepdims=True)], axis=1)
    seg = jnp.cumsum(npg, axis=1) - npg
    page_base = jnp.arange(n_tiles, dtype=i32)[:, None] * PAGES_PER_TILE + seg
    cum_t = jnp.cumsum(npg, axis=0)
    tot = cum_t[-1]
    steps_e = (tot + STEP_PAGES - 1) // STEP_PAGES
    step_end = jnp.cumsum(steps_e)
    n_live = step_end[-1]
    g = jnp.arange(n_steps_max, dtype=i32)
    ex = jnp.minimum(jnp.sum((step_end[:, None] <= g[None, :]).astype(i32), axis=0),
                     n_groups - 1)
    sel = jnp.arange(n_groups, dtype=i32)[:, None] == ex[None, :]

    def of_step(per_group):
        return jnp.sum(jnp.where(sel, per_group[:, None], 0), axis=0)

    def rows_of_step(per_tile_group):
        return jnp.sum(jnp.where(sel[None], per_tile_group[:, :, None], 0), axis=1)

    q = ((g - of_step(step_end - steps_e))[:, None] * STEP_PAGES
         + jnp.arange(STEP_PAGES, dtype=i32)[None, :])
    valid = jnp.logical_and(q < of_step(tot)[:, None], (g < n_live)[:, None])
    cum_e = rows_of_step(cum_t)
    ti = jnp.minimum(jnp.sum((cum_e[:, :, None] <= q[None]).astype(i32), axis=0), n_tiles - 1)
    shift = rows_of_step(page_base - (cum_t - npg))
    hit = jnp.arange(n_tiles, dtype=i32)[:, None, None] == ti[None]
    pid = q + jnp.sum(jnp.where(hit, shift[:, :, None], 0), axis=0)
    spare = (n_tiles * PAGES_PER_TILE + (g % 2)[:, None] * STEP_PAGES
             + jnp.arange(STEP_PAGES, dtype=i32)[None, :])
    first = jnp.where(g < n_live, pid[:, 0], pid[0, 0])[:, None]
    in_pages = jnp.where(valid, pid, first).astype(i32).reshape(-1)
    out_pages = jnp.where(valid, pid, spare).astype(i32).reshape(-1)
    return in_pages, out_pages, ex.astype(i32), n_live.astype(i32).reshape(1)


def _expert_kernel(inp_ref, outp_ref, ex_ref, live_ref, xs_ref, wgu_ref, wd_ref, ys_ref,
                   wgu_bf, wd_bf, xbuf, ybuf, zbuf, in_sems, out_sems):
    g = pl.program_id(0)
    live = live_ref[0]
    slot = g % 2
    ff = wd_ref.shape[0]
    n_spare = 2 * STEP_PAGES
    half_rows = STEP_PAGES * PAGE // 2

    def is_expert_step(step):
        group = ex_ref[jnp.minimum(step, pl.num_programs(0) - 1)]
        return jnp.logical_and(step < live, group < N_EXPERTS)

    def in_copies(step, buf):
        return [pltpu.make_async_copy(xs_ref.at[inp_ref[step * STEP_PAGES + k]],
                                      xbuf.at[buf, pl.ds(k * PAGE, PAGE), :], in_sems.at[buf])
                for k in range(STEP_PAGES)]

    def out_copies(step, pages, zeros=False):
        return [pltpu.make_async_copy(
            zbuf.at[k] if zeros else ybuf.at[slot, pl.ds(k * PAGE, PAGE), :],
            ys_ref.at[outp_ref[step * STEP_PAGES + k]], out_sems.at[slot])
                for k in pages]

    @pl.when(g == 0)
    def _():
        zbuf[...] = jnp.zeros_like(zbuf)
        for j in range(n_spare // STEP_PAGES):
            first = ys_ref.shape[0] - n_spare + j * STEP_PAGES
            cp = pltpu.make_async_copy(zbuf, ys_ref.at[pl.ds(first, STEP_PAGES)], out_sems.at[1])
            cp.start()
            cp.wait()

        @pl.when(is_expert_step(0))
        def _():
            for cp in in_copies(0, 0):
                cp.start()

    @pl.when(jnp.logical_and(g >= 2, g - 2 < live))
    def _():
        for cp in out_copies(g - 2, range(STEP_PAGES)):
            cp.wait()

    group = ex_ref[g]
    is_expert = is_expert_step(g)
    new_expert = jnp.logical_or(g == 0, group != ex_ref[jnp.maximum(g - 1, 0)])

    @pl.when(jnp.logical_and(new_expert, is_expert))
    def _():
        wgu_bf[...] = wgu_ref[...].astype(BF16)
        wd_bf[...] = wd_ref[...].astype(BF16)

    @pl.when(is_expert)
    def _():
        for cp in in_copies(g, slot):
            cp.wait()

        @pl.when(is_expert_step(g + 1))
        def _():
            for cp in in_copies(g + 1, 1 - slot):
                cp.start()

        gu = jnp.dot(xbuf[slot], wgu_bf[...], preferred_element_type=F32)
        act = (_silu(gu[:, :ff]) * gu[:, ff:]).astype(BF16)
        for hlf in range(2):
            rows = slice(hlf * half_rows, (hlf + 1) * half_rows)
            ybuf[slot, rows, :] = jnp.dot(act[rows, :], wd_bf[...],
                                          preferred_element_type=F32).astype(BF16)
            for cp in out_copies(g, range(hlf * STEP_PAGES // 2, (hlf + 1) * STEP_PAGES // 2)):
                cp.start()

    @pl.when(jnp.logical_and(g < live, group == N_EXPERTS))
    def _():
        for cp in out_copies(g, range(STEP_PAGES), zeros=True):
            cp.start()


def _experts(xs_pages, plan, w_gu, w_down, layer, n_steps):
    n_pages, _, d = xs_pages.shape
    two_ff = w_gu.shape[-1]
    ff = two_ff // 2
    in_pages, out_pages, ex, live = plan
    grid_spec = pltpu.PrefetchScalarGridSpec(
        num_scalar_prefetch=4,
        grid=(n_steps,),
        in_specs=[
            pl.BlockSpec(memory_space=pl.ANY),
            pl.BlockSpec((None, None, d, two_ff),
                         lambda g, ip, op, e, nl: (layer, jnp.minimum(e[g], N_EXPERTS - 1), 0, 0)),
            pl.BlockSpec((None, None, ff, d),
                         lambda g, ip, op, e, nl: (layer, jnp.minimum(e[g], N_EXPERTS - 1), 0, 0)),
        ],
        out_specs=pl.BlockSpec(memory_space=pl.ANY),
        scratch_shapes=[pltpu.VMEM((d, two_ff), BF16), pltpu.VMEM((ff, d), BF16),
                        pltpu.VMEM((2, STEP_PAGES * PAGE, d), BF16),
                        pltpu.VMEM((2, STEP_PAGES * PAGE, d), BF16),
                        pltpu.VMEM((STEP_PAGES, PAGE, d), BF16),
                        pltpu.SemaphoreType.DMA((2,)),
                        pltpu.SemaphoreType.DMA((2,))],
    )
    return pl.pallas_call(
        _expert_kernel,
        out_shape=jax.ShapeDtypeStruct((n_pages + 2 * STEP_PAGES, PAGE, d), BF16),
        grid_spec=grid_spec,
        compiler_params=_params(("arbitrary",)),
        name="experts",
    )(in_pages, out_pages, ex, live, xs_pages, w_gu, w_down)


def _combine_kernel(ys_ref, meta_ref, x_ref, mod_ref, fg_ref, o_ref):
    for c in range(meta_ref.shape[0]):
        ts = slice(c * SORT_TILE, (c + 1) * SORT_TILE)
        y = _unsort(ys_ref[c * SLOTS:(c + 1) * SLOTS, :], meta_ref[c])
        o_ref[ts, :] = _rms(x_ref[ts, :] + mod_ref[5:6, :] * y, fg_ref[...])


def _final_combine(ys_rows, meta, x2, mod_l, final_g, seq, tm=1024):
    t, d = x2.shape
    per_b = seq // tm
    sub = tm // SORT_TILE
    return pl.pallas_call(
        _combine_kernel,
        out_shape=jax.ShapeDtypeStruct((t, d), F32),
        grid=(t // tm,),
        in_specs=[
            pl.BlockSpec((sub * SLOTS, d), lambda i: (i, 0)),
            pl.BlockSpec((sub, META_ROWS, SORT_TILE), lambda i: (i, 0, 0)),
            pl.BlockSpec((tm, d), lambda i: (i, 0)),
            pl.BlockSpec((None, 6, d), lambda i: (i // per_b, 0, 0)),
            pl.BlockSpec((1, d), lambda i: (0, 0)),
        ],
        out_specs=pl.BlockSpec((tm, d), lambda i: (i, 0)),
        compiler_params=_params(("parallel",)),
        name="moe_combine_final_norm",
    )(ys_rows, meta, x2, mod_l, final_g.reshape(1, d))


def _moe_experts(xs, counts, w_gu, w_down, layer):
    d = xs.shape[1]
    n_tiles = counts.shape[0]
    n_pages = n_tiles * PAGES_PER_TILE
    n_steps = n_pages // STEP_PAGES + N_EXPERTS + 1 + 2
    plan = _expert_plan(counts[:, :, 0].astype(jnp.int32), n_steps)
    ys = _experts(xs.reshape(n_pages, PAGE, d), plan, w_gu, w_down, layer, n_steps)
    return ys.reshape(-1, d)


def kernel(x, c, norm1_g, norm2_g, ada_w, ada_b, ev_w_in, ev_w_out, sgu_ln_g, sgu_ln_b, sgu_w, sgu_b, lam_q1, lam_k1, lam_q2, lam_k2, diff_subln_g, od_w_in, od_w_out, conv_w, conv_b, conv_ln_g, conv_ln_b, router_w, router_b, moe_w_gu, moe_w_down, final_g):
    bsz, seq, d = x.shape
    depth = ada_w.shape[0]
    half = d // 2
    diff_heads = half // (2 * HEAD_DIM)
    dil_heads = half // HEAD_DIM
    t = bsz * seq

    mod = _modulation(c, ada_w, ada_b)
    rw_hi = router_w.astype(BF16)
    rw_lo = (router_w - rw_hi.astype(F32)).astype(BF16)
    rw_split = jnp.pad(jnp.concatenate([rw_hi, rw_lo], axis=1),
                       ((0, 0), (0, LANES - 2 * N_EXPERTS)))

    x2 = x.reshape(t, d)
    moe = None
    for l in range(depth):
        i = l // 2
        w_in = (ev_w_in if l % 2 == 0 else od_w_in)[i].astype(BF16)
        x2, proj = _in_projection(x2, mod[l], norm1_g[l], w_in, seq, moe)
        if l % 2 == 0:
            lam_init = 0.8 - 0.6 * math.exp(-0.3 * l)
            part_a, conv = None, None
            sgu = (proj, sgu_ln_g[i], sgu_ln_b[i], sgu_w[i], sgu_b[i])
            part_b = _diff_attention(proj.reshape(bsz, seq, -1), lam_q1[i], lam_k1[i], lam_q2[i],
                                     lam_k2[i], diff_subln_g[i], lam_init, diff_heads,
                                     2 * half).reshape(t, half)
            w_out = ev_w_out[i]
        else:
            proj3 = proj.reshape(bsz, seq, -1)
            part_a = _dilated_attention(proj3, dil_heads).reshape(t, half)
            part_b, sgu = None, None
            conv = (proj, (2 * len(DIL_PAIRS) + 1) * half, conv_w[i], conv_b[i],
                    conv_ln_g[i], conv_ln_b[i])
            w_out = od_w_out[i]
        x2, xs, meta, counts = _out_projection(part_a, part_b, x2, mod[l], norm2_g[l],
                                               w_out.astype(BF16), rw_split, router_b, seq,
                                               sgu, conv)
        moe = (_moe_experts(xs, counts, moe_w_gu, moe_w_down, l), meta, mod[l])
    out = _final_combine(moe[0], moe[1], x2, mod[depth - 1], final_g, seq)
    return out.reshape(bsz, seq, d)
```

```python
import functools
import math

import jax
import jax.numpy as jnp
from jax import lax
from jax.experimental import pallas as pl
from jax.experimental.pallas import tpu as pltpu

F32 = jnp.float32
BF16 = jnp.bfloat16

HEAD_DIM = 64
LANES = 128
CHUNK = 128
SGU_GROUPS = 4
DIL_PAIRS = ((128, 1), (512, 4), (2048, 16))
WIN_BLOCK = 128
CONV_K = 31
CONV_SUBLANES = 8
N_EXPERTS = 16
EXPERTS_PER_GROUP = 4
RMS_EPS = 1e-6
LN_EPS = 1e-5
NEG = -1e30
LOG2E = 1.4426950408889634
VMEM_LIMIT = 56 * 1024 * 1024


def _params(sem):
    return pltpu.CompilerParams(dimension_semantics=sem, vmem_limit_bytes=VMEM_LIMIT)


def _gelu(x):
    return x * (0.5 * (1.0 + jnp.tanh(0.7978845608028654 * (x + 0.044715 * (x * x * x)))))


def _silu(x):
    return x * jax.nn.sigmoid(x)


def _rms(x, g):
    ms = jnp.mean(x * x, axis=-1, keepdims=True)
    return x * lax.rsqrt(ms + RMS_EPS) * g


def _layer_norm(x, g, b):
    mu = jnp.mean(x, axis=-1, keepdims=True)
    xc = x - mu
    var = jnp.mean(xc * xc, axis=-1, keepdims=True)
    return xc * lax.rsqrt(var + LN_EPS) * g + b


def _mod_kernel(c_ref, w_ref, b_ref, o_ref):
    ca = _silu(c_ref[...]).astype(BF16)
    o_ref[...] = jnp.dot(ca, w_ref[...].astype(BF16), preferred_element_type=F32) + b_ref[...]


def _modulation(c, ada_w, ada_b):
    depth, d, six_d = ada_w.shape
    bsz = c.shape[0]
    n = six_d // d
    out = pl.pallas_call(
        _mod_kernel,
        out_shape=jax.ShapeDtypeStruct((depth, bsz, six_d), F32),
        grid=(depth, n),
        in_specs=[
            pl.BlockSpec((bsz, d), lambda l, j: (0, 0)),
            pl.BlockSpec((None, d, d), lambda l, j: (l, 0, j)),
            pl.BlockSpec((None, 1, d), lambda l, j: (l, 0, j)),
        ],
        out_specs=pl.BlockSpec((None, bsz, d), lambda l, j: (l, 0, j)),
        compiler_params=_params(("parallel", "parallel")),
        name="modulation",
    )(c, ada_w, ada_b.reshape(depth, 1, six_d))
    return out.reshape(depth, bsz, n, d)


SORT_TILE = 256
PAGE = 16
TOP_K = 2
SLOTS = TOP_K * SORT_TILE + N_EXPERTS * PAGE
PAGES_PER_TILE = SLOTS // PAGE
STEP_PAGES = 32
META_ROWS = 8


def _unsort(ys, meta):
    pos0 = meta[0:1, :].astype(jnp.int32)
    pos1 = meta[1:2, :].astype(jnp.int32)
    slot = lax.broadcasted_iota(jnp.int32, (SLOTS, meta.shape[1]), 0)
    w = (jnp.where(slot == pos0, meta[2:3, :], 0.0)
         + jnp.where(slot == pos1, meta[3:4, :], 0.0)).astype(BF16)
    return lax.dot_general(w, ys, (((0,), (0,)), ((), ())), preferred_element_type=F32)


def _project(x, mod_ref, g_ref, w_ref, o_ref, n_chunk):
    h = _rms(x, g_ref[...]) * (1.0 + mod_ref[1:2, :]) + mod_ref[0:1, :]
    hb = h.astype(BF16)
    for j in range(0, o_ref.shape[-1], n_chunk):
        o_ref[:, j:j + n_chunk] = jnp.dot(
            hb, w_ref[:, j:j + n_chunk], preferred_element_type=F32).astype(o_ref.dtype)


def _inproj_kernel(x_ref, mod_ref, g_ref, w_ref, o_ref, *, n_chunk):
    _project(x_ref[...], mod_ref, g_ref, w_ref, o_ref, n_chunk)


def _inproj_moe_kernel(ys_ref, meta_ref, x_ref, pmod_ref, mod_ref, g_ref, w_ref, xo_ref, o_ref,
                       *, n_chunk):
    for c in range(meta_ref.shape[0]):
        ts = slice(c * SORT_TILE, (c + 1) * SORT_TILE)
        y = _unsort(ys_ref[c * SLOTS:(c + 1) * SLOTS, :], meta_ref[c])
        xo_ref[ts, :] = x_ref[ts, :] + pmod_ref[5:6, :] * y
    _project(xo_ref[...], mod_ref, g_ref, w_ref, o_ref, n_chunk)


def _in_projection(x2, mod_l, g, w_bf, seq, moe=None, tm=512):
    t, d = x2.shape
    n = w_bf.shape[1]
    per_b = seq // tm
    sub = tm // SORT_TILE
    row_spec = pl.BlockSpec((tm, d), lambda i: (i, 0))
    mod_spec = pl.BlockSpec((None, 6, d), lambda i: (i // per_b, 0, 0))
    tail_specs = [mod_spec, pl.BlockSpec((1, d), lambda i: (0, 0)),
                  pl.BlockSpec((d, n), lambda i: (0, 0))]
    proj_shape = jax.ShapeDtypeStruct((t, n), BF16)
    proj_spec = pl.BlockSpec((tm, n), lambda i: (i, 0))
    if moe is None:
        return x2, pl.pallas_call(
            functools.partial(_inproj_kernel, n_chunk=512),
            out_shape=proj_shape,
            grid=(t // tm,),
            in_specs=[row_spec] + tail_specs,
            out_specs=proj_spec,
            compiler_params=_params(("parallel",)),
            name="in_projection",
        )(x2, mod_l, g.reshape(1, d), w_bf)
    ys_rows, meta, prev_mod = moe
    return pl.pallas_call(
        functools.partial(_inproj_moe_kernel, n_chunk=512),
        out_shape=(jax.ShapeDtypeStruct((t, d), F32), proj_shape),
        grid=(t // tm,),
        in_specs=[pl.BlockSpec((sub * SLOTS, d), lambda i: (i, 0)),
                  pl.BlockSpec((sub, META_ROWS, SORT_TILE), lambda i: (i, 0, 0)),
                  row_spec, mod_spec] + tail_specs,
        out_specs=(row_spec, proj_spec),
        compiler_params=_params(("parallel",)),
        name="moe_combine_in_projection",
    )(ys_rows, meta, x2, prev_mod, mod_l, g.reshape(1, d), w_bf)


def _sgu_rows(u, v, lng_ref, lnb_ref, w_ref, bias_ref):
    u = _gelu(u.astype(F32))
    v = _gelu(v.astype(F32))
    vb = _layer_norm(v, lng_ref[...], lnb_ref[...]).astype(BF16)
    row = lax.broadcasted_iota(jnp.int32, (CHUNK, CHUNK), 0)
    col = lax.broadcasted_iota(jnp.int32, (CHUNK, CHUNK), 1)
    causal = col <= row
    gd = u.shape[1] // SGU_GROUPS
    n_chunks = u.shape[0] // CHUNK
    cols = []
    for g in range(SGU_GROUPS):
        w = jnp.where(causal, w_ref[g], 0.0).astype(BF16)
        cs = slice(g * gd, (g + 1) * gd)
        rhs = jnp.concatenate([vb[c * CHUNK:(c + 1) * CHUNK, cs] for c in range(n_chunks)], axis=1)
        mixed = jnp.dot(w, rhs, preferred_element_type=F32)
        cols.append(jnp.concatenate(
            [u[c * CHUNK:(c + 1) * CHUNK, cs] * (mixed[:, c * gd:(c + 1) * gd] + bias_ref[:, cs])
             for c in range(n_chunks)], axis=0))
    return jnp.concatenate(cols, axis=1).astype(BF16)


ONES_ROWS = 16


def _split3(x):
    hi = x.astype(BF16).astype(F32)
    r1 = x - hi
    mid = r1.astype(BF16).astype(F32)
    lo = (r1 - mid).astype(BF16).astype(F32)
    return hi, mid, lo


def _diff_kernel(slopes_ref, lq1_ref, lk1_ref, lq2_ref, lk2_ref, q_ref, k_ref, v_ref, g_ref,
                 o_ref, kx, qx, vt, acc, *, tq, lam_init):
    h = pl.program_id(1)
    seq, vd = v_ref.shape
    n_q = seq // tq
    hd = HEAD_DIM
    lane = lax.broadcasted_iota(jnp.int32, (tq, LANES), 1)

    def place(pieces, base, fill):
        out = fill
        for n, piece in enumerate(pieces):
            out = jnp.where(lane == base + n, piece, out)
        return out

    slope = slopes_ref[h] * LOG2E
    one = jnp.ones((tq, LANES), F32)
    for c in range(n_q):
        rs = slice(c * tq, (c + 1) * tq)
        k = k_ref[rs, :].astype(F32)
        pos = (lax.broadcasted_iota(jnp.int32, (tq, LANES), 0) + c * tq).astype(F32) * slope
        pieces = _split3(pos)
        zero = jnp.zeros_like(k)
        kx[0, rs, :] = jnp.where(lane < hd, k, place(pieces, hd, zero)).astype(BF16)
        kx[1, rs, :] = jnp.where(lane >= hd, k, place(pieces, 0, zero)).astype(BF16)
        vt[0:vd, rs] = v_ref[rs, :].astype(F32).T.astype(BF16)
        vt[vd:, rs] = jnp.ones((ONES_ROWS, tq), BF16)
        q = q_ref[rs, :].astype(F32) * (hd ** -0.5 * LOG2E)
        qx[0, rs, :] = jnp.where(lane < hd, q, place((one, one, one), hd, zero)).astype(BF16)
        qx[1, rs, :] = jnp.where(lane >= hd, q, place((one, one, one), 0, zero)).astype(BF16)

    kc = tq
    chunks = {qi: [(k0, min(k0 + kc, (qi + 1) * tq)) for k0 in range(0, (qi + 1) * tq, kc)]
              for qi in range(n_q)}
    ms = {}
    for c in range(max(len(v) for v in chunks.values())):
        for qi in range(n_q):
            if c >= len(chunks[qi]):
                continue
            k0, k1 = chunks[qi][c]
            for m in range(2):
                s = lax.dot_general(kx[m, k0:k1, :], qx[m, qi * tq:(qi + 1) * tq, :],
                                    (((1,), (1,)), ((), ())),
                                    preferred_element_type=F32)
                if k1 > qi * tq + 1:
                    key_i = lax.broadcasted_iota(jnp.int32, s.shape, 0) + k0
                    qry_i = lax.broadcasted_iota(jnp.int32, s.shape, 1) + qi * tq
                    s = jnp.where(key_i <= qry_i, s, NEG)
                m_tile = jnp.max(s, axis=0, keepdims=True)
                if c == 0:
                    ms[qi, m] = m_tile
                    p = jnp.exp2(s - m_tile).astype(BF16)
                    acc[qi, m] = jnp.dot(vt[:, k0:k1], p, preferred_element_type=F32)
                else:
                    m_new = jnp.maximum(ms[qi, m], m_tile)
                    alpha = jnp.exp2(ms[qi, m] - m_new)
                    ms[qi, m] = m_new
                    p = jnp.exp2(s - m_new).astype(BF16)
                    acc[qi, m] = (alpha * acc[qi, m]
                                  + jnp.dot(vt[:, k0:k1], p, preferred_element_type=F32))

    lam = (jnp.exp(jnp.sum(lq1_ref[...] * lk1_ref[...], axis=-1, keepdims=True))
           - jnp.exp(jnp.sum(lq2_ref[...] * lk2_ref[...], axis=-1, keepdims=True)) + lam_init)
    for qi in range(n_q):
        o_t = (acc[qi, 0, 0:vd, :] / acc[qi, 0, vd:vd + 1, :]
               - lam * (acc[qi, 1, 0:vd, :] / acc[qi, 1, vd:vd + 1, :]))
        o_ref[qi * tq:(qi + 1) * tq, :] = (
            _rms(o_t.T, g_ref[...]) * (1.0 - lam_init)).astype(o_ref.dtype)


def _diff_attention(proj3, lq1, lk1, lq2, lk2, subln_g, lam_init, n_heads, col0, tq=1024):
    bsz, seq, _ = proj3.shape
    slopes = 2.0 ** (-8.0 * jnp.arange(1, n_heads + 1, dtype=F32) / n_heads)
    qb, kb, vb = col0 // LANES, col0 // LANES + n_heads, col0 // LANES + 2 * n_heads
    vec = lambda a: a.reshape(1, -1)
    small = lambda n: pl.BlockSpec((1, n), lambda b, h: (0, 0))
    return pl.pallas_call(
        functools.partial(_diff_kernel, tq=tq, lam_init=lam_init),
        out_shape=jax.ShapeDtypeStruct((bsz, seq, n_heads * LANES), BF16),
        grid=(bsz, n_heads),
        in_specs=[
            pl.BlockSpec(memory_space=pltpu.SMEM),
            small(HEAD_DIM), small(HEAD_DIM), small(HEAD_DIM), small(HEAD_DIM),
            pl.BlockSpec((None, seq, LANES), lambda b, h: (b, 0, qb + h)),
            pl.BlockSpec((None, seq, LANES), lambda b, h: (b, 0, kb + h)),
            pl.BlockSpec((None, seq, LANES), lambda b, h: (b, 0, vb + h)),
            small(LANES),
        ],
        out_specs=pl.BlockSpec((None, seq, LANES), lambda b, h: (b, 0, h)),
        scratch_shapes=[
            pltpu.VMEM((2, seq, LANES), BF16),
            pltpu.VMEM((2, seq, LANES), BF16),
            pltpu.VMEM((LANES + ONES_ROWS, seq), BF16),
            pltpu.VMEM((seq // tq, 2, LANES + ONES_ROWS, tq), F32),
        ],
        compiler_params=_params(("parallel", "parallel")),
        name="diff_attention",
    )(slopes, vec(lq1), vec(lk1), vec(lq2), vec(lk2), proj3, proj3, proj3, vec(subln_g))


def _dil_kernel(slopes_ref, q1_ref, q2_ref, q3_ref, k1_ref, k2_ref, k3_ref, v_ref, o_ref,
                qf, kf, vf, bias_s, acc_s, m_s, l_s, *, seq):
    hp = pl.program_id(1)
    wb = WIN_BLOCK
    hd = HEAD_DIM
    n_q_blocks = seq // wb

    qf[0] = q2_ref[...].astype(F32)
    qf[1] = q3_ref[...].astype(F32)
    kf[0] = k2_ref[...].astype(F32)
    kf[1] = k3_ref[...].astype(F32)
    vf[...] = v_ref[...].astype(F32)

    row = lax.broadcasted_iota(jnp.int32, (2 * wb, 2 * wb), 0)
    col = lax.broadcasted_iota(jnp.int32, (2 * wb, 2 * wb), 1)
    qi_ = jnp.where(row < wb, row, row - wb)
    dist = qi_ + wb - col
    slope = jnp.where(row < wb, slopes_ref[2 * hp], slopes_ref[2 * hp + 1])
    valid = jnp.logical_and(dist >= 0, dist <= wb)
    table = jnp.where(valid, -(slope * LOG2E) * dist.astype(F32), NEG)
    bias_s[1] = table
    bias_s[0] = jnp.where(col < wb, NEG, table)

    first = lax.broadcasted_iota(jnp.int32, (wb, LANES), 1) < hd
    q_scale = hd ** -0.5 * LOG2E

    for g, (window, dil) in enumerate(DIL_PAIRS):
        assert window // dil == wb
        nb = n_q_blocks // dil
        refs = {"q": (q1_ref, qf), "k": (k1_ref, kf), "v": (v_ref, vf)}

        def rows(kind, start, g=g, dil=dil, refs=refs):
            direct, copies = refs[kind]
            if dil == 1:
                return direct[pl.ds(pl.multiple_of(start, wb), wb), :]
            src = copies if kind == "v" else copies.at[g - 1]
            return src[pl.ds(start, wb, stride=dil), :]

        def body(idx, _, g=g, dil=dil, nb=nb, rows=rows):
            r = idx // nb
            n = idx % nb
            cur = n * (wb * dil) + r
            q = rows("q", cur).astype(F32) * q_scale
            zero = jnp.zeros_like(q)
            qx = jnp.concatenate([jnp.where(first, q, zero), jnp.where(first, zero, q)],
                                 axis=0).astype(BF16)
            if nb > 1:
                prev = jnp.maximum(n - 1, 0) * (wb * dil) + r
                keys = jnp.concatenate([rows("k", prev), rows("k", cur)], axis=0).astype(BF16)
                vals = jnp.concatenate([rows("v", prev), rows("v", cur)], axis=0).astype(BF16)
                bias = bias_s[jnp.minimum(n, 1)]
            else:
                keys, vals = rows("k", cur).astype(BF16), rows("v", cur).astype(BF16)
                bias = bias_s[1, :, wb:]
            vx = jnp.concatenate([vals, jnp.ones_like(vals)], axis=1)
            s = lax.dot_general(qx, keys, (((1,), (1,)), ((), ())),
                                preferred_element_type=F32) + bias
            m = jnp.max(s, axis=-1, keepdims=True)
            p = jnp.exp2(s - m).astype(BF16)
            o = jnp.dot(p, vx, preferred_element_type=F32)
            acc = jnp.where(first, o[0:wb, 0:LANES], o[wb:, 0:LANES])
            den = jnp.where(first, o[0:wb, LANES:], o[wb:, LANES:])
            mx = jnp.where(first, jnp.broadcast_to(m[0:wb], (wb, LANES)),
                           jnp.broadcast_to(m[wb:], (wb, LANES)))
            if dil == 1:
                dst = pl.ds(pl.multiple_of(cur, wb), wb)
            else:
                dst = pl.ds(cur, wb, stride=dil)
            acc_s[g, dst, :] = acc
            m_s[g, dst, :] = mx
            l_s[g, dst, :] = den
            return 0

        lax.fori_loop(0, n_q_blocks, body, 0, unroll=16)

    mr = 256
    for c in range(seq // mr):
        rs = slice(c * mr, (c + 1) * mr)
        m1, m2, m3 = m_s[0, rs, :], m_s[1, rs, :], m_s[2, rs, :]
        m = jnp.maximum(jnp.maximum(m1, m2), m3)
        w1, w2, w3 = jnp.exp2(m1 - m), jnp.exp2(m2 - m), jnp.exp2(m3 - m)
        num = w1 * acc_s[0, rs, :] + w2 * acc_s[1, rs, :] + w3 * acc_s[2, rs, :]
        den = w1 * l_s[0, rs, :] + w2 * l_s[1, rs, :] + w3 * l_s[2, rs, :]
        o_ref[rs, :] = (num / den).astype(o_ref.dtype)


def _dilated_attention(proj3, n_heads):
    bsz, seq, _ = proj3.shape
    n_pairs = n_heads * HEAD_DIM // LANES
    slopes = 2.0 ** (-8.0 * jnp.arange(1, n_heads + 1, dtype=F32) / n_heads)
    groups = len(DIL_PAIRS)

    def spec(blk0):
        return pl.BlockSpec((None, seq, LANES), lambda b, p: (b, 0, blk0 + p))

    q_specs = [spec(g * n_pairs) for g in range(groups)]
    k_specs = [spec((groups + g) * n_pairs) for g in range(groups)]
    v_spec = spec(2 * groups * n_pairs)
    return pl.pallas_call(
        functools.partial(_dil_kernel, seq=seq),
        out_shape=jax.ShapeDtypeStruct((bsz, seq, n_pairs * LANES), BF16),
        grid=(bsz, n_pairs),
        in_specs=[pl.BlockSpec(memory_space=pltpu.SMEM)] + q_specs + k_specs + [v_spec],
        out_specs=pl.BlockSpec((None, seq, LANES), lambda b, p: (b, 0, p)),
        scratch_shapes=[
            pltpu.VMEM((groups - 1, seq, LANES), F32),
            pltpu.VMEM((groups - 1, seq, LANES), F32),
            pltpu.VMEM((seq, LANES), F32),
            pltpu.VMEM((2, 2 * WIN_BLOCK, 2 * WIN_BLOCK), F32),
            pltpu.VMEM((groups, seq, LANES), F32),
            pltpu.VMEM((groups, seq, LANES), F32),
            pltpu.VMEM((groups, seq, LANES), F32),
        ],
        compiler_params=_params(("parallel", "parallel")),
        name="dilated_attention",
    )(slopes, *([proj3] * 7))


CONV_HALO = 32
CONV_ROWS = 64


def _conv_rows(buf, win, w_ref, cb_ref, lng_ref, lnb_ref, row0, rows):
    off = CONV_HALO - (CONV_K - 1)
    sub = CONV_SUBLANES
    rc = CONV_ROWS
    outs = []
    for c in range(rows // rc):
        base = row0 + c * rc + off
        acc = jnp.zeros((rc, buf.shape[1]), F32)
        for r in range(sub):
            span = rc + (CONV_K - 1 - r) // sub * sub
            win[c % 2, r, 0:span, :] = buf[base + r:base + r + span, :]
            for k in range(r, CONV_K, sub):
                acc = acc + w_ref[k:k + 1, :] * win[c % 2, r, k - r:k - r + rc, :]
        y = _layer_norm(acc + cb_ref[...], lng_ref[...], lnb_ref[...])
        outs.append(_silu(y).astype(BF16))
    return jnp.concatenate(outs, axis=0)


def _route(logits_t, rb):
    mx = jnp.max(logits_t, axis=0, keepdims=True)
    ex = jnp.exp(logits_t - mx)
    probs = ex / jnp.sum(ex, axis=0, keepdims=True)
    sel = probs + rb
    srow = [sel[i:i + 1, :] for i in range(N_EXPERTS)]
    prow = [probs[i:i + 1, :] for i in range(N_EXPERTS)]
    epg = EXPERTS_PER_GROUP
    n_grp = N_EXPERTS // epg

    scores = []
    for g in range(n_grp):
        a, b, c, d = srow[g * epg:(g + 1) * epg]
        hi1, lo1 = jnp.maximum(a, b), jnp.minimum(a, b)
        hi2, lo2 = jnp.maximum(c, d), jnp.minimum(c, d)
        scores.append(jnp.maximum(hi1, hi2)
                      + jnp.maximum(jnp.minimum(hi1, hi2), jnp.maximum(lo1, lo2)))
    best = scores[0]
    grp = jnp.zeros(best.shape, jnp.int32)
    for g in range(1, n_grp):
        better = scores[g] > best
        grp = jnp.where(better, g, grp)
        best = jnp.where(better, scores[g], best)

    def pick(rows_, j):
        out = rows_[j]
        for g in range(1, n_grp):
            out = jnp.where(grp == g, rows_[g * epg + j], out)
        return out

    ing = [pick(srow, j) for j in range(epg)]
    ping = [pick(prow, j) for j in range(epg)]
    b0, i0, p0 = ing[0], jnp.zeros(best.shape, jnp.int32), ping[0]
    for j in range(1, epg):
        better = ing[j] > b0
        i0 = jnp.where(better, j, i0)
        p0 = jnp.where(better, ping[j], p0)
        b0 = jnp.where(better, ing[j], b0)
    b1 = jnp.full(best.shape, -jnp.inf, F32)
    i1 = jnp.zeros(best.shape, jnp.int32)
    p1 = jnp.zeros(best.shape, F32)
    for j in range(epg):
        better = jnp.logical_and(i0 != j, ing[j] > b1)
        i1 = jnp.where(better, j, i1)
        p1 = jnp.where(better, ping[j], p1)
        b1 = jnp.where(better, ing[j], b1)
    den = p0 + p1
    return grp * epg + i0, grp * epg + i1, p0 / den, p1 / den


def _sort_tile(e0, e1, hb):
    st = hb.shape[0]
    eid = lax.broadcasted_iota(jnp.int32, (N_EXPERTS, st), 0)
    sel0 = jnp.where(eid == e0, 1.0, 0.0)
    sel1 = jnp.where(eid == e1, 1.0, 0.0)
    sel = sel0 + sel1
    before = (lax.broadcasted_iota(jnp.int32, (st, st), 0)
              < lax.broadcasted_iota(jnp.int32, (st, st), 1))
    rank = jnp.dot(sel.astype(BF16), jnp.where(before, 1.0, 0.0).astype(BF16),
                   preferred_element_type=F32)
    cnt = jnp.sum(sel, axis=1, keepdims=True)
    padded = jnp.ceil(cnt * (1.0 / PAGE)) * PAGE
    ecol = lax.broadcasted_iota(jnp.int32, (N_EXPERTS, 1), 0)
    seg = jnp.zeros((N_EXPERTS, 1), F32)
    run = jnp.zeros((1, 1), F32)
    for e in range(N_EXPERTS):
        seg = jnp.where(ecol == e, run, seg)
        run = run + padded[e:e + 1, :]
    base = seg + rank
    pos0 = jnp.sum(sel0 * base, axis=0, keepdims=True)
    pos1 = jnp.sum(sel1 * base, axis=0, keepdims=True)
    slot = lax.broadcasted_iota(jnp.int32, (SLOTS, st), 0)
    perm = (jnp.where(slot == pos0.astype(jnp.int32), 1.0, 0.0)
            + jnp.where(slot == pos1.astype(jnp.int32), 1.0, 0.0)).astype(BF16)
    rows = jnp.dot(perm, hb, preferred_element_type=F32).astype(BF16)
    return pos0, pos1, cnt, rows


def _outproj_tiles(a_tile, b_tile, x_ref, mod_ref, g_ref, w_ref, rw_ref, rb_ref,
                   xo_ref, xs_ref, meta_ref, cnt_ref):
    ka = w_ref.shape[0] // 2
    for c in range(x_ref.shape[0] // SORT_TILE):
        ts = slice(c * SORT_TILE, (c + 1) * SORT_TILE)
        y = (jnp.dot(a_tile(c), w_ref[0:ka, :], preferred_element_type=F32)
             + jnp.dot(b_tile(c), w_ref[ka:, :], preferred_element_type=F32))
        xn = x_ref[ts, :] + mod_ref[2:3, :] * y
        xo_ref[ts, :] = xn
        h = _rms(xn, g_ref[...]) * (1.0 + mod_ref[4:5, :]) + mod_ref[3:4, :]
        hb = h.astype(BF16)
        hl = (h - hb.astype(F32)).astype(BF16)
        parts = (jnp.dot(hb, rw_ref[...], preferred_element_type=F32)
                 + jnp.dot(hl, rw_ref[...], preferred_element_type=F32)).T
        lt = parts[0:N_EXPERTS, :] + parts[N_EXPERTS:2 * N_EXPERTS, :]
        e0, e1, g0, g1 = _route(lt, rb_ref[...])
        pos0, pos1, cnt, rows = _sort_tile(e0, e1, hb)
        xs_ref[c * SLOTS:(c + 1) * SLOTS, :] = rows
        meta_ref[c] = jnp.concatenate(
            [pos0, pos1, g0, g1, jnp.zeros((META_ROWS - 4, SORT_TILE), F32)], axis=0)
        cnt_ref[c] = jnp.broadcast_to(cnt, (N_EXPERTS, LANES))


def _tile_rows(ref):
    return lambda c: ref[c * SORT_TILE:(c + 1) * SORT_TILE, :]


def _outproj_sgu_kernel(u_ref, v_ref, lng_ref, lnb_ref, ws_ref, bias_ref, b_ref, *rest):
    u, v = _tile_rows(u_ref), _tile_rows(v_ref)
    _outproj_tiles(lambda c: _sgu_rows(u(c), v(c), lng_ref, lnb_ref, ws_ref, bias_ref),
                   _tile_rows(b_ref), *rest)


def _outproj_conv_kernel(a_ref, ca_ref, cg_ref, cw_ref, cb_ref, clg_ref, clb_ref, *rest,
                         per_b):
    *rest, buf, win = rest
    tm = ca_ref.shape[0]

    @pl.when(pl.program_id(0) % per_b == 0)
    def _():
        buf[0:CONV_HALO, :] = jnp.zeros((CONV_HALO, buf.shape[1]), F32)

    buf[CONV_HALO:, :] = ca_ref[...].astype(F32) * jax.nn.sigmoid(cg_ref[...].astype(F32))
    _outproj_tiles(
        _tile_rows(a_ref),
        lambda c: _conv_rows(buf, win, cw_ref, cb_ref, clg_ref, clb_ref, c * SORT_TILE, SORT_TILE),
        *rest)
    buf[0:CONV_HALO, :] = buf[tm:tm + CONV_HALO, :]


def _out_projection(a, b, x2, mod_l, g, w_bf, rw_split, rb, seq, sgu=None, conv=None, tm=1024):
    t, d = x2.shape
    ka = kb = w_bf.shape[0] // 2
    per_b = seq // tm
    sub = tm // SORT_TILE
    n_tiles = t // SORT_TILE
    vec = lambda v: v.reshape(1, -1)
    if conv is None:
        proj, ln_g, ln_b, w_s, b_s = sgu
        gd = ka // SGU_GROUPS
        small = pl.BlockSpec((1, ka), lambda i: (0, 0))
        kernel_fn, sem, scratch = _outproj_sgu_kernel, "parallel", []
        a_args = [proj, proj, vec(ln_g), vec(ln_b), w_s, jnp.repeat(b_s.T, gd, axis=1)]
        a_specs = [pl.BlockSpec((tm, ka), lambda i: (i, 0)),
                   pl.BlockSpec((tm, ka), lambda i: (i, 1)), small, small,
                   pl.BlockSpec((SGU_GROUPS, CHUNK, CHUNK), lambda i: (0, 0, 0)),
                   pl.BlockSpec((CHUNK, ka), lambda i: (0, 0))]
        b_args = [b]
        b_specs = [pl.BlockSpec((tm, kb), lambda i: (i, 0))]
    else:
        a_args = [a]
        a_specs = [pl.BlockSpec((tm, ka), lambda i: (i, 0))]
        proj, col0, cw, cb, clg, clb = conv
        small = pl.BlockSpec((1, kb), lambda i: (0, 0))
        kernel_fn = functools.partial(_outproj_conv_kernel, per_b=per_b)
        sem = "arbitrary"
        scratch = [pltpu.VMEM((tm + CONV_HALO, kb), F32),
                   pltpu.VMEM((2, CONV_SUBLANES, CONV_ROWS + CONV_HALO, kb), F32)]
        b_args = [proj, proj, cw, vec(cb), vec(clg), vec(clb)]
        b_specs = [pl.BlockSpec((tm, kb), lambda i: (i, col0 // kb)),
                   pl.BlockSpec((tm, kb), lambda i: (i, col0 // kb + 1)),
                   pl.BlockSpec((CONV_K, kb), lambda i: (0, 0)), small, small, small]
    return pl.pallas_call(
        kernel_fn,
        out_shape=(jax.ShapeDtypeStruct((t, d), F32),
                   jax.ShapeDtypeStruct((n_tiles * SLOTS, d), BF16),
                   jax.ShapeDtypeStruct((n_tiles, META_ROWS, SORT_TILE), F32),
                   jax.ShapeDtypeStruct((n_tiles, N_EXPERTS, LANES), F32)),
        grid=(t // tm,),
        in_specs=a_specs + b_specs + [
            pl.BlockSpec((tm, d), lambda i: (i, 0)),
            pl.BlockSpec((None, 6, d), lambda i: (i // per_b, 0, 0)),
            pl.BlockSpec((1, d), lambda i: (0, 0)),
            pl.BlockSpec((ka + kb, d), lambda i: (0, 0)),
            pl.BlockSpec((d, LANES), lambda i: (0, 0)),
            pl.BlockSpec((N_EXPERTS, 1), lambda i: (0, 0)),
        ],
        out_specs=(pl.BlockSpec((tm, d), lambda i: (i, 0)),
                   pl.BlockSpec((sub * SLOTS, d), lambda i: (i, 0)),
                   pl.BlockSpec((sub, META_ROWS, SORT_TILE), lambda i: (i, 0, 0)),
                   pl.BlockSpec((sub, N_EXPERTS, LANES), lambda i: (i, 0, 0))),
        scratch_shapes=scratch,
        compiler_params=_params((sem,)),
        name="sgu_out_projection" if conv is None else "conv_out_projection",
    )(*a_args, *b_args, x2, mod_l, g.reshape(1, d), w_bf, rw_split, rb.reshape(N_EXPERTS, 1))


def _expert_plan(counts, n_steps_max):
    n_tiles = counts.shape[0]
    n_groups = N_EXPERTS + 1
    i32 = jnp.int32
    npg = (counts + PAGE - 1) // PAGE
    npg = jnp.concatenate([npg, PAGES_PER_TILE - jnp.sum(npg, axis=1, keepdims=True)], axis=1)
    seg = jnp.cumsum(npg, axis=1) - npg
    page_base = jnp.arange(n_tiles, dtype=i32)[:, None] * PAGES_PER_TILE + seg
    cum_t = jnp.cumsum(npg, axis=0)
    tot = cum_t[-1]
    steps_e = (tot + STEP_PAGES - 1) // STEP_PAGES
    step_end = jnp.cumsum(steps_e)
    n_live = step_end[-1]
    g = jnp.arange(n_steps_max, dtype=i32)
    ex = jnp.minimum(jnp.sum((step_end[:, None] <= g[None, :]).astype(i32), axis=0),
                     n_groups - 1)
    sel = jnp.arange(n_groups, dtype=i32)[:, None] == ex[None, :]

    def of_step(per_group):
        return jnp.sum(jnp.where(sel, per_group[:, None], 0), axis=0)

    def rows_of_step(per_tile_group):
        return jnp.sum(jnp.where(sel[None], per_tile_group[:, :, None], 0), axis=1)

    q = ((g - of_step(step_end - steps_e))[:, None] * STEP_PAGES
         + jnp.arange(STEP_PAGES, dtype=i32)[None, :])
    valid = jnp.logical_and(q < of_step(tot)[:, None], (g < n_live)[:, None])
    cum_e = rows_of_step(cum_t)
    ti = jnp.minimum(jnp.sum((cum_e[:, :, None] <= q[None]).astype(i32), axis=0), n_tiles - 1)
    shift = rows_of_step(page_base - (cum_t - npg))
    hit = jnp.arange(n_tiles, dtype=i32)[:, None, None] == ti[None]
    pid = q + jnp.sum(jnp.where(hit, shift[:, :, None], 0), axis=0)
    spare = (n_tiles * PAGES_PER_TILE + (g % 2)[:, None] * STEP_PAGES
             + jnp.arange(STEP_PAGES, dtype=i32)[None, :])
    first = jnp.where(g < n_live, pid[:, 0], pid[0, 0])[:, None]
    in_pages = jnp.where(valid, pid, first).astype(i32).reshape(-1)
    out_pages = jnp.where(valid, pid, spare).astype(i32).reshape(-1)
    return in_pages, out_pages, ex.astype(i32), n_live.astype(i32).reshape(1)


def _expert_kernel(inp_ref, outp_ref, ex_ref, live_ref, xs_ref, wgu_ref, wd_ref, ys_ref,
                   wgu_bf, wd_bf, xbuf, ybuf, zbuf, in_sems, out_sems):
    g = pl.program_id(0)
    live = live_ref[0]
    slot = g % 2
    ff = wd_ref.shape[0]
    n_spare = 2 * STEP_PAGES
    half_rows = STEP_PAGES * PAGE // 2

    def is_expert_step(step):
        group = ex_ref[jnp.minimum(step, pl.num_programs(0) - 1)]
        return jnp.logical_and(step < live, group < N_EXPERTS)

    def in_copies(step, buf):
        return [pltpu.make_async_copy(xs_ref.at[inp_ref[step * STEP_PAGES + k]],
                                      xbuf.at[buf, pl.ds(k * PAGE, PAGE), :], in_sems.at[buf])
                for k in range(STEP_PAGES)]

    def out_copies(step, pages, zeros=False):
        return [pltpu.make_async_copy(
            zbuf.at[k] if zeros else ybuf.at[slot, pl.ds(k * PAGE, PAGE), :],
            ys_ref.at[outp_ref[step * STEP_PAGES + k]], out_sems.at[slot])
                for k in pages]

    @pl.when(g == 0)
    def _():
        zbuf[...] = jnp.zeros_like(zbuf)
        for j in range(n_spare // STEP_PAGES):
            first = ys_ref.shape[0] - n_spare + j * STEP_PAGES
            cp = pltpu.make_async_copy(zbuf, ys_ref.at[pl.ds(first, STEP_PAGES)], out_sems.at[1])
            cp.start()
            cp.wait()

        @pl.when(is_expert_step(0))
        def _():
            for cp in in_copies(0, 0):
                cp.start()

    @pl.when(jnp.logical_and(g >= 2, g - 2 < live))
    def _():
        for cp in out_copies(g - 2, range(STEP_PAGES)):
            cp.wait()

    group = ex_ref[g]
    is_expert = is_expert_step(g)
    new_expert = jnp.logical_or(g == 0, group != ex_ref[jnp.maximum(g - 1, 0)])

    @pl.when(jnp.logical_and(new_expert, is_expert))
    def _():
        wgu_bf[...] = wgu_ref[...].astype(BF16)
        wd_bf[...] = wd_ref[...].astype(BF16)

    @pl.when(is_expert)
    def _():
        for cp in in_copies(g, slot):
            cp.wait()

        @pl.when(is_expert_step(g + 1))
        def _():
            for cp in in_copies(g + 1, 1 - slot):
                cp.start()

        gu = jnp.dot(xbuf[slot], wgu_bf[...], preferred_element_type=F32)
        act = (_silu(gu[:, :ff]) * gu[:, ff:]).astype(BF16)
        for hlf in range(2):
            rows = slice(hlf * half_rows, (hlf + 1) * half_rows)
            ybuf[slot, rows, :] = jnp.dot(act[rows, :], wd_bf[...],
                                          preferred_element_type=F32).astype(BF16)
            pages = range(hlf * STEP_PAGES // 2, (hlf + 1) * STEP_PAGES // 2)
            for k, cp in zip(pages, out_copies(g, pages)):
                cp.start(priority=k % 2)

    @pl.when(jnp.logical_and(g < live, group == N_EXPERTS))
    def _():
        for cp in out_copies(g, range(STEP_PAGES), zeros=True):
            cp.start()


def _experts(xs_pages, plan, w_gu, w_down, layer, n_steps):
    n_pages, _, d = xs_pages.shape
    two_ff = w_gu.shape[-1]
    ff = two_ff // 2
    in_pages, out_pages, ex, live = plan
    grid_spec = pltpu.PrefetchScalarGridSpec(
        num_scalar_prefetch=4,
        grid=(n_steps,),
        in_specs=[
            pl.BlockSpec(memory_space=pl.ANY),
            pl.BlockSpec((None, None, d, two_ff),
                         lambda g, ip, op, e, nl: (layer, jnp.minimum(e[g], N_EXPERTS - 1), 0, 0)),
            pl.BlockSpec((None, None, ff, d),
                         lambda g, ip, op, e, nl: (layer, jnp.minimum(e[g], N_EXPERTS - 1), 0, 0)),
        ],
        out_specs=pl.BlockSpec(memory_space=pl.ANY),
        scratch_shapes=[pltpu.VMEM((d, two_ff), BF16), pltpu.VMEM((ff, d), BF16),
                        pltpu.VMEM((2, STEP_PAGES * PAGE, d), BF16),
                        pltpu.VMEM((2, STEP_PAGES * PAGE, d), BF16),
                        pltpu.VMEM((STEP_PAGES, PAGE, d), BF16),
                        pltpu.SemaphoreType.DMA((2,)),
                        pltpu.SemaphoreType.DMA((2,))],
    )
    return pl.pallas_call(
        _expert_kernel,
        out_shape=jax.ShapeDtypeStruct((n_pages + 2 * STEP_PAGES, PAGE, d), BF16),
        grid_spec=grid_spec,
        compiler_params=_params(("arbitrary",)),
        name="experts",
    )(in_pages, out_pages, ex, live, xs_pages, w_gu, w_down)


def _combine_kernel(ys_ref, meta_ref, x_ref, mod_ref, fg_ref, o_ref):
    for c in range(meta_ref.shape[0]):
        ts = slice(c * SORT_TILE, (c + 1) * SORT_TILE)
        y = _unsort(ys_ref[c * SLOTS:(c + 1) * SLOTS, :], meta_ref[c])
        o_ref[ts, :] = _rms(x_ref[ts, :] + mod_ref[5:6, :] * y, fg_ref[...])


def _final_combine(ys_rows, meta, x2, mod_l, final_g, seq, tm=1024):
    t, d = x2.shape
    per_b = seq // tm
    sub = tm // SORT_TILE
    return pl.pallas_call(
        _combine_kernel,
        out_shape=jax.ShapeDtypeStruct((t, d), F32),
        grid=(t // tm,),
        in_specs=[
            pl.BlockSpec((sub * SLOTS, d), lambda i: (i, 0)),
            pl.BlockSpec((sub, META_ROWS, SORT_TILE), lambda i: (i, 0, 0)),
            pl.BlockSpec((tm, d), lambda i: (i, 0)),
            pl.BlockSpec((None, 6, d), lambda i: (i // per_b, 0, 0)),
            pl.BlockSpec((1, d), lambda i: (0, 0)),
        ],
        out_specs=pl.BlockSpec((tm, d), lambda i: (i, 0)),
        compiler_params=_params(("parallel",)),
        name="moe_combine_final_norm",
    )(ys_rows, meta, x2, mod_l, final_g.reshape(1, d))


def _moe_experts(xs, counts, w_gu, w_down, layer):
    d = xs.shape[1]
    n_tiles = counts.shape[0]
    n_pages = n_tiles * PAGES_PER_TILE
    n_steps = n_pages // STEP_PAGES + N_EXPERTS + 1 + 2
    plan = _expert_plan(counts[:, :, 0].astype(jnp.int32), n_steps)
    ys = _experts(xs.reshape(n_pages, PAGE, d), plan, w_gu, w_down, layer, n_steps)
    return ys.reshape(-1, d)


def kernel(x, c, norm1_g, norm2_g, ada_w, ada_b, ev_w_in, ev_w_out, sgu_ln_g, sgu_ln_b, sgu_w, sgu_b, lam_q1, lam_k1, lam_q2, lam_k2, diff_subln_g, od_w_in, od_w_out, conv_w, conv_b, conv_ln_g, conv_ln_b, router_w, router_b, moe_w_gu, moe_w_down, final_g):
    bsz, seq, d = x.shape
    depth = ada_w.shape[0]
    half = d // 2
    diff_heads = half // (2 * HEAD_DIM)
    dil_heads = half // HEAD_DIM
    t = bsz * seq

    mod = _modulation(c, ada_w, ada_b)
    rw_hi = router_w.astype(BF16)
    rw_lo = (router_w - rw_hi.astype(F32)).astype(BF16)
    rw_split = jnp.pad(jnp.concatenate([rw_hi, rw_lo], axis=1),
                       ((0, 0), (0, LANES - 2 * N_EXPERTS)))

    x2 = x.reshape(t, d)
    moe = None
    for l in range(depth):
        i = l // 2
        w_in = (ev_w_in if l % 2 == 0 else od_w_in)[i].astype(BF16)
        x2, proj = _in_projection(x2, mod[l], norm1_g[l], w_in, seq, moe)
        if l % 2 == 0:
            lam_init = 0.8 - 0.6 * math.exp(-0.3 * l)
            part_a, conv = None, None
            sgu = (proj, sgu_ln_g[i], sgu_ln_b[i], sgu_w[i], sgu_b[i])
            part_b = _diff_attention(proj.reshape(bsz, seq, -1), lam_q1[i], lam_k1[i], lam_q2[i],
                                     lam_k2[i], diff_subln_g[i], lam_init, diff_heads,
                                     2 * half).reshape(t, half)
            w_out = ev_w_out[i]
        else:
            proj3 = proj.reshape(bsz, seq, -1)
            part_a = _dilated_attention(proj3, dil_heads).reshape(t, half)
            part_b, sgu = None, None
            conv = (proj, (2 * len(DIL_PAIRS) + 1) * half, conv_w[i], conv_b[i],
                    conv_ln_g[i], conv_ln_b[i])
            w_out = od_w_out[i]
        x2, xs, meta, counts = _out_projection(part_a, part_b, x2, mod[l], norm2_g[l],
                                               w_out.astype(BF16), rw_split, router_b, seq,
                                               sgu, conv)
        moe = (_moe_experts(xs, counts, moe_w_gu, moe_w_down, l), meta, mod[l])
    out = _final_combine(moe[0], moe[1], x2, mod[depth - 1], final_g, seq)
    return out.reshape(bsz, seq, d)
```
